```python
import jax, jax.numpy as jnp
from jax import lax
import numpy as np

D_MODEL = 2048
BATCH = 4
SEQ = 2048
DEPTH = 1
DEC_BATCH = 128
DEC_SEQ = 4
PAST_LEN = 16384
PAGE_SIZE = 128

GM_CHUNK = 128
GM_HEADS = 8
GM_HEAD_DIM = D_MODEL // (2 * GM_HEADS)
GM_WIDTH = GM_HEADS * GM_HEAD_DIM
SSD_HEAD_DIM = 64
SSD_WIDTH = D_MODEL
SSD_HEADS = SSD_WIDTH // SSD_HEAD_DIM
SSD_GROUPS = 4
SSD_HPG = SSD_HEADS // SSD_GROUPS
D_STATE = 128
CONV_W = 4
CONV_DIM = SSD_WIDTH + 2 * SSD_GROUPS * D_STATE
SSD_CHUNK = 128
N_BRANCH = 2
D_FF = 5632
IN_COLS = 2 * GM_WIDTH + SSD_WIDTH + CONV_DIM + SSD_HEADS + N_BRANCH * D_MODEL
EPS = 1e-6

kernel_name = "hybrid_gmlp_ssd_macaron_step"


def rmsnorm(x, w):
    xf = x.astype(jnp.float32)
    r = lax.rsqrt(jnp.mean(xf * xf, axis=-1, keepdims=True) + EPS)
    return (xf * r).astype(x.dtype) * w


def swiglu_ffn(x, norm_w, w1, w3, w2):
    h = rmsnorm(x, norm_w)
    return (jax.nn.silu(h @ w1) * (h @ w3)) @ w2


def causal_dwconv(xbc, conv_state, w, b):
    T = xbc.shape[1]
    xp = jnp.concatenate([conv_state.astype(xbc.dtype), xbc], axis=1)
    y = b + sum(w[k] * xp[:, k:k + T] for k in range(CONV_W))
    return jax.nn.silu(y), xp[:, T:]


def chunk_spatial_gating(u, v, ws, bs):
    b_, T = u.shape[:2]
    Lc = GM_CHUNK if T % GM_CHUNK == 0 else T
    nc = T // Lc
    w = jnp.tril(ws[:, :Lc, :Lc])
    vg = v.reshape(b_, nc, Lc, GM_HEADS, GM_HEAD_DIM)
    s = jnp.einsum('hls,bcshd->bclhd', w, vg) + bs[:, :Lc].T[None, None, :, :, None]
    return u * s.reshape(b_, T, GM_WIDTH)


def ssd_scan(x, dt, a, B, C, h0):
    f32 = jnp.float32
    b_, T = x.shape[:2]
    Lc = SSD_CHUNK if T % SSD_CHUNK == 0 else T
    nc = T // Lc
    xg = x.astype(f32).reshape(b_, nc, Lc, SSD_GROUPS, SSD_HPG, SSD_HEAD_DIM)
    dtg = dt.astype(f32).reshape(b_, nc, Lc, SSD_GROUPS, SSD_HPG)
    Bg = B.astype(f32).reshape(b_, nc, Lc, SSD_GROUPS, D_STATE)
    Cg = C.astype(f32).reshape(b_, nc, Lc, SSD_GROUPS, D_STATE)
    acum = jnp.cumsum(dtg * a.astype(f32).reshape(SSD_GROUPS, SSD_HPG), axis=2)
    xdt = xg * dtg[..., None]
    causal = jnp.tril(jnp.ones((Lc, Lc), dtype=bool))[None, None, :, :, None, None]
    seg = acum[:, :, :, None] - acum[:, :, None, :]
    decay = jnp.exp(jnp.where(causal, seg, -jnp.inf))
    cb = jnp.einsum('bclgn,bcsgn->bclsg', Cg, Bg)
    y_diag = jnp.einsum('bclsg,bclsgj,bcsgjp->bclgjp', cb, decay, xdt)
    to_end = jnp.exp(acum[:, :, -1:] - acum)
    s_chunk = jnp.einsum('bclgn,bclgj,bclgjp->bcgjpn', Bg, to_end, xdt)
    chunk_decay = jnp.exp(acum[:, :, -1])

    def step(h, inp):
        dec, s = inp
        return dec[..., None, None] * h + s, h

    h0g = h0.astype(f32).reshape(b_, SSD_GROUPS, SSD_HPG, SSD_HEAD_DIM, D_STATE)
    h_last, h_in = lax.scan(step, h0g, (jnp.moveaxis(chunk_decay, 1, 0), jnp.moveaxis(s_chunk, 1, 0)))
    h_in = jnp.moveaxis(h_in, 0, 1)
    y_off = jnp.einsum('bclgn,bcgjpn,bclgj->bclgjp', Cg, h_in, jnp.exp(acum))
    y = (y_diag + y_off).reshape(b_, T, SSD_HEADS, SSD_HEAD_DIM)
    return y, h_last.reshape(b_, SSD_HEADS, SSD_HEAD_DIM, D_STATE)


def decoder_layer(x, h0, conv0, ffn1_norm, ffn1_w1, ffn1_w3, ffn1_w2, mix_norm, w_in, b_gate,
                  conv_w, conv_b, dt_bias, a_log, d_skip, ssd_norm, gm_v_norm, gm_ws, gm_bs,
                  w_proj_a, w_proj_b, w_out, ffn2_norm, ffn2_w1, ffn2_w3, ffn2_w2):
    b_, T = x.shape[:2]
    x = x + 0.5 * swiglu_ffn(x, ffn1_norm, ffn1_w1, ffn1_w3, ffn1_w2)
    h = rmsnorm(x, mix_norm)
    proj = h @ w_in
    sizes = [GM_WIDTH, GM_WIDTH, SSD_WIDTH, CONV_DIM, SSD_HEADS, N_BRANCH * D_MODEL]
    cuts = [int(c) for c in np.cumsum(sizes)[:-1]]
    u, v, z, xbc, dt_raw, gates = jnp.split(proj, cuts, axis=-1)
    v = rmsnorm(jax.nn.gelu(v), gm_v_norm)
    y_a = chunk_spatial_gating(jax.nn.gelu(u), v, gm_ws, gm_bs)
    xbc, conv_new = causal_dwconv(xbc, conv0, conv_w, conv_b)
    xs, Bm, Cm = jnp.split(xbc, [SSD_WIDTH, SSD_WIDTH + SSD_GROUPS * D_STATE], axis=-1)
    dt = jax.nn.softplus((dt_raw + dt_bias).astype(jnp.float32))
    a = -jnp.exp(a_log.astype(jnp.float32))
    xh = xs.reshape(b_, T, SSD_HEADS, SSD_HEAD_DIM)
    y_ssd, h_new = ssd_scan(xh, dt, a, Bm.reshape(b_, T, SSD_GROUPS, D_STATE),
                            Cm.reshape(b_, T, SSD_GROUPS, D_STATE), h0)
    y_ssd = (y_ssd + d_skip[:, None].astype(jnp.float32) * xh.astype(jnp.float32)).astype(x.dtype)
    y_b = y_ssd.reshape(b_, T, SSD_WIDTH) * jax.nn.silu(z)
    y_b = rmsnorm(y_b.reshape(b_, T, SSD_GROUPS, SSD_WIDTH // SSD_GROUPS),
                  ssd_norm.reshape(SSD_GROUPS, SSD_WIDTH // SSD_GROUPS)).reshape(b_, T, SSD_WIDTH)
    g = jax.nn.sigmoid(gates.reshape(b_, T, N_BRANCH, D_MODEL) + b_gate)
    m = g[:, :, 0] * (y_a @ w_proj_a) + g[:, :, 1] * (y_b @ w_proj_b)
    x = x + m @ w_out
    x = x + 0.5 * swiglu_ffn(x, ffn2_norm, ffn2_w1, ffn2_w3, ffn2_w2)
    return x, h_new.astype(h0.dtype), conv_new, v


def setup_inputs(seed: int = 0) -> dict:
    key = jax.random.key(seed)
    ks = jax.random.split(key, 32)
    f32 = jnp.float32

    def nrm(k, shape, scale):
        return jax.random.normal(k, shape, f32) * scale

    def gain(k, n):
        return 1.0 + nrm(k, (DEPTH, n), 0.01)

    dt0 = jnp.exp(jax.random.uniform(ks[14], (DEPTH, SSD_HEADS), f32) * (np.log(0.1) - np.log(0.001)) + np.log(0.001))
    return {
        "x_prompt": nrm(ks[0], (BATCH, SEQ, D_MODEL), 1.0),
        "x_sample": nrm(ks[1], (DEC_BATCH, DEC_SEQ, D_MODEL), 1.0),
        "state_ssm": nrm(ks[2], (DEPTH, DEC_BATCH, SSD_HEADS, SSD_HEAD_DIM, D_STATE), 0.5),
        "state_conv": nrm(ks[3], (DEPTH, DEC_BATCH, CONV_W - 1, CONV_DIM), 1.0),
        "ffn1_norm": gain(ks[4], D_MODEL),
        "ffn1_w1": nrm(ks[5], (DEPTH, D_MODEL, D_FF), D_MODEL ** -0.5),
        "ffn1_w3": nrm(ks[6], (DEPTH, D_MODEL, D_FF), D_MODEL ** -0.5),
        "ffn1_w2": nrm(ks[7], (DEPTH, D_FF, D_MODEL), D_FF ** -0.5),
        "mix_norm": gain(ks[8], D_MODEL),
        "w_in": nrm(ks[9], (DEPTH, D_MODEL, IN_COLS), D_MODEL ** -0.5),
        "b_gate": nrm(ks[10], (DEPTH, N_BRANCH, D_MODEL), 0.1),
        "conv_w": nrm(ks[11], (DEPTH, CONV_W, CONV_DIM), CONV_W ** -0.5),
        "conv_b": nrm(ks[12], (DEPTH, CONV_DIM), 0.02),
        "dt_bias": jnp.log(jnp.expm1(dt0)),
        "a_log": jnp.log(jax.random.uniform(ks[15], (DEPTH, SSD_HEADS), f32, 1.0, 16.0)),
        "d_skip": 1.0 + nrm(ks[16], (DEPTH, SSD_HEADS), 0.1),
        "ssd_norm": gain(ks[17], SSD_WIDTH),
        "gm_v_norm": gain(ks[18], GM_WIDTH),
        "gm_ws": nrm(ks[19], (DEPTH, GM_HEADS, GM_CHUNK, GM_CHUNK), GM_CHUNK ** -0.5),
        "gm_bs": 1.0 + nrm(ks[20], (DEPTH, GM_HEADS, GM_CHUNK), 0.1),
        "w_proj_a": nrm(ks[21], (DEPTH, GM_WIDTH, D_MODEL), GM_WIDTH ** -0.5),
        "w_proj_b": nrm(ks[22], (DEPTH, SSD_WIDTH, D_MODEL), SSD_WIDTH ** -0.5),
        "w_out": nrm(ks[23], (DEPTH, D_MODEL, D_MODEL), D_MODEL ** -0.5),
        "ffn2_norm": gain(ks[24], D_MODEL),
        "ffn2_w1": nrm(ks[25], (DEPTH, D_MODEL, D_FF), D_MODEL ** -0.5),
        "ffn2_w3": nrm(ks[26], (DEPTH, D_MODEL, D_FF), D_MODEL ** -0.5),
        "ffn2_w2": nrm(ks[27], (DEPTH, D_FF, D_MODEL), D_FF ** -0.5),
        "final_norm": 1.0 + nrm(ks[28], (D_MODEL,), 0.01),
    }


def reference(x_prompt, x_sample, state_ssm, state_conv, ffn1_norm, ffn1_w1, ffn1_w3, ffn1_w2,
              mix_norm, w_in, b_gate, conv_w, conv_b, dt_bias, a_log, d_skip, ssd_norm,
              gm_v_norm, gm_ws, gm_bs, w_proj_a, w_proj_b, w_out, ffn2_norm, ffn2_w1, ffn2_w3,
              ffn2_w2, final_norm):
    yp, ys = x_prompt, x_sample
    bp = x_prompt.shape[0]
    ssm_p, conv_p, ssm_s, conv_s, v_s = [], [], [], [], []
    for l in range(DEPTH):
        lp = (ffn1_norm[l], ffn1_w1[l], ffn1_w3[l], ffn1_w2[l], mix_norm[l], w_in[l], b_gate[l],
              conv_w[l], conv_b[l], dt_bias[l], a_log[l], d_skip[l], ssd_norm[l], gm_v_norm[l],
              gm_ws[l], gm_bs[l], w_proj_a[l], w_proj_b[l], w_out[l], ffn2_norm[l], ffn2_w1[l],
              ffn2_w3[l], ffn2_w2[l])
        h0p = jnp.zeros((bp, SSD_HEADS, SSD_HEAD_DIM, D_STATE), state_ssm.dtype)
        c0p = jnp.zeros((bp, CONV_W - 1, CONV_DIM), x_prompt.dtype)
        yp, hp, cp, _ = decoder_layer(yp, h0p, c0p, *lp)
        ys, hs, cs, vs = decoder_layer(ys, state_ssm[l], state_conv[l], *lp)
        ssm_p.append(hp)
        conv_p.append(cp)
        ssm_s.append(hs)
        conv_s.append(cs)
        v_s.append(vs)
    y_prompt = rmsnorm(yp, final_norm)
    y_sample = rmsnorm(ys, final_norm)
    return (y_prompt, y_sample, jnp.stack(ssm_p), jnp.stack(conv_p), jnp.stack(ssm_s),
            jnp.stack(conv_s), jnp.stack(v_s))
```

```python
import functools

import jax
import jax.numpy as jnp
from jax import lax
from jax.experimental import pallas as pl
from jax.experimental.pallas import tpu as pltpu

F32 = jnp.float32
BF16 = jnp.bfloat16
EPS = 1e-6

LANES = 128
SUBLANES = 8
VMEM_LIMIT_BYTES = 56 * 1024 * 1024

GM_HEADS = 8
SSD_HEAD_DIM = 64
SSD_GROUPS = 4
D_STATE = 128
CONV_W = 4
TOKEN_BLOCK = 128
SAMPLE_SEQS_PER_STEP = 4


def _cparams(*sem):
    return pltpu.CompilerParams(dimension_semantics=sem, vmem_limit_bytes=VMEM_LIMIT_BYTES)


def _gelu(x):
    return x * (0.5 * (1.0 + jnp.tanh(0.7978845608028654 * (x + 0.044715 * (x * x * x)))))


def _silu(x):
    return x * jax.nn.sigmoid(x)


def _softplus(x):
    return jnp.maximum(x, 0.0) + jnp.log1p(jnp.exp(-jnp.abs(x)))


def _split3(a):
    a1 = a.astype(BF16)
    r1 = a - a1.astype(F32)
    a2 = r1.astype(BF16)
    r2 = r1 - a2.astype(F32)
    return a1, a2, r2.astype(BF16)


def _dot(a, b):
    return jnp.dot(a, b, preferred_element_type=F32)


def _dot_nt(a, b):
    return lax.dot_general(a, b, (((1,), (1,)), ((), ())), preferred_element_type=F32)


def _sel_right(a, sel):
    a1, a2, a3 = _split3(a)
    return (_dot(a1, sel) + _dot(a2, sel)) + _dot(a3, sel)


def _sel_left(sel, a):
    a1, a2, a3 = _split3(a)
    return (_dot(sel, a1) + _dot(sel, a2)) + _dot(sel, a3)


def _rmsnorm_rows(x_ref, w_ref, out_ref, rows):
    n = x_ref.shape[0] // rows

    def body(i, c):
        sl = pl.ds(pl.multiple_of(i * rows, rows), rows)
        x = x_ref[sl, :]
        r = lax.rsqrt(jnp.mean(x * x, axis=-1, keepdims=True) + EPS)
        out_ref[sl, :] = ((x * r) * w_ref[...]).astype(out_ref.dtype)
        return c

    lax.fori_loop(0, n, body, 0)


def _ffn_body(x_ref, nw_ref, w1_ref, w3_ref, w2_ref, fn_ref, o_ref, xn_ref, acc_ref, *, final_norm):
    j = pl.program_id(1)

    @pl.when(j == 0)
    def _():
        _rmsnorm_rows(x_ref, nw_ref, xn_ref, 64)

    xn = xn_ref[...]
    h1 = _dot(xn, w1_ref[...])
    h3 = _dot(xn, w3_ref[...])
    g = (_silu(h1) * h3).astype(BF16)
    part = _dot(g, w2_ref[...])

    @pl.when(j == 0)
    def _():
        acc_ref[...] = part

    @pl.when(j > 0)
    def _():
        acc_ref[...] += part

    @pl.when(j == pl.num_programs(1) - 1)
    def _():
        rows = 64
        n = x_ref.shape[0] // rows

        def body(i, c):
            sl = pl.ds(pl.multiple_of(i * rows, rows), rows)
            y = x_ref[sl, :] + 0.5 * acc_ref[sl, :]
            if final_norm:
                r = lax.rsqrt(jnp.mean(y * y, axis=-1, keepdims=True) + EPS)
                y = (y * r) * fn_ref[...]
            o_ref[sl, :] = y
            return c

        lax.fori_loop(0, n, body, 0)


def _ffn(x, norm_w, w1, w3, w2, final_w, *, tm, tf):
    n, d = x.shape
    f = w1.shape[1]
    final_norm = final_w is not None
    fw = final_w if final_norm else norm_w
    return pl.pallas_call(
        functools.partial(_ffn_body, final_norm=final_norm),
        grid=(n // tm, f // tf),
        in_specs=[
            pl.BlockSpec((tm, d), lambda i, j: (i, 0)),
            pl.BlockSpec((1, d), lambda i, j: (0, 0)),
            pl.BlockSpec((d, tf), lambda i, j: (0, j)),
            pl.BlockSpec((d, tf), lambda i, j: (0, j)),
            pl.BlockSpec((tf, d), lambda i, j: (j, 0)),
            pl.BlockSpec((1, d), lambda i, j: (0, 0)),
        ],
        out_specs=pl.BlockSpec((tm, d), lambda i, j: (i, 0)),
        out_shape=jax.ShapeDtypeStruct((n, d), F32),
        scratch_shapes=[pltpu.VMEM((tm, d), BF16), pltpu.VMEM((tm, d), F32)],
        compiler_params=_cparams("arbitrary", "arbitrary"),
        name="ffn",
    )(x, norm_w.reshape(1, d), w1, w3, w2, fw.reshape(1, d))


def _inproj_body(x_ref, nw_ref, w_ref, o_ref, xn_ref):
    @pl.when(pl.program_id(1) == 0)
    def _():
        _rmsnorm_rows(x_ref, nw_ref, xn_ref, 64)

    o_ref[...] = _dot(xn_ref[...], w_ref[...]).astype(o_ref.dtype)


def _inproj(x, norm_w, w, out_dtype, *, tm, tn):
    n, d = x.shape
    c = w.shape[1]
    return pl.pallas_call(
        _inproj_body,
        grid=(n // tm, c // tn),
        in_specs=[
            pl.BlockSpec((tm, d), lambda i, j: (i, 0)),
            pl.BlockSpec((1, d), lambda i, j: (0, 0)),
            pl.BlockSpec((d, tn), lambda i, j: (0, j)),
        ],
        out_specs=pl.BlockSpec((tm, tn), lambda i, j: (i, j)),
        out_shape=jax.ShapeDtypeStruct((n, c), out_dtype),
        scratch_shapes=[pltpu.VMEM((tm, d), BF16)],
        compiler_params=_cparams("arbitrary", "arbitrary"),
        name="in_proj",
    )(x, norm_w.reshape(1, d), w)


def _merge_body(x_ref, ya_ref, yb_ref, ga_ref, gb_ref, bg_ref, wa_ref, wb_ref, wo_ref, o_ref):
    pa = _dot(ya_ref[...], wa_ref[...])
    pb = _dot(yb_ref[...], wb_ref[...])
    ga = jax.nn.sigmoid(ga_ref[...].astype(F32) + bg_ref[0:1, :])
    gb = jax.nn.sigmoid(gb_ref[...].astype(F32) + bg_ref[1:2, :])
    m = (ga * pa + gb * pb).astype(BF16)
    o_ref[...] = x_ref[...] + _dot(m, wo_ref[...])


def _merge(x, ya, yb, p1, b_gate, wa, wb, wo, *, tm):
    n, d = x.shape
    gcol = p1.shape[1] // d - 2
    const = dict(pipeline_mode=pl.Buffered(1))
    return pl.pallas_call(
        _merge_body,
        grid=(n // tm,),
        in_specs=[
            pl.BlockSpec((tm, d), lambda i: (i, 0)),
            pl.BlockSpec((tm, ya.shape[1]), lambda i: (i, 0)),
            pl.BlockSpec((tm, yb.shape[1]), lambda i: (i, 0)),
            pl.BlockSpec((tm, d), lambda i: (i, gcol)),
            pl.BlockSpec((tm, d), lambda i: (i, gcol + 1)),
            pl.BlockSpec((2, d), lambda i: (0, 0)),
            pl.BlockSpec(wa.shape, lambda i: (0, 0), **const),
            pl.BlockSpec(wb.shape, lambda i: (0, 0), **const),
            pl.BlockSpec(wo.shape, lambda i: (0, 0), **const),
        ],
        out_specs=pl.BlockSpec((tm, d), lambda i: (i, 0)),
        out_shape=jax.ShapeDtypeStruct((n, d), F32),
        compiler_params=_cparams("arbitrary"),
        name="merge",
    )(x, ya, yb, p1, p1, b_gate, wa, wb, wo)


def _iota2(shape, dim):
    return lax.broadcasted_iota(jnp.int32, shape, dim)


def _block_masks(seq_len):
    tb = TOKEN_BLOCK
    row = _iota2((tb, tb), 0)
    col = _iota2((tb, tb), 1)
    if seq_len >= tb:
        causal = col <= row
        last = col == tb - 1
    else:
        causal = (col <= row) & ((row // seq_len) == (col // seq_len))
        last = col == (row // seq_len) * seq_len + (seq_len - 1)
    return causal, last


def _init_constants(wm_ref, e_ref, gw_ref, causal):
    for h in range(GM_HEADS):
        wm_ref[h] = jnp.where(causal, gw_ref[h], 0.0).astype(BF16)
    hrow = _iota2(e_ref.shape, 0)
    ccol = _iota2(e_ref.shape, 1)
    e_ref[...] = jnp.where(hrow == ccol // SSD_HEAD_DIM, 1.0, 0.0).astype(BF16)


def _gating(u_ref, v_ref, gvn_ref, bs_ref, wm_ref, ya_ref, vn_ref):
    vg = _gelu(v_ref[...].astype(F32))
    r = lax.rsqrt(jnp.mean(vg * vg, axis=-1, keepdims=True) + EPS)
    vn = (vg * r) * gvn_ref[...]
    if vn_ref is not None:
        vn_ref[...] = vn
    vnb = vn.astype(BF16)
    for h in range(GM_HEADS):
        cs = slice(h * LANES, (h + 1) * LANES)
        s = _dot(wm_ref[h], vnb[:, cs]) + bs_ref[:, h:h + 1]
        ya_ref[:, cs] = (_gelu(u_ref[:, cs].astype(F32)) * s).astype(ya_ref.dtype)


def _conv(xp_ref, cw_ref, cb_ref, xc_ref, halo_refs, seq_len):
    tb = TOKEN_BLOCK
    width = xc_ref.shape[1]
    cw = 512
    tpos = _iota2((tb, 1), 0) % seq_len if seq_len < tb else None
    for c0 in range(0, width, cw):
        cs = slice(c0, c0 + cw)
        acc = None
        for k in range(CONV_W):
            d = CONV_W - 1 - k
            xs = xp_ref[SUBLANES - d:SUBLANES - d + tb, cs]
            if tpos is not None and d > 0:
                xs = jnp.where(tpos >= d, xs, 0.0) + halo_refs[d - 1][:, cs]
            term = cw_ref[k:k + 1, cs] * xs
            acc = term if acc is None else acc + term
        xc_ref[:, cs] = _silu(cb_ref[:, cs] + acc)


def _ssd_prepare(dt_ref, dtb_ref, alog_ref, e_ref, causal, last, dtf_ref, tef_ref, eaf_ref):
    dt = _softplus(dt_ref[...] + dtb_ref[...])
    a = -jnp.exp(alog_ref[...])
    da = dt * a
    lmat = jnp.where(causal, 1.0, 0.0).astype(BF16)
    acum = _sel_left(lmat, da)
    a_last = _sel_left(jnp.where(last, 1.0, 0.0).astype(BF16), acum)
    ea = jnp.exp(acum)
    e = e_ref[...]
    dtf_ref[...] = _sel_right(dt, e)
    tef_ref[...] = _sel_right(dt * jnp.exp(a_last - acum), e)
    eaf_ref[...] = _sel_right(ea, e)
    return acum, acum.T


def _ssd_diag_group(g, xc_ref, dtf_ref, acum, acum_t, causal):
    tb = TOKEN_BLOCK
    width = xc_ref.shape[1] - 2 * SSD_GROUPS * D_STATE
    gw = width // SSD_GROUPS
    bg = xc_ref[:, width + g * D_STATE:width + (g + 1) * D_STATE]
    cg = xc_ref[:, width + (SSD_GROUPS + g) * D_STATE:width + (SSD_GROUPS + g + 1) * D_STATE].astype(BF16)
    cb = _dot_nt(cg, bg.astype(BF16))
    gs = slice(g * gw, (g + 1) * gw)
    xdt = (xc_ref[:, gs] * dtf_ref[:, gs]).astype(BF16)
    lane = _iota2((tb, LANES), 1)
    heads_per_group = gw // SSD_HEAD_DIM
    outs = []
    for jp in range(heads_per_group // 2):
        ms = []
        for hh in range(2):
            h = g * heads_per_group + 2 * jp + hh
            seg = acum[:, h:h + 1] - acum_t[h:h + 1, :]
            dec = jnp.exp(jnp.where(causal, seg, -jnp.inf))
            ms.append((cb * dec).astype(BF16))
        lhs = jnp.concatenate(ms, axis=1)
        xpair = xdt[:, jp * LANES:(jp + 1) * LANES]
        zero = jnp.zeros_like(xpair)
        rhs = jnp.concatenate([jnp.where(lane < SSD_HEAD_DIM, xpair, zero),
                               jnp.where(lane >= SSD_HEAD_DIM, xpair, zero)], axis=0)
        outs.append(_dot(lhs, rhs))
    return jnp.concatenate(outs, axis=1), cg, bg


def _finish_group(g, y, xc_ref, z_ref, dsk_ref, sn_ref, yb_ref):
    gw = yb_ref.shape[1] // SSD_GROUPS
    gs = slice(g * gw, (g + 1) * gw)
    y = y + dsk_ref[:, gs] * xc_ref[:, gs]
    yz = y * _silu(z_ref[:, gs].astype(F32))
    r = lax.rsqrt(jnp.mean(yz * yz, axis=-1, keepdims=True) + EPS)
    yb_ref[:, gs] = ((yz * r) * sn_ref[:, gs]).astype(yb_ref.dtype)


def _prompt_body(u_ref, v_ref, z_ref, xbc_ref, dt_ref, gvn_ref, gw_ref, bs_ref, cw_ref, cb_ref,
                 dtb_ref, alog_ref, dsk_ref, sn_ref,
                 ya_ref, yb_ref, hs_ref,
                 wm_ref, e_ref, xp_ref, xc_ref, dtf_ref, tef_ref, eaf_ref, st_ref):
    tb = TOKEN_BLOCK
    c = pl.program_id(1)
    causal, last = _block_masks(tb)

    @pl.when((pl.program_id(0) == 0) & (c == 0))
    def _():
        _init_constants(wm_ref, e_ref, gw_ref, causal)

    @pl.when(c == 0)
    def _():
        st_ref[...] = jnp.zeros_like(st_ref)
        xp_ref[0:SUBLANES, :] = jnp.zeros((SUBLANES, xp_ref.shape[1]), F32)

    _gating(u_ref, v_ref, gvn_ref, bs_ref, wm_ref, ya_ref, None)

    xp_ref[SUBLANES:SUBLANES + tb, :] = xbc_ref[...]
    _conv(xp_ref, cw_ref, cb_ref, xc_ref, None, tb)
    xp_ref[0:SUBLANES, :] = xbc_ref[tb - SUBLANES:tb, :]

    acum, acum_t = _ssd_prepare(dt_ref, dtb_ref, alog_ref, e_ref, causal, last, dtf_ref, tef_ref, eaf_ref)

    gw = yb_ref.shape[1] // SSD_GROUPS
    for g in range(SSD_GROUPS):
        gs = slice(g * gw, (g + 1) * gw)
        yd, cg, bg = _ssd_diag_group(g, xc_ref, dtf_ref, acum, acum_t, causal)
        st = st_ref[:, gs]
        y = yd + _dot(cg, st.astype(BF16)) * eaf_ref[:, gs]
        _finish_group(g, y, xc_ref, z_ref, dsk_ref, sn_ref, yb_ref)
        xw = (xc_ref[:, gs] * tef_ref[:, gs]).astype(BF16)
        st_ref[:, gs] = eaf_ref[tb - 1:tb, gs] * st + _dot(bg.T.astype(BF16), xw)

    @pl.when(c == pl.num_programs(1) - 1)
    def _():
        for k in range(st_ref.shape[1] // LANES):
            hs_ref[k * LANES:(k + 1) * LANES, :] = st_ref[:, k * LANES:(k + 1) * LANES].T


def _sample_body(u_ref, v_ref, z_ref, xbc_ref, dt_ref, h1_ref, h2_ref, h3_ref, h0_ref,
                 gvn_ref, gw_ref, bs_ref, cw_ref, cb_ref, dtb_ref, alog_ref, dsk_ref, sn_ref,
                 ya_ref, yb_ref, vn_ref, hs_ref,
                 wm_ref, e_ref, et_ref, xp_ref, xc_ref, dtf_ref, tef_ref, eaf_ref,
                 y_ref, yoff_ref, cg_ref, bg_ref, xwt_ref, eat_ref, *, seq_len):
    tb = TOKEN_BLOCK
    k = pl.program_id(1)
    causal, last = _block_masks(seq_len)
    gw = yb_ref.shape[1] // SSD_GROUPS
    width = SSD_GROUPS * gw

    @pl.when((pl.program_id(0) == 0) & (k == 0))
    def _():
        _init_constants(wm_ref, e_ref, gw_ref, causal)
        crow = _iota2(et_ref.shape, 0)
        hcol = _iota2(et_ref.shape, 1)
        et_ref[...] = jnp.where(crow // SSD_HEAD_DIM == hcol, 1.0, 0.0).astype(BF16)
        xp_ref[0:SUBLANES, :] = jnp.zeros((SUBLANES, xp_ref.shape[1]), F32)

    @pl.when(k == 0)
    def _():
        _gating(u_ref, v_ref, gvn_ref, bs_ref, wm_ref, ya_ref, vn_ref)
        xp_ref[SUBLANES:SUBLANES + tb, :] = xbc_ref[...]
        _conv(xp_ref, cw_ref, cb_ref, xc_ref, (h1_ref, h2_ref, h3_ref), seq_len)
        acum, acum_t = _ssd_prepare(dt_ref, dtb_ref, alog_ref, e_ref, causal, last, dtf_ref, tef_ref, eaf_ref)
        eat_ref[...] = jnp.exp(acum_t)
        for g in range(SSD_GROUPS):
            gs = slice(g * gw, (g + 1) * gw)
            yd, cg, bg = _ssd_diag_group(g, xc_ref, dtf_ref, acum, acum_t, causal)
            y_ref[:, gs] = yd
            cg_ref[:, g * D_STATE:(g + 1) * D_STATE] = cg.astype(F32)
            bg_ref[:, g * D_STATE:(g + 1) * D_STATE] = bg
            xw = xc_ref[:, gs] * tef_ref[:, gs]
            for q in range(gw // LANES):
                r0 = g * gw + q * LANES
                xwt_ref[r0:r0 + LANES, :] = xw[:, q * LANES:(q + 1) * LANES].T.astype(BF16)
        yoff_ref[...] = jnp.zeros_like(yoff_ref)

    nseq = h0_ref.shape[0]
    rowseq = _iota2((tb, 1), 0) // seq_len
    tok = _iota2((tb, LANES), 0)
    et = et_ref[...]
    for b in range(nseq):
        sb = k * nseq + b
        rmask = rowseq == sb
        onehot = jnp.where(tok == sb * seq_len + (seq_len - 1), 1.0, 0.0).astype(BF16)
        cd_heads = _sel_right(eat_ref[...], onehot)
        cd = _sel_left(et, cd_heads)
        for g in range(SSD_GROUPS):
            gs = slice(g * gw, (g + 1) * gw)
            ds = slice(g * D_STATE, (g + 1) * D_STATE)
            h0 = h0_ref[b, gs, :]
            cm = jnp.where(rmask, cg_ref[:, ds], 0.0).astype(BF16)
            yoff_ref[:, gs] += _dot_nt(cm, h0.astype(BF16))
            bm = jnp.where(rmask, bg_ref[:, ds], 0.0).astype(BF16)
            hs_ref[b, gs, :] = cd[gs, :] * h0 + _dot(xwt_ref[gs, :], bm)

    @pl.when(k == pl.num_programs(1) - 1)
    def _():
        for g in range(SSD_GROUPS):
            gs = slice(g * gw, (g + 1) * gw)
            y = y_ref[:, gs] + yoff_ref[:, gs] * eaf_ref[:, gs]
            _finish_group(g, y, xc_ref, z_ref, dsk_ref, sn_ref, yb_ref)


def _mixer_params(lp):
    (gm_v_norm, gm_ws, gm_bs, conv_w, conv_b, dt_bias, a_log, d_skip, ssd_norm) = lp
    nh = dt_bias.shape[0]
    pad = LANES - nh
    return dict(
        gvn=gm_v_norm.reshape(1, -1),
        cw=conv_w,
        cb=conv_b.reshape(1, -1),
        dtb=jnp.pad(dt_bias, (0, pad)).reshape(1, LANES),
        alog=jnp.pad(a_log, (0, pad)).reshape(1, LANES),
        dsk=jnp.repeat(d_skip, SSD_HEAD_DIM).reshape(1, -1),
        sn=ssd_norm.reshape(1, -1),
    )


def _full_spec(a):
    nd = a.ndim
    return pl.BlockSpec(a.shape, lambda i, j: (0,) * nd)


def _mixer_prompt(p1, p2, lp, n_seq, seq_len, gm_w, ssd_w, conv_dim):
    tb = TOKEN_BLOCK
    nc = seq_len // tb
    n = p1.shape[0]
    mp = _mixer_params(lp)
    gw_full = lp[1]
    bs_t = lp[2].T
    row = lambda b, c: b * nc + c
    params = [mp["gvn"], gw_full, bs_t, mp["cw"], mp["cb"], mp["dtb"], mp["alog"], mp["dsk"], mp["sn"]]
    return pl.pallas_call(
        _prompt_body,
        grid=(n_seq, nc),
        in_specs=[
            pl.BlockSpec((tb, gm_w), lambda b, c: (row(b, c), 0)),
            pl.BlockSpec((tb, gm_w), lambda b, c: (row(b, c), 1)),
            pl.BlockSpec((tb, ssd_w), lambda b, c: (row(b, c), 1)),
            pl.BlockSpec((tb, conv_dim), lambda b, c: (row(b, c), 0)),
            pl.BlockSpec((tb, LANES), lambda b, c: (row(b, c), conv_dim // LANES)),
        ] + [_full_spec(a) for a in params],
        out_specs=[
            pl.BlockSpec((tb, gm_w), lambda b, c: (row(b, c), 0)),
            pl.BlockSpec((tb, ssd_w), lambda b, c: (row(b, c), 0)),
            pl.BlockSpec((None, ssd_w, D_STATE), lambda b, c: (b, 0, 0)),
        ],
        out_shape=[
            jax.ShapeDtypeStruct((n_seq * seq_len, gm_w), BF16),
            jax.ShapeDtypeStruct((n_seq * seq_len, ssd_w), BF16),
            jax.ShapeDtypeStruct((n_seq, ssd_w, D_STATE), F32),
        ],
        scratch_shapes=[
            pltpu.VMEM((GM_HEADS, tb, tb), BF16),
            pltpu.VMEM((LANES, ssd_w), BF16),
            pltpu.VMEM((SUBLANES + tb, conv_dim), F32),
            pltpu.VMEM((tb, conv_dim), F32),
            pltpu.VMEM((tb, ssd_w), F32),
            pltpu.VMEM((tb, ssd_w), F32),
            pltpu.VMEM((tb, ssd_w), F32),
            pltpu.VMEM((D_STATE, ssd_w), F32),
        ],
        compiler_params=_cparams("arbitrary", "arbitrary"),
        name="mixer_prompt",
    )(p1, p1, p1, p2, p2, *params)


def _mixer_sample(p1, p2, lp, h0, halos, row0, n_seq, seq_len, gm_w, ssd_w, conv_dim):
    tb = TOKEN_BLOCK
    seqs_per_block = tb // seq_len
    nblk = n_seq // seqs_per_block
    nsub = seqs_per_block // SAMPLE_SEQS_PER_STEP
    blk0 = row0 // tb
    mp = _mixer_params(lp)
    reps = tb // seq_len
    gw_tiled = jnp.tile(lp[1][:, :seq_len, :seq_len], (1, reps, reps))
    bs_t = jnp.tile(lp[2][:, :seq_len], (1, reps)).T
    params = [mp["gvn"], gw_tiled, bs_t, mp["cw"], mp["cb"], mp["dtb"], mp["alog"], mp["dsk"], mp["sn"]]
    ntok = n_seq * seq_len
    return pl.pallas_call(
        functools.partial(_sample_body, seq_len=seq_len),
        grid=(nblk, nsub),
        in_specs=[
            pl.BlockSpec((tb, gm_w), lambda i, k: (blk0 + i, 0)),
            pl.BlockSpec((tb, gm_w), lambda i, k: (blk0 + i, 1)),
            pl.BlockSpec((tb, ssd_w), lambda i, k: (blk0 + i, 1)),
            pl.BlockSpec((tb, conv_dim), lambda i, k: (blk0 + i, 0)),
            pl.BlockSpec((tb, LANES), lambda i, k: (blk0 + i, conv_dim // LANES)),
            pl.BlockSpec((tb, conv_dim), lambda i, k: (i, 0)),
            pl.BlockSpec((tb, conv_dim), lambda i, k: (i, 0)),
            pl.BlockSpec((tb, conv_dim), lambda i, k: (i, 0)),
            pl.BlockSpec((SAMPLE_SEQS_PER_STEP, ssd_w, D_STATE), lambda i, k: (i * nsub + k, 0, 0)),
        ] + [_full_spec(a) for a in params],
        out_specs=[
            pl.BlockSpec((tb, gm_w), lambda i, k: (i, 0)),
            pl.BlockSpec((tb, ssd_w), lambda i, k: (i, 0)),
            pl.BlockSpec((tb, gm_w), lambda i, k: (i, 0)),
            pl.BlockSpec((SAMPLE_SEQS_PER_STEP, ssd_w, D_STATE), lambda i, k: (i * nsub + k, 0, 0)),
        ],
        out_shape=[
            jax.ShapeDtypeStruct((ntok, gm_w), BF16),
            jax.ShapeDtypeStruct((ntok, ssd_w), BF16),
            jax.ShapeDtypeStruct((ntok, gm_w), F32),
            jax.ShapeDtypeStruct((n_seq, ssd_w, D_STATE), F32),
        ],
        scratch_shapes=[
            pltpu.VMEM((GM_HEADS, tb, tb), BF16),
            pltpu.VMEM((LANES, ssd_w), BF16),
            pltpu.VMEM((ssd_w, LANES), BF16),
            pltpu.VMEM((SUBLANES + tb, conv_dim), F32),
            pltpu.VMEM((tb, conv_dim), F32),
            pltpu.VMEM((tb, ssd_w), F32),
            pltpu.VMEM((tb, ssd_w), F32),
            pltpu.VMEM((tb, ssd_w), F32),
            pltpu.VMEM((tb, ssd_w), F32),
            pltpu.VMEM((tb, ssd_w), F32),
            pltpu.VMEM((tb, SSD_GROUPS * D_STATE), F32),
            pltpu.VMEM((tb, SSD_GROUPS * D_STATE), F32),
            pltpu.VMEM((ssd_w, tb), BF16),
            pltpu.VMEM((LANES, tb), F32),
        ],
        compiler_params=_cparams("arbitrary", "arbitrary"),
        name="mixer_sample",
    )(p1, p1, p1, p2, p2, *halos, h0, *params)


def kernel(x_prompt, x_sample, state_ssm, state_conv, ffn1_norm, ffn1_w1, ffn1_w3, ffn1_w2, mix_norm, w_in,
           b_gate, conv_w, conv_b, dt_bias, a_log, d_skip, ssd_norm, gm_v_norm, gm_ws, gm_bs, w_proj_a,
           w_proj_b, w_out, ffn2_norm, ffn2_w1, ffn2_w3, ffn2_w2, final_norm):
    bp, tp, d = x_prompt.shape
    bs_, ts, _ = x_sample.shape
    depth = w_in.shape[0]
    n_heads = dt_bias.shape[1]
    gm_w = gm_v_norm.shape[1]
    ssd_w = ssd_norm.shape[1]
    conv_dim = conv_w.shape[2]
    np_, ns = bp * tp, bs_ * ts
    tm = 512

    x = jnp.concatenate([x_prompt.reshape(np_, d), x_sample.reshape(ns, d)], axis=0)
    ssm_p, conv_p, ssm_s, conv_s, v_s = [], [], [], [], []
    for l in range(depth):
        c_xbc = 2 * gm_w + ssd_w
        c_dt = c_xbc + conv_dim
        c_gate = c_dt + n_heads
        wl = w_in[l]
        w_a = jnp.concatenate([wl[:, :c_xbc], wl[:, c_gate:]], axis=1).astype(BF16)
        dt_pad = LANES - n_heads
        w_b = jnp.pad(wl[:, c_xbc:c_gate], ((0, 0), (0, dt_pad))).astype(BF16)

        x = _ffn(x, ffn1_norm[l], ffn1_w1[l].astype(BF16), ffn1_w3[l].astype(BF16), ffn1_w2[l].astype(BF16),
                 None, tm=tm, tf=512)
        p1 = _inproj(x, mix_norm[l], w_a, BF16, tm=tm, tn=1024)
        p2 = _inproj(x, mix_norm[l], w_b, F32, tm=tm, tn=640)

        lp = (gm_v_norm[l], gm_ws[l], gm_bs[l], conv_w[l], conv_b[l], dt_bias[l], a_log[l], d_skip[l], ssd_norm[l])
        ya_p, yb_p, hs_p = _mixer_prompt(p1, p2, lp, bp, tp, gm_w, ssd_w, conv_dim)

        st = state_conv[l]
        halos = []
        for dd in range(1, CONV_W):
            hz = jnp.concatenate([st[:, CONV_W - 1 - dd:, :], jnp.zeros((bs_, ts - dd, conv_dim), F32)], axis=1)
            halos.append(hz.reshape(ns, conv_dim))
        ya_s, yb_s, vn_s, hs_s = _mixer_sample(p1, p2, lp, state_ssm[l].reshape(bs_, ssd_w, D_STATE), halos,
                                               np_, bs_, ts, gm_w, ssd_w, conv_dim)

        ya = jnp.concatenate([ya_p, ya_s], axis=0)
        yb = jnp.concatenate([yb_p, yb_s], axis=0)
        x = _merge(x, ya, yb, p1, b_gate[l], w_proj_a[l].astype(BF16), w_proj_b[l].astype(BF16),
                   w_out[l].astype(BF16), tm=256)
        x = _ffn(x, ffn2_norm[l], ffn2_w1[l].astype(BF16), ffn2_w3[l].astype(BF16), ffn2_w2[l].astype(BF16),
                 final_norm if l == depth - 1 else None, tm=tm, tf=512)

        xbc_p = p2[:np_, :conv_dim].reshape(bp, tp, conv_dim)
        xbc_s = p2[np_:, :conv_dim].reshape(bs_, ts, conv_dim)
        ssm_p.append(hs_p.reshape(bp, n_heads, SSD_HEAD_DIM, D_STATE))
        conv_p.append(xbc_p[:, tp - (CONV_W - 1):, :])
        ssm_s.append(hs_s.reshape(bs_, n_heads, SSD_HEAD_DIM, D_STATE))
        conv_s.append(jnp.concatenate([st, xbc_s], axis=1)[:, ts:, :])
        v_s.append(vn_s.reshape(bs_, ts, gm_w))

    return (x[:np_].reshape(bp, tp, d), x[np_:].reshape(bs_, ts, d), jnp.stack(ssm_p), jnp.stack(conv_p),
            jnp.stack(ssm_s), jnp.stack(conv_s), jnp.stack(v_s))
```

```python
import functools

import jax
import jax.numpy as jnp
from jax import lax
from jax.experimental import pallas as pl
from jax.experimental.pallas import tpu as pltpu

F32 = jnp.float32
BF16 = jnp.bfloat16
EPS = 1e-6

LANES = 128
SUBLANES = 8
VMEM_LIMIT_BYTES = 60 * 1024 * 1024

GM_HEADS = 8
SSD_HEAD_DIM = 64
SSD_GROUPS = 4
D_STATE = 128
CONV_W = 4
TOKEN_BLOCK = 128
SAMPLE_SEQS_PER_STEP = 4

TOKEN_TILE = 1024
FFN_COLS = 256
PROJ_COLS = 512
MERGE_TILE = 256
NORM_ROWS = 64

_SINGLE = dict(pipeline_mode=pl.Buffered(1))


def _cparams(*sem):
    return pltpu.CompilerParams(dimension_semantics=sem, vmem_limit_bytes=VMEM_LIMIT_BYTES)


def _gelu(x):
    return x * (0.5 * (1.0 + jnp.tanh(0.7978845608028654 * (x + 0.044715 * (x * x * x)))))


def _silu(x):
    return x * jax.nn.sigmoid(x)


def _softplus(x):
    return jnp.maximum(x, 0.0) + jnp.log1p(jnp.exp(-jnp.abs(x)))


def _split3(a):
    a1 = a.astype(BF16)
    r1 = a - a1.astype(F32)
    a2 = r1.astype(BF16)
    r2 = r1 - a2.astype(F32)
    return a1, a2, r2.astype(BF16)


def _dot(a, b):
    return jnp.dot(a, b, preferred_element_type=F32)


def _dot_nt(a, b):
    return lax.dot_general(a, b, (((1,), (1,)), ((), ())), preferred_element_type=F32)


def _sel_right(a, sel):
    a1, a2, a3 = _split3(a)
    return (_dot(a1, sel) + _dot(a2, sel)) + _dot(a3, sel)


def _sel_left(sel, a):
    a1, a2, a3 = _split3(a)
    return (_dot(sel, a1) + _dot(sel, a2)) + _dot(sel, a3)


def _row_loop(nrows, fn):
    def body(i, c):
        fn(pl.ds(pl.multiple_of(i * NORM_ROWS, NORM_ROWS), NORM_ROWS))
        return c

    lax.fori_loop(0, nrows // NORM_ROWS, body, 0)


def _rmsnorm_rows(x_ref, w_ref, out_ref, nrows):
    def one(sl):
        x = x_ref[sl, :]
        r = lax.rsqrt(jnp.mean(x * x, axis=-1, keepdims=True) + EPS)
        out_ref[sl, :] = ((x * r) * w_ref[...]).astype(out_ref.dtype)

    _row_loop(nrows, one)


def _copy_rows(x_ref, out_ref, nrows):
    def one(sl):
        out_ref[sl, :] = x_ref[sl, :]

    _row_loop(nrows, one)


def _tile_branches(n_full, tile, tail_rows, run):
    i = pl.program_id(0)

    @pl.when(i < n_full)
    def _():
        run(False, tile)

    if tail_rows:
        @pl.when(i >= n_full)
        def _():
            run(True, tail_rows)


def _ffn_body(*refs, split_in, split_out, final_norm, n_full, tail_rows):
    refs = list(refs)
    x_refs = [refs.pop(0) for _ in range(2 if split_in else 1)]
    nw_ref, w1_ref, w3_ref, w2_ref, fn_ref = [refs.pop(0) for _ in range(5)]
    o_refs = [refs.pop(0) for _ in range(2 if split_out else 1)]
    (xn_ref,) = refs
    j = pl.program_id(1)
    last_j = pl.num_programs(1) - 1

    def run(is_tail, rows):
        x_ref = x_refs[-1] if is_tail else x_refs[0]
        o_ref = o_refs[-1] if is_tail else o_refs[0]

        @pl.when(j == 0)
        def _():
            _rmsnorm_rows(x_ref, nw_ref, xn_ref, rows)
            _copy_rows(x_ref, o_ref, rows)

        xn = xn_ref[0:rows, :]
        h1 = _dot(xn, w1_ref[...].astype(BF16))
        h3 = _dot(xn, w3_ref[...].astype(BF16))
        g = ((0.5 * _silu(h1)) * h3).astype(BF16)
        o_ref[0:rows, :] += _dot(g, w2_ref[...].astype(BF16))

        if final_norm:
            @pl.when(j == last_j)
            def _():
                _rmsnorm_rows(o_ref, fn_ref, o_ref, rows)

    _tile_branches(n_full, TOKEN_TILE, tail_rows, run)


def _ffn(xs, norm_w, w1, w3, w2, final_w, *, split_out):
    split_in = len(xs) == 2
    d, f = w1.shape
    tm, tf = TOKEN_TILE, FFN_COLS
    n = sum(x.shape[0] for x in xs)
    n_full, tail_rows = divmod(n, tm)
    n_tiles = n_full + (1 if tail_rows else 0)
    if split_in or split_out:
        assert tail_rows == xs[-1].shape[0] or not split_in
    final_norm = final_w is not None
    fw = final_w if final_norm else norm_w
    last_full = max(n_full - 1, 0)
    if split_in:
        x_specs = [pl.BlockSpec((tm, d), lambda i, j: (jnp.minimum(i, last_full), 0), **_SINGLE),
                   pl.BlockSpec((tail_rows, d), lambda i, j: (0, 0), **_SINGLE)]
    else:
        x_specs = [pl.BlockSpec((tm, d), lambda i, j: (i, 0), **_SINGLE)]
    if split_out:
        o_specs = [pl.BlockSpec((tm, d), lambda i, j: (jnp.minimum(i, last_full), 0)),
                   pl.BlockSpec((tail_rows, d), lambda i, j: (0, 0), **_SINGLE)]
        o_shapes = [jax.ShapeDtypeStruct((n_full * tm, d), F32), jax.ShapeDtypeStruct((tail_rows, d), F32)]
    else:
        o_specs = [pl.BlockSpec((tm, d), lambda i, j: (i, 0))]
        o_shapes = [jax.ShapeDtypeStruct((n, d), F32)]
    return pl.pallas_call(
        functools.partial(_ffn_body, split_in=split_in, split_out=split_out, final_norm=final_norm,
                          n_full=n_full, tail_rows=tail_rows),
        grid=(n_tiles, f // tf),
        in_specs=x_specs + [
            pl.BlockSpec((1, d), lambda i, j: (0, 0)),
            pl.BlockSpec((d, tf), lambda i, j: (0, j)),
            pl.BlockSpec((d, tf), lambda i, j: (0, j)),
            pl.BlockSpec((tf, d), lambda i, j: (j, 0)),
            pl.BlockSpec((1, d), lambda i, j: (0, 0)),
        ],
        out_specs=o_specs,
        out_shape=o_shapes,
        scratch_shapes=[pltpu.VMEM((tm, d), BF16)],
        compiler_params=_cparams("arbitrary", "arbitrary"),
        name="ffn",
    )(*xs, norm_w.reshape(1, d), w1, w3, w2, fw.reshape(1, d))


def _inproj_body(x_ref, nw_ref, wm_ref, wg_ref, wdt_ref, p1_ref, xbc_ref, dt_ref, xn_ref, *,
                 n_full, tail_rows, n_uvz, n_xbc):
    j = pl.program_id(1)

    def run(is_tail, rows):
        @pl.when(j == 0)
        def _():
            _rmsnorm_rows(x_ref, nw_ref, xn_ref, rows)
            dt_ref[0:rows, :] = _dot(xn_ref[0:rows, :], wdt_ref[...].astype(BF16))

        @pl.when(j < n_uvz)
        def _():
            p1_ref[0:rows, :] = _dot(xn_ref[0:rows, :], wm_ref[...].astype(BF16)).astype(p1_ref.dtype)

        @pl.when((j >= n_uvz) & (j < n_uvz + n_xbc))
        def _():
            xbc_ref[0:rows, :] = _dot(xn_ref[0:rows, :], wm_ref[...].astype(BF16))

        @pl.when(j >= n_uvz + n_xbc)
        def _():
            p1_ref[0:rows, :] = _dot(xn_ref[0:rows, :], wg_ref[...].astype(BF16)).astype(p1_ref.dtype)

    _tile_branches(n_full, TOKEN_TILE, tail_rows, run)


def _inproj(x, norm_w, w_in, w_gates, w_dt, c_uvz, c_xbc):
    n, d = x.shape
    tm, tn = TOKEN_TILE, PROJ_COLS
    n_full, tail_rows = divmod(n, tm)
    n_tiles = n_full + (1 if tail_rows else 0)
    n_uvz, n_xbc, n_g = c_uvz // tn, c_xbc // tn, w_gates.shape[1] // tn
    n_main = n_uvz + n_xbc
    return pl.pallas_call(
        functools.partial(_inproj_body, n_full=n_full, tail_rows=tail_rows, n_uvz=n_uvz, n_xbc=n_xbc),
        grid=(n_tiles, n_main + n_g),
        in_specs=[
            pl.BlockSpec((tm, d), lambda i, j: (i, 0)),
            pl.BlockSpec((1, d), lambda i, j: (0, 0)),
            pl.BlockSpec((d, tn), lambda i, j: (0, jnp.minimum(j, n_main - 1))),
            pl.BlockSpec((d, tn), lambda i, j: (0, jnp.clip(j - n_main, 0, n_g - 1))),
            pl.BlockSpec(w_dt.shape, lambda i, j: (0, 0), **_SINGLE),
        ],
        out_specs=[
            pl.BlockSpec((tm, tn), lambda i, j: (i, jnp.where(j < n_uvz, j, jnp.maximum(j - n_xbc, n_uvz - 1)))),
            pl.BlockSpec((tm, tn), lambda i, j: (i, jnp.clip(j - n_uvz, 0, n_xbc - 1))),
            pl.BlockSpec((tm, w_dt.shape[1]), lambda i, j: (i, 0)),
        ],
        out_shape=[
            jax.ShapeDtypeStruct((n, c_uvz + w_gates.shape[1]), BF16),
            jax.ShapeDtypeStruct((n, c_xbc), F32),
            jax.ShapeDtypeStruct((n, w_dt.shape[1]), F32),
        ],
        scratch_shapes=[pltpu.VMEM((tm, d), BF16)],
        compiler_params=_cparams("arbitrary", "arbitrary"),
        name="in_proj",
    )(x, norm_w.reshape(1, d), w_in, w_gates, w_dt)


def _merge_body(x_ref, yap_ref, ybp_ref, yas_ref, ybs_ref, ga_ref, gb_ref, bg_ref, wa_ref, wb_ref, wo_ref, o_ref,
                *, n_prompt_tiles):
    def run(ya_ref, yb_ref):
        pa = _dot(ya_ref[...], wa_ref[...])
        pb = _dot(yb_ref[...], wb_ref[...])
        ga = jax.nn.sigmoid(ga_ref[...].astype(F32) + bg_ref[0:1, :])
        gb = jax.nn.sigmoid(gb_ref[...].astype(F32) + bg_ref[1:2, :])
        m = (ga * pa + gb * pb).astype(BF16)
        o_ref[...] = x_ref[...] + _dot(m, wo_ref[...])

    i = pl.program_id(0)

    @pl.when(i < n_prompt_tiles)
    def _():
        run(yap_ref, ybp_ref)

    @pl.when(i >= n_prompt_tiles)
    def _():
        run(yas_ref, ybs_ref)


def _merge(x, ya_p, yb_p, ya_s, yb_s, p1, b_gate, wa, wb, wo):
    n, d = x.shape
    tm = MERGE_TILE
    npt = ya_p.shape[0] // tm
    nst = ya_s.shape[0] // tm
    gcol = p1.shape[1] // d - 2
    p_idx = lambda i: (jnp.minimum(i, npt - 1), 0)
    s_idx = lambda i: (jnp.clip(i - npt, 0, nst - 1), 0)
    return pl.pallas_call(
        functools.partial(_merge_body, n_prompt_tiles=npt),
        grid=(n // tm,),
        in_specs=[
            pl.BlockSpec((tm, d), lambda i: (i, 0)),
            pl.BlockSpec((tm, ya_p.shape[1]), p_idx),
            pl.BlockSpec((tm, yb_p.shape[1]), p_idx),
            pl.BlockSpec((tm, ya_s.shape[1]), s_idx),
            pl.BlockSpec((tm, yb_s.shape[1]), s_idx),
            pl.BlockSpec((tm, d), lambda i: (i, gcol)),
            pl.BlockSpec((tm, d), lambda i: (i, gcol + 1)),
            pl.BlockSpec((2, d), lambda i: (0, 0)),
            pl.BlockSpec(wa.shape, lambda i: (0, 0), **_SINGLE),
            pl.BlockSpec(wb.shape, lambda i: (0, 0), **_SINGLE),
            pl.BlockSpec(wo.shape, lambda i: (0, 0), **_SINGLE),
        ],
        out_specs=pl.BlockSpec((tm, d), lambda i: (i, 0)),
        out_shape=jax.ShapeDtypeStruct((n, d), F32),
        compiler_params=_cparams("arbitrary"),
        name="merge",
    )(x, ya_p, yb_p, ya_s, yb_s, p1, p1, b_gate, wa, wb, wo)


def _iota2(shape, dim):
    return lax.broadcasted_iota(jnp.int32, shape, dim)


def _block_masks(seq_len):
    tb = TOKEN_BLOCK
    row = _iota2((tb, tb), 0)
    col = _iota2((tb, tb), 1)
    if seq_len >= tb:
        causal = col <= row
        last = col == tb - 1
    else:
        causal = (col <= row) & ((row // seq_len) == (col // seq_len))
        last = col == (row // seq_len) * seq_len + (seq_len - 1)
    return causal, last


def _init_constants(wm_ref, e_ref, gw_ref, causal):
    for h in range(GM_HEADS):
        wm_ref[h] = jnp.where(causal, gw_ref[h], 0.0).astype(BF16)
    hrow = _iota2(e_ref.shape, 0)
    ccol = _iota2(e_ref.shape, 1)
    e_ref[...] = jnp.where(hrow == ccol // SSD_HEAD_DIM, 1.0, 0.0).astype(BF16)


def _gating(u_ref, v_ref, gvn_ref, bs_ref, wm_ref, ya_ref, vn_ref):
    vg = _gelu(v_ref[...].astype(F32))
    r = lax.rsqrt(jnp.mean(vg * vg, axis=-1, keepdims=True) + EPS)
    vn = (vg * r) * gvn_ref[...]
    if vn_ref is not None:
        vn_ref[...] = vn
    vnb = vn.astype(BF16)
    for h in range(GM_HEADS):
        cs = slice(h * LANES, (h + 1) * LANES)
        s = _dot(wm_ref[h], vnb[:, cs]) + bs_ref[:, h:h + 1]
        ya_ref[:, cs] = (_gelu(u_ref[:, cs].astype(F32)) * s).astype(ya_ref.dtype)


def _conv(xp_ref, cw_ref, cb_ref, xc_ref, halo_refs, seq_len):
    tb = TOKEN_BLOCK
    width = xc_ref.shape[1]
    cw = 512
    tpos = _iota2((tb, 1), 0) % seq_len if seq_len < tb else None
    for c0 in range(0, width, cw):
        cs = slice(c0, c0 + cw)
        acc = None
        for k in range(CONV_W):
            d = CONV_W - 1 - k
            xs = xp_ref[SUBLANES - d:SUBLANES - d + tb, cs]
            if tpos is not None and d > 0:
                xs = jnp.where(tpos >= d, xs, 0.0) + halo_refs[d - 1][:, cs]
            term = cw_ref[k:k + 1, cs] * xs
            acc = term if acc is None else acc + term
        xc_ref[:, cs] = _silu(cb_ref[:, cs] + acc)


def _ssd_prepare(dt_ref, dtb_ref, alog_ref, e_ref, causal, last, dtf_ref, tef_ref, eaf_ref):
    dt = _softplus(dt_ref[...] + dtb_ref[...])
    a = -jnp.exp(alog_ref[...])
    da = dt * a
    lmat = jnp.where(causal, 1.0, 0.0).astype(BF16)
    acum = _sel_left(lmat, da)
    a_last = _sel_left(jnp.where(last, 1.0, 0.0).astype(BF16), acum)
    ea = jnp.exp(acum)
    e = e_ref[...]
    dtf_ref[...] = _sel_right(dt, e)
    tef_ref[...] = _sel_right(dt * jnp.exp(a_last - acum), e)
    eaf_ref[...] = _sel_right(ea, e)
    return acum, acum.T


def _ssd_diag_group(g, xc_ref, dtf_ref, acum, acum_t, causal):
    tb = TOKEN_BLOCK
    width = xc_ref.shape[1] - 2 * SSD_GROUPS * D_STATE
    gw = width // SSD_GROUPS
    bg = xc_ref[:, width + g * D_STATE:width + (g + 1) * D_STATE]
    cg = xc_ref[:, width + (SSD_GROUPS + g) * D_STATE:width + (SSD_GROUPS + g + 1) * D_STATE].astype(BF16)
    cb = _dot_nt(cg, bg.astype(BF16))
    gs = slice(g * gw, (g + 1) * gw)
    xdt = (xc_ref[:, gs] * dtf_ref[:, gs]).astype(BF16)
    lane = _iota2((tb, LANES), 1)
    heads_per_group = gw // SSD_HEAD_DIM
    outs = []
    for jp in range(heads_per_group // 2):
        ms = []
        for hh in range(2):
            h = g * heads_per_group + 2 * jp + hh
            seg = acum[:, h:h + 1] - acum_t[h:h + 1, :]
            dec = jnp.exp(jnp.where(causal, seg, -jnp.inf))
            ms.append((cb * dec).astype(BF16))
        lhs = jnp.concatenate(ms, axis=1)
        xpair = xdt[:, jp * LANES:(jp + 1) * LANES]
        zero = jnp.zeros_like(xpair)
        rhs = jnp.concatenate([jnp.where(lane < SSD_HEAD_DIM, xpair, zero),
                               jnp.where(lane >= SSD_HEAD_DIM, xpair, zero)], axis=0)
        outs.append(_dot(lhs, rhs))
    return jnp.concatenate(outs, axis=1), cg, bg


def _finish_group(g, y, xc_ref, z_ref, dsk_ref, sn_ref, yb_ref):
    gw = yb_ref.shape[1] // SSD_GROUPS
    gs = slice(g * gw, (g + 1) * gw)
    y = y + dsk_ref[:, gs] * xc_ref[:, gs]
    yz = y * _silu(z_ref[:, gs].astype(F32))
    r = lax.rsqrt(jnp.mean(yz * yz, axis=-1, keepdims=True) + EPS)
    yb_ref[:, gs] = ((yz * r) * sn_ref[:, gs]).astype(yb_ref.dtype)


def _prompt_body(u_ref, v_ref, z_ref, xbc_ref, dt_ref, gvn_ref, gw_ref, bs_ref, cw_ref, cb_ref,
                 dtb_ref, alog_ref, dsk_ref, sn_ref,
                 ya_ref, yb_ref, hs_ref,
                 wm_ref, e_ref, xp_ref, xc_ref, dtf_ref, tef_ref, eaf_ref, st_ref):
    tb = TOKEN_BLOCK
    c = pl.program_id(1)
    causal, last = _block_masks(tb)

    @pl.when((pl.program_id(0) == 0) & (c == 0))
    def _():
        _init_constants(wm_ref, e_ref, gw_ref, causal)

    @pl.when(c == 0)
    def _():
        st_ref[...] = jnp.zeros_like(st_ref)
        xp_ref[0:SUBLANES, :] = jnp.zeros((SUBLANES, xp_ref.shape[1]), F32)

    _gating(u_ref, v_ref, gvn_ref, bs_ref, wm_ref, ya_ref, None)

    xp_ref[SUBLANES:SUBLANES + tb, :] = xbc_ref[...]
    _conv(xp_ref, cw_ref, cb_ref, xc_ref, None, tb)
    xp_ref[0:SUBLANES, :] = xbc_ref[tb - SUBLANES:tb, :]

    acum, acum_t = _ssd_prepare(dt_ref, dtb_ref, alog_ref, e_ref, causal, last, dtf_ref, tef_ref, eaf_ref)

    gw = yb_ref.shape[1] // SSD_GROUPS
    for g in range(SSD_GROUPS):
        gs = slice(g * gw, (g + 1) * gw)
        yd, cg, bg = _ssd_diag_group(g, xc_ref, dtf_ref, acum, acum_t, causal)
        st = st_ref[:, gs]
        y = yd + _dot(cg, st.astype(BF16)) * eaf_ref[:, gs]
        _finish_group(g, y, xc_ref, z_ref, dsk_ref, sn_ref, yb_ref)
        xw = (xc_ref[:, gs] * tef_ref[:, gs]).astype(BF16)
        st_ref[:, gs] = eaf_ref[tb - 1:tb, gs] * st + _dot(bg.T.astype(BF16), xw)

    @pl.when(c == pl.num_programs(1) - 1)
    def _():
        for k in range(st_ref.shape[1] // LANES):
            hs_ref[k * LANES:(k + 1) * LANES, :] = st_ref[:, k * LANES:(k + 1) * LANES].T


def _sample_body(u_ref, v_ref, z_ref, xbc_ref, dt_ref, h1_ref, h2_ref, h3_ref, h0_ref,
                 gvn_ref, gw_ref, bs_ref, cw_ref, cb_ref, dtb_ref, alog_ref, dsk_ref, sn_ref,
                 ya_ref, yb_ref, vn_ref, hs_ref,
                 wm_ref, e_ref, et_ref, xp_ref, xc_ref, dtf_ref, tef_ref, eaf_ref,
                 y_ref, yoff_ref, cg_ref, bg_ref, xwt_ref, eat_ref, *, seq_len):
    tb = TOKEN_BLOCK
    k = pl.program_id(1)
    causal, last = _block_masks(seq_len)
    gw = yb_ref.shape[1] // SSD_GROUPS

    @pl.when((pl.program_id(0) == 0) & (k == 0))
    def _():
        _init_constants(wm_ref, e_ref, gw_ref, causal)
        crow = _iota2(et_ref.shape, 0)
        hcol = _iota2(et_ref.shape, 1)
        et_ref[...] = jnp.where(crow // SSD_HEAD_DIM == hcol, 1.0, 0.0).astype(BF16)
        xp_ref[0:SUBLANES, :] = jnp.zeros((SUBLANES, xp_ref.shape[1]), F32)

    @pl.when(k == 0)
    def _():
        _gating(u_ref, v_ref, gvn_ref, bs_ref, wm_ref, ya_ref, vn_ref)
        xp_ref[SUBLANES:SUBLANES + tb, :] = xbc_ref[...]
        _conv(xp_ref, cw_ref, cb_ref, xc_ref, (h1_ref, h2_ref, h3_ref), seq_len)
        acum, acum_t = _ssd_prepare(dt_ref, dtb_ref, alog_ref, e_ref, causal, last, dtf_ref, tef_ref, eaf_ref)
        eat_ref[...] = jnp.exp(acum_t)
        for g in range(SSD_GROUPS):
            gs = slice(g * gw, (g + 1) * gw)
            yd, cg, bg = _ssd_diag_group(g, xc_ref, dtf_ref, acum, acum_t, causal)
            y_ref[:, gs] = yd
            cg_ref[:, g * D_STATE:(g + 1) * D_STATE] = cg.astype(F32)
            bg_ref[:, g * D_STATE:(g + 1) * D_STATE] = bg
            xw = xc_ref[:, gs] * tef_ref[:, gs]
            for q in range(gw // LANES):
                r0 = g * gw + q * LANES
                xwt_ref[r0:r0 + LANES, :] = xw[:, q * LANES:(q + 1) * LANES].T.astype(BF16)
        yoff_ref[...] = jnp.zeros_like(yoff_ref)

    nseq = h0_ref.shape[0]
    rowseq = _iota2((tb, 1), 0) // seq_len
    tok = _iota2((tb, LANES), 0)
    et = et_ref[...]
    for b in range(nseq):
        sb = k * nseq + b
        rmask = rowseq == sb
        onehot = jnp.where(tok == sb * seq_len + (seq_len - 1), 1.0, 0.0).astype(BF16)
        cd_heads = _sel_right(eat_ref[...], onehot)
        cd = _sel_left(et, cd_heads)
        for g in range(SSD_GROUPS):
            gs = slice(g * gw, (g + 1) * gw)
            ds = slice(g * D_STATE, (g + 1) * D_STATE)
            h0 = h0_ref[b, gs, :]
            cm = jnp.where(rmask, cg_ref[:, ds], 0.0).astype(BF16)
            yoff_ref[:, gs] += _dot_nt(cm, h0.astype(BF16))
            bm = jnp.where(rmask, bg_ref[:, ds], 0.0).astype(BF16)
            hs_ref[b, gs, :] = cd[gs, :] * h0 + _dot(xwt_ref[gs, :], bm)

    @pl.when(k == pl.num_programs(1) - 1)
    def _():
        for g in range(SSD_GROUPS):
            gs = slice(g * gw, (g + 1) * gw)
            y = y_ref[:, gs] + yoff_ref[:, gs] * eaf_ref[:, gs]
            _finish_group(g, y, xc_ref, z_ref, dsk_ref, sn_ref, yb_ref)


def _mixer_params(lp):
    (gm_v_norm, gm_ws, gm_bs, conv_w, conv_b, dt_bias, a_log, d_skip, ssd_norm) = lp
    nh = dt_bias.shape[0]
    pad = LANES - nh
    return dict(
        gvn=gm_v_norm.reshape(1, -1),
        cw=conv_w,
        cb=conv_b.reshape(1, -1),
        dtb=jnp.pad(dt_bias, (0, pad)).reshape(1, LANES),
        alog=jnp.pad(a_log, (0, pad)).reshape(1, LANES),
        dsk=jnp.repeat(d_skip, SSD_HEAD_DIM).reshape(1, -1),
        sn=ssd_norm.reshape(1, -1),
    )


def _full_spec(a):
    nd = a.ndim
    return pl.BlockSpec(a.shape, lambda i, j: (0,) * nd)


def _mixer_prompt(p1, xbc, dt, lp, n_seq, seq_len, gm_w, ssd_w, conv_dim):
    tb = TOKEN_BLOCK
    nc = seq_len // tb
    mp = _mixer_params(lp)
    gw_full = lp[1]
    bs_t = lp[2].T
    row = lambda b, c: b * nc + c
    params = [mp["gvn"], gw_full, bs_t, mp["cw"], mp["cb"], mp["dtb"], mp["alog"], mp["dsk"], mp["sn"]]
    return pl.pallas_call(
        _prompt_body,
        grid=(n_seq, nc),
        in_specs=[
            pl.BlockSpec((tb, gm_w), lambda b, c: (row(b, c), 0)),
            pl.BlockSpec((tb, gm_w), lambda b, c: (row(b, c), 1)),
            pl.BlockSpec((tb, ssd_w), lambda b, c: (row(b, c), 1)),
            pl.BlockSpec((tb, conv_dim), lambda b, c: (row(b, c), 0)),
            pl.BlockSpec((tb, LANES), lambda b, c: (row(b, c), 0)),
        ] + [_full_spec(a) for a in params],
        out_specs=[
            pl.BlockSpec((tb, gm_w), lambda b, c: (row(b, c), 0)),
            pl.BlockSpec((tb, ssd_w), lambda b, c: (row(b, c), 0)),
            pl.BlockSpec((None, ssd_w, D_STATE), lambda b, c: (b, 0, 0)),
        ],
        out_shape=[
            jax.ShapeDtypeStruct((n_seq * seq_len, gm_w), BF16),
            jax.ShapeDtypeStruct((n_seq * seq_len, ssd_w), BF16),
            jax.ShapeDtypeStruct((n_seq, ssd_w, D_STATE), F32),
        ],
        scratch_shapes=[
            pltpu.VMEM((GM_HEADS, tb, tb), BF16),
            pltpu.VMEM((LANES, ssd_w), BF16),
            pltpu.VMEM((SUBLANES + tb, conv_dim), F32),
            pltpu.VMEM((tb, conv_dim), F32),
            pltpu.VMEM((tb, ssd_w), F32),
            pltpu.VMEM((tb, ssd_w), F32),
            pltpu.VMEM((tb, ssd_w), F32),
            pltpu.VMEM((D_STATE, ssd_w), F32),
        ],
        compiler_params=_cparams("arbitrary", "arbitrary"),
        name="mixer_prompt",
    )(p1, p1, p1, xbc, dt, *params)


def _mixer_sample(p1, xbc, dt, lp, h0, halos, row0, n_seq, seq_len, gm_w, ssd_w, conv_dim):
    tb = TOKEN_BLOCK
    seqs_per_block = tb // seq_len
    nblk = n_seq // seqs_per_block
    nsub = seqs_per_block // SAMPLE_SEQS_PER_STEP
    blk0 = row0 // tb
    mp = _mixer_params(lp)
    reps = tb // seq_len
    gw_tiled = jnp.tile(lp[1][:, :seq_len, :seq_len], (1, reps, reps))
    bs_t = jnp.tile(lp[2][:, :seq_len], (1, reps)).T
    params = [mp["gvn"], gw_tiled, bs_t, mp["cw"], mp["cb"], mp["dtb"], mp["alog"], mp["dsk"], mp["sn"]]
    ntok = n_seq * seq_len
    return pl.pallas_call(
        functools.partial(_sample_body, seq_len=seq_len),
        grid=(nblk, nsub),
        in_specs=[
            pl.BlockSpec((tb, gm_w), lambda i, k: (blk0 + i, 0)),
            pl.BlockSpec((tb, gm_w), lambda i, k: (blk0 + i, 1)),
            pl.BlockSpec((tb, ssd_w), lambda i, k: (blk0 + i, 1)),
            pl.BlockSpec((tb, conv_dim), lambda i, k: (blk0 + i, 0)),
            pl.BlockSpec((tb, LANES), lambda i, k: (blk0 + i, 0)),
            pl.BlockSpec((tb, conv_dim), lambda i, k: (i, 0)),
            pl.BlockSpec((tb, conv_dim), lambda i, k: (i, 0)),
            pl.BlockSpec((tb, conv_dim), lambda i, k: (i, 0)),
            pl.BlockSpec((SAMPLE_SEQS_PER_STEP, ssd_w, D_STATE), lambda i, k: (i * nsub + k, 0, 0)),
        ] + [_full_spec(a) for a in params],
        out_specs=[
            pl.BlockSpec((tb, gm_w), lambda i, k: (i, 0)),
            pl.BlockSpec((tb, ssd_w), lambda i, k: (i, 0)),
            pl.BlockSpec((tb, gm_w), lambda i, k: (i, 0)),
            pl.BlockSpec((SAMPLE_SEQS_PER_STEP, ssd_w, D_STATE), lambda i, k: (i * nsub + k, 0, 0)),
        ],
        out_shape=[
            jax.ShapeDtypeStruct((ntok, gm_w), BF16),
            jax.ShapeDtypeStruct((ntok, ssd_w), BF16),
            jax.ShapeDtypeStruct((ntok, gm_w), F32),
            jax.ShapeDtypeStruct((n_seq, ssd_w, D_STATE), F32),
        ],
        scratch_shapes=[
            pltpu.VMEM((GM_HEADS, tb, tb), BF16),
            pltpu.VMEM((LANES, ssd_w), BF16),
            pltpu.VMEM((ssd_w, LANES), BF16),
            pltpu.VMEM((SUBLANES + tb, conv_dim), F32),
            pltpu.VMEM((tb, conv_dim), F32),
            pltpu.VMEM((tb, ssd_w), F32),
            pltpu.VMEM((tb, ssd_w), F32),
            pltpu.VMEM((tb, ssd_w), F32),
            pltpu.VMEM((tb, ssd_w), F32),
            pltpu.VMEM((tb, ssd_w), F32),
            pltpu.VMEM((tb, SSD_GROUPS * D_STATE), F32),
            pltpu.VMEM((tb, SSD_GROUPS * D_STATE), F32),
            pltpu.VMEM((ssd_w, tb), BF16),
            pltpu.VMEM((LANES, tb), F32),
        ],
        compiler_params=_cparams("arbitrary", "arbitrary"),
        name="mixer_sample",
    )(p1, p1, p1, xbc, dt, *halos, h0, *params)


def kernel(x_prompt, x_sample, state_ssm, state_conv, ffn1_norm, ffn1_w1, ffn1_w3, ffn1_w2, mix_norm, w_in,
           b_gate, conv_w, conv_b, dt_bias, a_log, d_skip, ssd_norm, gm_v_norm, gm_ws, gm_bs, w_proj_a,
           w_proj_b, w_out, ffn2_norm, ffn2_w1, ffn2_w3, ffn2_w2, final_norm):
    bp, tp, d = x_prompt.shape
    bs_, ts, _ = x_sample.shape
    depth = w_in.shape[0]
    n_heads = dt_bias.shape[1]
    gm_w = gm_v_norm.shape[1]
    ssd_w = ssd_norm.shape[1]
    conv_dim = conv_w.shape[2]
    np_, ns = bp * tp, bs_ * ts

    xs = [x_prompt.reshape(np_, d), x_sample.reshape(ns, d)]
    ssm_p, conv_p, ssm_s, conv_s, v_s = [], [], [], [], []
    for l in range(depth):
        c_uvz = 2 * gm_w + ssd_w
        c_dt = c_uvz + conv_dim
        c_gate = c_dt + n_heads
        wl = w_in[l]
        w_gates = wl[:, c_gate:]
        w_dt = jnp.pad(wl[:, c_dt:c_gate], ((0, 0), (0, LANES - n_heads)))

        (x,) = _ffn(xs, ffn1_norm[l], ffn1_w1[l], ffn1_w3[l], ffn1_w2[l], None, split_out=False)
        p1, xbc, dt = _inproj(x, mix_norm[l], wl, w_gates, w_dt, c_uvz, conv_dim)

        lp = (gm_v_norm[l], gm_ws[l], gm_bs[l], conv_w[l], conv_b[l], dt_bias[l], a_log[l], d_skip[l], ssd_norm[l])
        ya_p, yb_p, hs_p = _mixer_prompt(p1, xbc, dt, lp, bp, tp, gm_w, ssd_w, conv_dim)

        st = state_conv[l]
        halos = []
        for dd in range(1, CONV_W):
            hz = jnp.concatenate([st[:, CONV_W - 1 - dd:, :], jnp.zeros((bs_, ts - dd, conv_dim), F32)], axis=1)
            halos.append(hz.reshape(ns, conv_dim))
        ya_s, yb_s, vn_s, hs_s = _mixer_sample(p1, xbc, dt, lp, state_ssm[l].reshape(bs_, ssd_w, D_STATE), halos,
                                               np_, bs_, ts, gm_w, ssd_w, conv_dim)

        x = _merge(x, ya_p, yb_p, ya_s, yb_s, p1, b_gate[l], w_proj_a[l].astype(BF16), w_proj_b[l].astype(BF16),
                   w_out[l].astype(BF16))
        last = l == depth - 1
        xs = _ffn([x], ffn2_norm[l], ffn2_w1[l], ffn2_w3[l], ffn2_w2[l], final_norm if last else None,
                  split_out=True)

        xbc_p = xbc[:np_].reshape(bp, tp, conv_dim)
        xbc_s = xbc[np_:].reshape(bs_, ts, conv_dim)
        ssm_p.append(hs_p.reshape(bp, n_heads, SSD_HEAD_DIM, D_STATE))
        conv_p.append(xbc_p[:, tp - (CONV_W - 1):, :])
        ssm_s.append(hs_s.reshape(bs_, n_heads, SSD_HEAD_DIM, D_STATE))
        conv_s.append(jnp.concatenate([st, xbc_s], axis=1)[:, ts:, :])
        v_s.append(vn_s.reshape(bs_, ts, gm_w))

    return (xs[0].reshape(bp, tp, d), xs[1].reshape(bs_, ts, d), jnp.stack(ssm_p), jnp.stack(conv_p),
            jnp.stack(ssm_s), jnp.stack(conv_s), jnp.stack(v_s))
```

```python
import functools

import jax
import jax.numpy as jnp
from jax import lax
from jax.experimental import pallas as pl
from jax.experimental.pallas import tpu as pltpu

F32 = jnp.float32
BF16 = jnp.bfloat16
EPS = 1e-6

LANES = 128
SUBLANES = 8
VMEM_LIMIT_BYTES = 60 * 1024 * 1024

GM_HEADS = 8
SSD_HEAD_DIM = 64
SSD_GROUPS = 4
D_STATE = 128
CONV_W = 4
TOKEN_BLOCK = 128
SAMPLE_SEQS_PER_STEP = 4

TOKEN_TILE = 1024
FFN_COLS = 256
PROJ_COLS = 1024
MERGE_TILE = 256
NORM_ROWS = 64

_SINGLE = dict(pipeline_mode=pl.Buffered(1))


def _cparams(*sem):
    return pltpu.CompilerParams(dimension_semantics=sem, vmem_limit_bytes=VMEM_LIMIT_BYTES)


def _gelu(x):
    return x * (0.5 * (1.0 + jnp.tanh(0.7978845608028654 * (x + 0.044715 * (x * x * x)))))


def _silu(x):
    return x * jax.nn.sigmoid(x)


def _softplus(x):
    return jnp.maximum(x, 0.0) + jnp.log1p(jnp.exp(-jnp.abs(x)))


def _split3(a):
    a1 = a.astype(BF16)
    r1 = a - a1.astype(F32)
    a2 = r1.astype(BF16)
    r2 = r1 - a2.astype(F32)
    return a1, a2, r2.astype(BF16)


def _dot(a, b):
    return jnp.dot(a, b, preferred_element_type=F32)


def _dot_nt(a, b):
    return lax.dot_general(a, b, (((1,), (1,)), ((), ())), preferred_element_type=F32)


def _sel_right(a, sel):
    a1, a2, a3 = _split3(a)
    return (_dot(a1, sel) + _dot(a2, sel)) + _dot(a3, sel)


def _sel_left(sel, a):
    a1, a2, a3 = _split3(a)
    return (_dot(sel, a1) + _dot(sel, a2)) + _dot(sel, a3)


def _row_loop(nrows, fn):
    def body(i, c):
        fn(pl.ds(pl.multiple_of(i * NORM_ROWS, NORM_ROWS), NORM_ROWS))
        return c

    lax.fori_loop(0, nrows // NORM_ROWS, body, 0)


def _rmsnorm_rows(x_ref, w_ref, out_ref, nrows):
    def one(sl):
        x = x_ref[sl, :]
        r = lax.rsqrt(jnp.mean(x * x, axis=-1, keepdims=True) + EPS)
        out_ref[sl, :] = ((x * r) * w_ref[...]).astype(out_ref.dtype)

    _row_loop(nrows, one)


def _copy_rows(x_ref, out_ref, nrows):
    def one(sl):
        out_ref[sl, :] = x_ref[sl, :]

    _row_loop(nrows, one)


def _tile_branches(n_full, tile, tail_rows, run, axis=0):
    i = pl.program_id(axis)

    @pl.when(i < n_full)
    def _():
        run(False, tile)

    if tail_rows:
        @pl.when(i >= n_full)
        def _():
            run(True, tail_rows)


def _ffn_body(*refs, split_in, split_out, final_norm, emit_norm, n_full, tail_rows):
    refs = list(refs)
    x_refs = [refs.pop(0) for _ in range(2 if split_in else 1)]
    nw_ref, w1_ref, w3_ref, w2_ref, fn_ref = [refs.pop(0) for _ in range(5)]
    o_refs = [refs.pop(0) for _ in range(2 if split_out else 1)]
    xno_ref = refs.pop(0) if emit_norm else None
    (xn_ref,) = refs
    j = pl.program_id(1)
    last_j = pl.num_programs(1) - 1

    def run(is_tail, rows):
        x_ref = x_refs[-1] if is_tail else x_refs[0]
        o_ref = o_refs[-1] if is_tail else o_refs[0]

        @pl.when(j == 0)
        def _():
            _rmsnorm_rows(x_ref, nw_ref, xn_ref, rows)
            _copy_rows(x_ref, o_ref, rows)

        xn = xn_ref[0:rows, :]
        h1 = _dot(xn, w1_ref[...].astype(BF16))
        h3 = _dot(xn, w3_ref[...].astype(BF16))
        g = ((0.5 * _silu(h1)) * h3).astype(BF16)
        o_ref[0:rows, :] += _dot(g, w2_ref[...].astype(BF16))

        if final_norm:
            @pl.when(j == last_j)
            def _():
                _rmsnorm_rows(o_ref, fn_ref, o_ref, rows)

        if emit_norm:
            @pl.when(j == last_j)
            def _():
                _rmsnorm_rows(o_ref, fn_ref, xno_ref, rows)

    _tile_branches(n_full, TOKEN_TILE, tail_rows, run)


def _ffn(xs, norm_w, w1, w3, w2, *, final_w=None, next_norm_w=None, split_out=False):
    split_in = len(xs) == 2
    d, f = w1.shape
    tm, tf = TOKEN_TILE, FFN_COLS
    n = sum(x.shape[0] for x in xs)
    n_full, tail_rows = divmod(n, tm)
    n_tiles = n_full + (1 if tail_rows else 0)
    if split_in:
        assert tail_rows == xs[-1].shape[0]
    final_norm = final_w is not None
    emit_norm = next_norm_w is not None
    assert not (final_norm and emit_norm)
    fw = final_w if final_norm else (next_norm_w if emit_norm else norm_w)
    last_full = max(n_full - 1, 0)
    if split_in:
        x_specs = [pl.BlockSpec((tm, d), lambda i, j: (jnp.minimum(i, last_full), 0), **_SINGLE),
                   pl.BlockSpec((tail_rows, d), lambda i, j: (0, 0), **_SINGLE)]
    else:
        x_specs = [pl.BlockSpec((tm, d), lambda i, j: (i, 0), **_SINGLE)]
    if split_out:
        o_specs = [pl.BlockSpec((tm, d), lambda i, j: (jnp.minimum(i, last_full), 0)),
                   pl.BlockSpec((tail_rows, d), lambda i, j: (0, 0), **_SINGLE)]
        o_shapes = [jax.ShapeDtypeStruct((n_full * tm, d), F32), jax.ShapeDtypeStruct((tail_rows, d), F32)]
    else:
        o_specs = [pl.BlockSpec((tm, d), lambda i, j: (i, 0))]
        o_shapes = [jax.ShapeDtypeStruct((n, d), F32)]
    if emit_norm:
        o_specs.append(pl.BlockSpec((tm, d), lambda i, j: (i, 0), **_SINGLE))
        o_shapes.append(jax.ShapeDtypeStruct((n, d), BF16))
    return pl.pallas_call(
        functools.partial(_ffn_body, split_in=split_in, split_out=split_out, final_norm=final_norm,
                          emit_norm=emit_norm, n_full=n_full, tail_rows=tail_rows),
        grid=(n_tiles, f // tf),
        in_specs=x_specs + [
            pl.BlockSpec((1, d), lambda i, j: (0, 0)),
            pl.BlockSpec((d, tf), lambda i, j: (0, j)),
            pl.BlockSpec((d, tf), lambda i, j: (0, j)),
            pl.BlockSpec((tf, d), lambda i, j: (j, 0)),
            pl.BlockSpec((1, d), lambda i, j: (0, 0)),
        ],
        out_specs=o_specs,
        out_shape=o_shapes,
        scratch_shapes=[pltpu.VMEM((tm, d), BF16)],
        compiler_params=_cparams("arbitrary", "arbitrary"),
        name="ffn",
    )(*xs, norm_w.reshape(1, d), w1, w3, w2, fw.reshape(1, d))


def _inproj_body(xn_ref, w_ref, wdt_ref, p1_ref, xbc_ref, dt_ref, wb_ref, *,
                 n_full, tail_rows, n_uvz, n_xbc, n_dt):
    j = pl.program_id(0)
    is_xbc = (j >= n_uvz) & (j < n_uvz + n_xbc)

    @pl.when(pl.program_id(1) == 0)
    def _():
        wb_ref[...] = w_ref[...].astype(BF16)

    def run(is_tail, rows):
        @pl.when(j == 0)
        def _():
            dt = _dot_nt(xn_ref[0:rows, :], wdt_ref[...].astype(BF16))
            lane = _iota2(dt.shape, 1)
            dt_ref[0:rows, :] = jnp.where(lane < n_dt, dt, 0.0)

        @pl.when(jnp.logical_not(is_xbc))
        def _():
            p1_ref[0:rows, :] = _dot_nt(xn_ref[0:rows, :], wb_ref[...]).astype(p1_ref.dtype)

        @pl.when(is_xbc)
        def _():
            xbc_ref[0:rows, :] = _dot_nt(xn_ref[0:rows, :], wb_ref[...])

    _tile_branches(n_full, TOKEN_TILE, tail_rows, run, axis=1)


def _inproj(xn, w_t, c_uvz, c_xbc, c_dt):
    n, d = xn.shape
    tm, tn = TOKEN_TILE, PROJ_COLS
    n_full, tail_rows = divmod(n, tm)
    n_tiles = n_full + (1 if tail_rows else 0)
    last_i = n_tiles - 1
    c_gate0 = c_uvz + c_xbc + c_dt
    c_gates = w_t.shape[0] - c_gate0
    n_uvz, n_xbc, n_g = c_uvz // tn, c_xbc // tn, c_gates // tn
    n_main = n_uvz + n_xbc
    assert c_gate0 % SUBLANES == 0 and (c_uvz + c_xbc) % SUBLANES == 0

    def w_row(j, i):
        return (pl.multiple_of(jnp.where(j < n_main, j * tn, c_gate0 + (j - n_main) * tn), SUBLANES), 0)

    def p1_idx(j, i):
        writes = (j < n_uvz) | (j >= n_main)
        col = jnp.where(j < n_uvz, j, jnp.maximum(j - n_xbc, n_uvz - 1))
        return (jnp.where(writes, i, last_i), col)

    def xbc_idx(j, i):
        row = jnp.where(j < n_uvz, 0, jnp.where(j < n_main, i, last_i))
        return (row, jnp.clip(j - n_uvz, 0, n_xbc - 1))

    return pl.pallas_call(
        functools.partial(_inproj_body, n_full=n_full, tail_rows=tail_rows, n_uvz=n_uvz, n_xbc=n_xbc, n_dt=c_dt),
        grid=(n_main + n_g, n_tiles),
        in_specs=[
            pl.BlockSpec((tm, d), lambda j, i: (i, 0)),
            pl.BlockSpec((pl.Element(tn), pl.Element(d)), w_row),
            pl.BlockSpec((pl.Element(LANES), pl.Element(d)), lambda j, i: (c_uvz + c_xbc, 0)),
        ],
        out_specs=[
            pl.BlockSpec((tm, tn), p1_idx),
            pl.BlockSpec((tm, tn), xbc_idx),
            pl.BlockSpec((tm, LANES), lambda j, i: (jnp.where(j == 0, i, last_i), 0)),
        ],
        out_shape=[
            jax.ShapeDtypeStruct((n, c_uvz + c_gates), BF16),
            jax.ShapeDtypeStruct((n, c_xbc), F32),
            jax.ShapeDtypeStruct((n, LANES), F32),
        ],
        scratch_shapes=[pltpu.VMEM((tn, d), BF16)],
        compiler_params=_cparams("arbitrary", "arbitrary"),
        name="in_proj",
    )(xn, w_t, w_t)


def _merge_body(x_ref, yap_ref, ybp_ref, yas_ref, ybs_ref, ga_ref, gb_ref, bg_ref, wa_ref, wb_ref, wo_ref, o_ref,
                *, n_prompt_tiles):
    def run(ya_ref, yb_ref):
        pa = _dot(ya_ref[...], wa_ref[...])
        pb = _dot(yb_ref[...], wb_ref[...])
        ga = jax.nn.sigmoid(ga_ref[...].astype(F32) + bg_ref[0:1, :])
        gb = jax.nn.sigmoid(gb_ref[...].astype(F32) + bg_ref[1:2, :])
        m = (ga * pa + gb * pb).astype(BF16)
        o_ref[...] = x_ref[...] + _dot(m, wo_ref[...])

    i = pl.program_id(0)

    @pl.when(i < n_prompt_tiles)
    def _():
        run(yap_ref, ybp_ref)

    @pl.when(i >= n_prompt_tiles)
    def _():
        run(yas_ref, ybs_ref)


def _merge(x, ya_p, yb_p, ya_s, yb_s, p1, b_gate, wa, wb, wo):
    n, d = x.shape
    tm = MERGE_TILE
    npt = ya_p.shape[0] // tm
    nst = ya_s.shape[0] // tm
    gcol = p1.shape[1] // d - 2
    p_idx = lambda i: (jnp.minimum(i, npt - 1), 0)
    s_idx = lambda i: (jnp.clip(i - npt, 0, nst - 1), 0)
    return pl.pallas_call(
        functools.partial(_merge_body, n_prompt_tiles=npt),
        grid=(n // tm,),
        in_specs=[
            pl.BlockSpec((tm, d), lambda i: (i, 0)),
            pl.BlockSpec((tm, ya_p.shape[1]), p_idx),
            pl.BlockSpec((tm, yb_p.shape[1]), p_idx),
            pl.BlockSpec((tm, ya_s.shape[1]), s_idx),
            pl.BlockSpec((tm, yb_s.shape[1]), s_idx),
            pl.BlockSpec((tm, d), lambda i: (i, gcol)),
            pl.BlockSpec((tm, d), lambda i: (i, gcol + 1)),
            pl.BlockSpec((2, d), lambda i: (0, 0)),
            pl.BlockSpec(wa.shape, lambda i: (0, 0), **_SINGLE),
            pl.BlockSpec(wb.shape, lambda i: (0, 0), **_SINGLE),
            pl.BlockSpec(wo.shape, lambda i: (0, 0), **_SINGLE),
        ],
        out_specs=pl.BlockSpec((tm, d), lambda i: (i, 0)),
        out_shape=jax.ShapeDtypeStruct((n, d), F32),
        compiler_params=_cparams("arbitrary"),
        name="merge",
    )(x, ya_p, yb_p, ya_s, yb_s, p1, p1, b_gate, wa, wb, wo)


def _iota2(shape, dim):
    return lax.broadcasted_iota(jnp.int32, shape, dim)


def _block_masks(seq_len):
    tb = TOKEN_BLOCK
    row = _iota2((tb, tb), 0)
    col = _iota2((tb, tb), 1)
    if seq_len >= tb:
        causal = col <= row
        last = col == tb - 1
    else:
        causal = (col <= row) & ((row // seq_len) == (col // seq_len))
        last = col == (row // seq_len) * seq_len + (seq_len - 1)
    return causal, last


def _init_constants(wm_ref, e_ref, gw_ref, causal):
    for h in range(GM_HEADS):
        wm_ref[h] = jnp.where(causal, gw_ref[h], 0.0).astype(BF16)
    hrow = _iota2(e_ref.shape, 0)
    ccol = _iota2(e_ref.shape, 1)
    e_ref[...] = jnp.where(hrow == ccol // SSD_HEAD_DIM, 1.0, 0.0).astype(BF16)


def _gating(u_ref, v_ref, gvn_ref, bs_ref, wm_ref, ya_ref, vn_ref):
    vg = _gelu(v_ref[...].astype(F32))
    r = lax.rsqrt(jnp.mean(vg * vg, axis=-1, keepdims=True) + EPS)
    vn = (vg * r) * gvn_ref[...]
    if vn_ref is not None:
        vn_ref[...] = vn
    vnb = vn.astype(BF16)
    for h in range(GM_HEADS):
        cs = slice(h * LANES, (h + 1) * LANES)
        s = _dot(wm_ref[h], vnb[:, cs]) + bs_ref[:, h:h + 1]
        ya_ref[:, cs] = (_gelu(u_ref[:, cs].astype(F32)) * s).astype(ya_ref.dtype)


def _conv(xp_ref, cw_ref, cb_ref, xc_ref, halo_refs, seq_len):
    tb = TOKEN_BLOCK
    width = xc_ref.shape[1]
    cw = 512
    tpos = _iota2((tb, 1), 0) % seq_len if seq_len < tb else None
    for c0 in range(0, width, cw):
        cs = slice(c0, c0 + cw)
        acc = None
        for k in range(CONV_W):
            d = CONV_W - 1 - k
            xs = xp_ref[SUBLANES - d:SUBLANES - d + tb, cs]
            if tpos is not None and d > 0:
                xs = jnp.where(tpos >= d, xs, 0.0) + halo_refs[d - 1][:, cs]
            term = cw_ref[k:k + 1, cs] * xs
            acc = term if acc is None else acc + term
        xc_ref[:, cs] = _silu(cb_ref[:, cs] + acc)


def _ssd_prepare(dt_ref, dtb_ref, alog_ref, e_ref, causal, last, dtf_ref, tef_ref, eaf_ref):
    dt = _softplus(dt_ref[...] + dtb_ref[...])
    a = -jnp.exp(alog_ref[...])
    da = dt * a
    lmat = jnp.where(causal, 1.0, 0.0).astype(BF16)
    acum = _sel_left(lmat, da)
    a_last = _sel_left(jnp.where(last, 1.0, 0.0).astype(BF16), acum)
    ea = jnp.exp(acum)
    e = e_ref[...]
    dtf_ref[...] = _sel_right(dt, e)
    tef_ref[...] = _sel_right(dt * jnp.exp(a_last - acum), e)
    eaf_ref[...] = _sel_right(ea, e)
    return acum, acum.T


def _ssd_diag_group(g, xc_ref, dtf_ref, acum, acum_t, causal):
    tb = TOKEN_BLOCK
    width = xc_ref.shape[1] - 2 * SSD_GROUPS * D_STATE
    gw = width // SSD_GROUPS
    bg = xc_ref[:, width + g * D_STATE:width + (g + 1) * D_STATE]
    cg = xc_ref[:, width + (SSD_GROUPS + g) * D_STATE:width + (SSD_GROUPS + g + 1) * D_STATE].astype(BF16)
    cb = _dot_nt(cg, bg.astype(BF16))
    gs = slice(g * gw, (g + 1) * gw)
    xdt = (xc_ref[:, gs] * dtf_ref[:, gs]).astype(BF16)
    lane = _iota2((tb, LANES), 1)
    heads_per_group = gw // SSD_HEAD_DIM
    outs = []
    for jp in range(heads_per_group // 2):
        ms = []
        for hh in range(2):
            h = g * heads_per_group + 2 * jp + hh
            seg = acum[:, h:h + 1] - acum_t[h:h + 1, :]
            dec = jnp.exp(jnp.where(causal, seg, -jnp.inf))
            ms.append((cb * dec).astype(BF16))
        lhs = jnp.concatenate(ms, axis=1)
        xpair = xdt[:, jp * LANES:(jp + 1) * LANES]
        zero = jnp.zeros_like(xpair)
        rhs = jnp.concatenate([jnp.where(lane < SSD_HEAD_DIM, xpair, zero),
                               jnp.where(lane >= SSD_HEAD_DIM, xpair, zero)], axis=0)
        outs.append(_dot(lhs, rhs))
    return jnp.concatenate(outs, axis=1), cg, bg


def _finish_group(g, y, xc_ref, z_ref, dsk_ref, sn_ref, yb_ref):
    gw = yb_ref.shape[1] // SSD_GROUPS
    gs = slice(g * gw, (g + 1) * gw)
    y = y + dsk_ref[:, gs] * xc_ref[:, gs]
    yz = y * _silu(z_ref[:, gs].astype(F32))
    r = lax.rsqrt(jnp.mean(yz * yz, axis=-1, keepdims=True) + EPS)
    yb_ref[:, gs] = ((yz * r) * sn_ref[:, gs]).astype(yb_ref.dtype)


def _prompt_body(u_ref, v_ref, z_ref, xbc_ref, dt_ref, gvn_ref, gw_ref, bs_ref, cw_ref, cb_ref,
                 dtb_ref, alog_ref, dsk_ref, sn_ref,
                 ya_ref, yb_ref, hs_ref, ct_ref,
                 wm_ref, e_ref, xp_ref, xc_ref, dtf_ref, tef_ref, eaf_ref, st_ref):
    tb = TOKEN_BLOCK
    c = pl.program_id(1)
    causal, last = _block_masks(tb)

    @pl.when((pl.program_id(0) == 0) & (c == 0))
    def _():
        _init_constants(wm_ref, e_ref, gw_ref, causal)

    @pl.when(c == 0)
    def _():
        st_ref[...] = jnp.zeros_like(st_ref)
        xp_ref[0:SUBLANES, :] = jnp.zeros((SUBLANES, xp_ref.shape[1]), F32)

    _gating(u_ref, v_ref, gvn_ref, bs_ref, wm_ref, ya_ref, None)

    xp_ref[SUBLANES:SUBLANES + tb, :] = xbc_ref[...]
    _conv(xp_ref, cw_ref, cb_ref, xc_ref, None, tb)
    xp_ref[0:SUBLANES, :] = xbc_ref[tb - SUBLANES:tb, :]

    acum, acum_t = _ssd_prepare(dt_ref, dtb_ref, alog_ref, e_ref, causal, last, dtf_ref, tef_ref, eaf_ref)

    gw = yb_ref.shape[1] // SSD_GROUPS
    for g in range(SSD_GROUPS):
        gs = slice(g * gw, (g + 1) * gw)
        yd, cg, bg = _ssd_diag_group(g, xc_ref, dtf_ref, acum, acum_t, causal)
        st = st_ref[:, gs]
        y = yd + _dot(cg, st.astype(BF16)) * eaf_ref[:, gs]
        _finish_group(g, y, xc_ref, z_ref, dsk_ref, sn_ref, yb_ref)
        xw = (xc_ref[:, gs] * tef_ref[:, gs]).astype(BF16)
        st_ref[:, gs] = eaf_ref[tb - 1:tb, gs] * st + _dot(bg.T.astype(BF16), xw)

    @pl.when(c == pl.num_programs(1) - 1)
    def _():
        for k in range(st_ref.shape[1] // LANES):
            hs_ref[k * LANES:(k + 1) * LANES, :] = st_ref[:, k * LANES:(k + 1) * LANES].T
        ct_ref[...] = xbc_ref[tb - SUBLANES:tb, :]


def _sample_body(u_ref, v_ref, z_ref, xbc_ref, dt_ref, h1_ref, h2_ref, h3_ref, h0_ref,
                 gvn_ref, gw_ref, bs_ref, cw_ref, cb_ref, dtb_ref, alog_ref, dsk_ref, sn_ref,
                 ya_ref, yb_ref, vn_ref, hs_ref,
                 wm_ref, e_ref, et_ref, xp_ref, xc_ref, dtf_ref, tef_ref, eaf_ref,
                 y_ref, yoff_ref, cg_ref, bg_ref, xwt_ref, eat_ref, *, seq_len):
    tb = TOKEN_BLOCK
    k = pl.program_id(1)
    causal, last = _block_masks(seq_len)
    gw = yb_ref.shape[1] // SSD_GROUPS

    @pl.when((pl.program_id(0) == 0) & (k == 0))
    def _():
        _init_constants(wm_ref, e_ref, gw_ref, causal)
        crow = _iota2(et_ref.shape, 0)
        hcol = _iota2(et_ref.shape, 1)
        et_ref[...] = jnp.where(crow // SSD_HEAD_DIM == hcol, 1.0, 0.0).astype(BF16)
        xp_ref[0:SUBLANES, :] = jnp.zeros((SUBLANES, xp_ref.shape[1]), F32)

    @pl.when(k == 0)
    def _():
        _gating(u_ref, v_ref, gvn_ref, bs_ref, wm_ref, ya_ref, vn_ref)
        xp_ref[SUBLANES:SUBLANES + tb, :] = xbc_ref[...]
        _conv(xp_ref, cw_ref, cb_ref, xc_ref, (h1_ref, h2_ref, h3_ref), seq_len)
        acum, acum_t = _ssd_prepare(dt_ref, dtb_ref, alog_ref, e_ref, causal, last, dtf_ref, tef_ref, eaf_ref)
        eat_ref[...] = jnp.exp(acum_t)
        for g in range(SSD_GROUPS):
            gs = slice(g * gw, (g + 1) * gw)
            yd, cg, bg = _ssd_diag_group(g, xc_ref, dtf_ref, acum, acum_t, causal)
            y_ref[:, gs] = yd
            cg_ref[:, g * D_STATE:(g + 1) * D_STATE] = cg.astype(F32)
            bg_ref[:, g * D_STATE:(g + 1) * D_STATE] = bg
            xw = xc_ref[:, gs] * tef_ref[:, gs]
            for q in range(gw // LANES):
                r0 = g * gw + q * LANES
                xwt_ref[r0:r0 + LANES, :] = xw[:, q * LANES:(q + 1) * LANES].T.astype(BF16)
        yoff_ref[...] = jnp.zeros_like(yoff_ref)

    nseq = h0_ref.shape[0]
    rowseq = _iota2((tb, 1), 0) // seq_len
    tok = _iota2((tb, LANES), 0)
    et = et_ref[...]
    for b in range(nseq):
        sb = k * nseq + b
        rmask = rowseq == sb
        onehot = jnp.where(tok == sb * seq_len + (seq_len - 1), 1.0, 0.0).astype(BF16)
        cd_heads = _sel_right(eat_ref[...], onehot)
        cd = _sel_left(et, cd_heads)
        for g in range(SSD_GROUPS):
            gs = slice(g * gw, (g + 1) * gw)
            ds = slice(g * D_STATE, (g + 1) * D_STATE)
            h0 = h0_ref[b, gs, :]
            cm = jnp.where(rmask, cg_ref[:, ds], 0.0).astype(BF16)
            yoff_ref[:, gs] += _dot_nt(cm, h0.astype(BF16))
            bm = jnp.where(rmask, bg_ref[:, ds], 0.0).astype(BF16)
            hs_ref[b, gs, :] = cd[gs, :] * h0 + _dot(xwt_ref[gs, :], bm)

    @pl.when(k == pl.num_programs(1) - 1)
    def _():
        for g in range(SSD_GROUPS):
            gs = slice(g * gw, (g + 1) * gw)
            y = y_ref[:, gs] + yoff_ref[:, gs] * eaf_ref[:, gs]
            _finish_group(g, y, xc_ref, z_ref, dsk_ref, sn_ref, yb_ref)


def _mixer_params(lp):
    (gm_v_norm, gm_ws, gm_bs, conv_w, conv_b, dt_bias, a_log, d_skip, ssd_norm) = lp
    nh = dt_bias.shape[0]
    pad = LANES - nh
    return dict(
        gvn=gm_v_norm.reshape(1, -1),
        cw=conv_w,
        cb=conv_b.reshape(1, -1),
        dtb=jnp.pad(dt_bias, (0, pad)).reshape(1, LANES),
        alog=jnp.pad(a_log, (0, pad)).reshape(1, LANES),
        dsk=jnp.repeat(d_skip, SSD_HEAD_DIM).reshape(1, -1),
        sn=ssd_norm.reshape(1, -1),
    )


def _full_spec(a):
    nd = a.ndim
    return pl.BlockSpec(a.shape, lambda i, j: (0,) * nd)


def _mixer_prompt(p1, xbc, dt, lp, n_seq, seq_len, gm_w, ssd_w, conv_dim):
    tb = TOKEN_BLOCK
    nc = seq_len // tb
    mp = _mixer_params(lp)
    gw_full = lp[1]
    bs_t = lp[2].T
    row = lambda b, c: b * nc + c
    params = [mp["gvn"], gw_full, bs_t, mp["cw"], mp["cb"], mp["dtb"], mp["alog"], mp["dsk"], mp["sn"]]
    return pl.pallas_call(
        _prompt_body,
        grid=(n_seq, nc),
        in_specs=[
            pl.BlockSpec((tb, gm_w), lambda b, c: (row(b, c), 0)),
            pl.BlockSpec((tb, gm_w), lambda b, c: (row(b, c), 1)),
            pl.BlockSpec((tb, ssd_w), lambda b, c: (row(b, c), 1)),
            pl.BlockSpec((tb, conv_dim), lambda b, c: (row(b, c), 0)),
            pl.BlockSpec((tb, LANES), lambda b, c: (row(b, c), 0)),
        ] + [_full_spec(a) for a in params],
        out_specs=[
            pl.BlockSpec((tb, gm_w), lambda b, c: (row(b, c), 0)),
            pl.BlockSpec((tb, ssd_w), lambda b, c: (row(b, c), 0)),
            pl.BlockSpec((None, ssd_w, D_STATE), lambda b, c: (b, 0, 0)),
            pl.BlockSpec((None, SUBLANES, conv_dim), lambda b, c: (b, 0, 0)),
        ],
        out_shape=[
            jax.ShapeDtypeStruct((n_seq * seq_len, gm_w), BF16),
            jax.ShapeDtypeStruct((n_seq * seq_len, ssd_w), BF16),
            jax.ShapeDtypeStruct((n_seq, ssd_w, D_STATE), F32),
            jax.ShapeDtypeStruct((n_seq, SUBLANES, conv_dim), F32),
        ],
        scratch_shapes=[
            pltpu.VMEM((GM_HEADS, tb, tb), BF16),
            pltpu.VMEM((LANES, ssd_w), BF16),
            pltpu.VMEM((SUBLANES + tb, conv_dim), F32),
            pltpu.VMEM((tb, conv_dim), F32),
            pltpu.VMEM((tb, ssd_w), F32),
            pltpu.VMEM((tb, ssd_w), F32),
            pltpu.VMEM((tb, ssd_w), F32),
            pltpu.VMEM((D_STATE, ssd_w), F32),
        ],
        compiler_params=_cparams("arbitrary", "arbitrary"),
        name="mixer_prompt",
    )(p1, p1, p1, xbc, dt, *params)


def _mixer_sample(p1, xbc, dt, lp, h0, halos, row0, n_seq, seq_len, gm_w, ssd_w, conv_dim):
    tb = TOKEN_BLOCK
    seqs_per_block = tb // seq_len
    nblk = n_seq // seqs_per_block
    nsub = seqs_per_block // SAMPLE_SEQS_PER_STEP
    blk0 = row0 // tb
    mp = _mixer_params(lp)
    reps = tb // seq_len
    gw_tiled = jnp.tile(lp[1][:, :seq_len, :seq_len], (1, reps, reps))
    bs_t = jnp.tile(lp[2][:, :seq_len], (1, reps)).T
    params = [mp["gvn"], gw_tiled, bs_t, mp["cw"], mp["cb"], mp["dtb"], mp["alog"], mp["dsk"], mp["sn"]]
    ntok = n_seq * seq_len
    return pl.pallas_call(
        functools.partial(_sample_body, seq_len=seq_len),
        grid=(nblk, nsub),
        in_specs=[
            pl.BlockSpec((tb, gm_w), lambda i, k: (blk0 + i, 0)),
            pl.BlockSpec((tb, gm_w), lambda i, k: (blk0 + i, 1)),
            pl.BlockSpec((tb, ssd_w), lambda i, k: (blk0 + i, 1)),
            pl.BlockSpec((tb, conv_dim), lambda i, k: (blk0 + i, 0)),
            pl.BlockSpec((tb, LANES), lambda i, k: (blk0 + i, 0)),
            pl.BlockSpec((tb, conv_dim), lambda i, k: (i, 0)),
            pl.BlockSpec((tb, conv_dim), lambda i, k: (i, 0)),
            pl.BlockSpec((tb, conv_dim), lambda i, k: (i, 0)),
            pl.BlockSpec((SAMPLE_SEQS_PER_STEP, ssd_w, D_STATE), lambda i, k: (i * nsub + k, 0, 0)),
        ] + [_full_spec(a) for a in params],
        out_specs=[
            pl.BlockSpec((tb, gm_w), lambda i, k: (i, 0)),
            pl.BlockSpec((tb, ssd_w), lambda i, k: (i, 0)),
            pl.BlockSpec((tb, gm_w), lambda i, k: (i, 0)),
            pl.BlockSpec((SAMPLE_SEQS_PER_STEP, ssd_w, D_STATE), lambda i, k: (i * nsub + k, 0, 0)),
        ],
        out_shape=[
            jax.ShapeDtypeStruct((ntok, gm_w), BF16),
            jax.ShapeDtypeStruct((ntok, ssd_w), BF16),
            jax.ShapeDtypeStruct((ntok, gm_w), F32),
            jax.ShapeDtypeStruct((n_seq, ssd_w, D_STATE), F32),
        ],
        scratch_shapes=[
            pltpu.VMEM((GM_HEADS, tb, tb), BF16),
            pltpu.VMEM((LANES, ssd_w), BF16),
            pltpu.VMEM((ssd_w, LANES), BF16),
            pltpu.VMEM((SUBLANES + tb, conv_dim), F32),
            pltpu.VMEM((tb, conv_dim), F32),
            pltpu.VMEM((tb, ssd_w), F32),
            pltpu.VMEM((tb, ssd_w), F32),
            pltpu.VMEM((tb, ssd_w), F32),
            pltpu.VMEM((tb, ssd_w), F32),
            pltpu.VMEM((tb, ssd_w), F32),
            pltpu.VMEM((tb, SSD_GROUPS * D_STATE), F32),
            pltpu.VMEM((tb, SSD_GROUPS * D_STATE), F32),
            pltpu.VMEM((ssd_w, tb), BF16),
            pltpu.VMEM((LANES, tb), F32),
        ],
        compiler_params=_cparams("arbitrary", "arbitrary"),
        name="mixer_sample",
    )(p1, p1, p1, xbc, dt, *halos, h0, *params)


def kernel(x_prompt, x_sample, state_ssm, state_conv, ffn1_norm, ffn1_w1, ffn1_w3, ffn1_w2, mix_norm, w_in,
           b_gate, conv_w, conv_b, dt_bias, a_log, d_skip, ssd_norm, gm_v_norm, gm_ws, gm_bs, w_proj_a,
           w_proj_b, w_out, ffn2_norm, ffn2_w1, ffn2_w3, ffn2_w2, final_norm):
    bp, tp, d = x_prompt.shape
    bs_, ts, _ = x_sample.shape
    depth = w_in.shape[0]
    n_heads = dt_bias.shape[1]
    gm_w = gm_v_norm.shape[1]
    ssd_w = ssd_norm.shape[1]
    conv_dim = conv_w.shape[2]
    np_, ns = bp * tp, bs_ * ts

    xs = [x_prompt.reshape(np_, d), x_sample.reshape(ns, d)]
    ssm_p, conv_p, ssm_s, conv_s, v_s = [], [], [], [], []
    for l in range(depth):
        c_uvz = 2 * gm_w + ssd_w
        x, xn = _ffn(xs, ffn1_norm[l], ffn1_w1[l], ffn1_w3[l], ffn1_w2[l], next_norm_w=mix_norm[l])
        p1, xbc, dt = _inproj(xn, w_in[l].T, c_uvz, conv_dim, n_heads)

        lp = (gm_v_norm[l], gm_ws[l], gm_bs[l], conv_w[l], conv_b[l], dt_bias[l], a_log[l], d_skip[l], ssd_norm[l])
        ya_p, yb_p, hs_p, ct_p = _mixer_prompt(p1, xbc, dt, lp, bp, tp, gm_w, ssd_w, conv_dim)

        st = state_conv[l]
        halos = []
        for dd in range(1, CONV_W):
            hz = jnp.concatenate([st[:, CONV_W - 1 - dd:, :], jnp.zeros((bs_, ts - dd, conv_dim), F32)], axis=1)
            halos.append(hz.reshape(ns, conv_dim))
        ya_s, yb_s, vn_s, hs_s = _mixer_sample(p1, xbc, dt, lp, state_ssm[l].reshape(bs_, ssd_w, D_STATE), halos,
                                               np_, bs_, ts, gm_w, ssd_w, conv_dim)

        x = _merge(x, ya_p, yb_p, ya_s, yb_s, p1, b_gate[l], w_proj_a[l].astype(BF16), w_proj_b[l].astype(BF16),
                   w_out[l].astype(BF16))
        last = l == depth - 1
        xs = _ffn([x], ffn2_norm[l], ffn2_w1[l], ffn2_w3[l], ffn2_w2[l], final_w=final_norm if last else None,
                  split_out=True)

        xbc_s = xbc[np_:].reshape(bs_, ts, conv_dim)
        ssm_p.append(hs_p.reshape(bp, n_heads, SSD_HEAD_DIM, D_STATE))
        conv_p.append(ct_p[:, SUBLANES - (CONV_W - 1):, :])
        ssm_s.append(hs_s.reshape(bs_, n_heads, SSD_HEAD_DIM, D_STATE))
        conv_s.append(jnp.concatenate([st, xbc_s], axis=1)[:, ts:, :])
        v_s.append(vn_s.reshape(bs_, ts, gm_w))

    return (xs[0].reshape(bp, tp, d), xs[1].reshape(bs_, ts, d), jnp.stack(ssm_p), jnp.stack(conv_p),
            jnp.stack(ssm_s), jnp.stack(conv_s), jnp.stack(v_s))
```

```python
import functools

import jax
import jax.numpy as jnp
from jax import lax
from jax.experimental import pallas as pl
from jax.experimental.pallas import tpu as pltpu

F32 = jnp.float32
BF16 = jnp.bfloat16
EPS = 1e-6

LANES = 128
SUBLANES = 8
VMEM_LIMIT_BYTES = 60 * 1024 * 1024

GM_HEADS = 8
SSD_HEAD_DIM = 64
SSD_GROUPS = 4
D_STATE = 128
CONV_W = 4
TOKEN_BLOCK = 128
SAMPLE_SEQS_PER_STEP = 4

TOKEN_TILE = 1024
FFN_COLS = 512
PROJ_COLS = 1024
MERGE_TILE = 256
NORM_ROWS = 64

_SINGLE = dict(pipeline_mode=pl.Buffered(1))


def _cparams(*sem):
    return pltpu.CompilerParams(dimension_semantics=sem, vmem_limit_bytes=VMEM_LIMIT_BYTES)


def _gelu(x):
    return x * (0.5 * (1.0 + jnp.tanh(0.7978845608028654 * (x + 0.044715 * (x * x * x)))))


def _silu(x):
    return x * jax.nn.sigmoid(x)


def _softplus(x):
    return jnp.maximum(x, 0.0) + jnp.log1p(jnp.exp(-jnp.abs(x)))


def _split3(a):
    a1 = a.astype(BF16)
    r1 = a - a1.astype(F32)
    a2 = r1.astype(BF16)
    r2 = r1 - a2.astype(F32)
    return a1, a2, r2.astype(BF16)


def _dot(a, b):
    return jnp.dot(a, b, preferred_element_type=F32)


def _dot_nt(a, b):
    return lax.dot_general(a, b, (((1,), (1,)), ((), ())), preferred_element_type=F32)


def _sel_right(a, sel):
    a1, a2, a3 = _split3(a)
    return (_dot(a1, sel) + _dot(a2, sel)) + _dot(a3, sel)


def _sel_left(sel, a):
    a1, a2, a3 = _split3(a)
    return (_dot(sel, a1) + _dot(sel, a2)) + _dot(sel, a3)


def _row_loop(nrows, fn):
    def body(i, c):
        fn(pl.ds(pl.multiple_of(i * NORM_ROWS, NORM_ROWS), NORM_ROWS))
        return c

    lax.fori_loop(0, nrows // NORM_ROWS, body, 0)


def _rmsnorm_rows(x_ref, w_ref, out_ref, nrows):
    def one(sl):
        x = x_ref[sl, :]
        r = lax.rsqrt(jnp.mean(x * x, axis=-1, keepdims=True) + EPS)
        out_ref[sl, :] = ((x * r) * w_ref[...]).astype(out_ref.dtype)

    _row_loop(nrows, one)


def _tile_branches(n_full, tile, tail_rows, run, axis=0):
    i = pl.program_id(axis)

    @pl.when(i < n_full)
    def _():
        run(False, tile)

    if tail_rows:
        @pl.when(i >= n_full)
        def _():
            run(True, tail_rows)


def _ffn_body(*refs, split_in, split_out, final_norm, emit_norm, n_full, tail_rows):
    refs = list(refs)
    x_refs = [refs.pop(0) for _ in range(2 if split_in else 1)]
    nw_ref, w1_ref, w3_ref, w2_ref, fn_ref = [refs.pop(0) for _ in range(5)]
    o_refs = [refs.pop(0) for _ in range(2 if split_out else 1)]
    xno_ref = refs.pop(0) if emit_norm else None
    xn_ref, sem = refs
    i = pl.program_id(0)
    j = pl.program_id(1)
    last_j = pl.num_programs(1) - 1
    tm = TOKEN_TILE

    def run(is_tail, rows):
        o_ref = o_refs[-1] if is_tail else o_refs[0]

        @pl.when(j == 0)
        def _():
            if is_tail and split_in:
                src = x_refs[1]
            elif is_tail:
                src = x_refs[0].at[pl.ds(n_full * tm, rows), :]
            else:
                src = x_refs[0].at[pl.ds(pl.multiple_of(i * tm, tm), rows), :]
            copy = pltpu.make_async_copy(src, o_ref.at[pl.ds(0, rows), :], sem)
            copy.start()
            copy.wait()
            _rmsnorm_rows(o_ref, nw_ref, xn_ref, rows)

        xn = xn_ref[0:rows, :]
        h1 = _dot(xn, w1_ref[...].astype(BF16))
        h3 = _dot(xn, w3_ref[...].astype(BF16))
        g = ((0.5 * _silu(h1)) * h3).astype(BF16)
        o_ref[0:rows, :] += _dot(g, w2_ref[...].astype(BF16))

        if final_norm:
            @pl.when(j == last_j)
            def _():
                _rmsnorm_rows(o_ref, fn_ref, o_ref, rows)

        if emit_norm:
            @pl.when(j == last_j)
            def _():
                _rmsnorm_rows(o_ref, fn_ref, xno_ref, rows)

    _tile_branches(n_full, TOKEN_TILE, tail_rows, run)


def _ffn(xs, norm_w, w1, w3, w2, *, final_w=None, next_norm_w=None, split_out=False):
    split_in = len(xs) == 2
    d, f = w1.shape
    tm, tf = TOKEN_TILE, FFN_COLS
    n = sum(x.shape[0] for x in xs)
    n_full, tail_rows = divmod(n, tm)
    n_tiles = n_full + (1 if tail_rows else 0)
    if split_in:
        assert tail_rows == xs[-1].shape[0]
    final_norm = final_w is not None
    emit_norm = next_norm_w is not None
    assert not (final_norm and emit_norm)
    fw = final_w if final_norm else (next_norm_w if emit_norm else norm_w)
    last_full = max(n_full - 1, 0)
    x_specs = [pl.BlockSpec(memory_space=pl.ANY) for _ in xs]
    if split_out:
        o_specs = [pl.BlockSpec((tm, d), lambda i, j: (jnp.minimum(i, last_full), 0)),
                   pl.BlockSpec((tail_rows, d), lambda i, j: (0, 0), **_SINGLE)]
        o_shapes = [jax.ShapeDtypeStruct((n_full * tm, d), F32), jax.ShapeDtypeStruct((tail_rows, d), F32)]
    else:
        o_specs = [pl.BlockSpec((tm, d), lambda i, j: (i, 0))]
        o_shapes = [jax.ShapeDtypeStruct((n, d), F32)]
    if emit_norm:
        o_specs.append(pl.BlockSpec((tm, d), lambda i, j: (i, 0), **_SINGLE))
        o_shapes.append(jax.ShapeDtypeStruct((n, d), BF16))
    return pl.pallas_call(
        functools.partial(_ffn_body, split_in=split_in, split_out=split_out, final_norm=final_norm,
                          emit_norm=emit_norm, n_full=n_full, tail_rows=tail_rows),
        grid=(n_tiles, f // tf),
        in_specs=x_specs + [
            pl.BlockSpec((1, d), lambda i, j: (0, 0)),
            pl.BlockSpec((d, tf), lambda i, j: (0, j)),
            pl.BlockSpec((d, tf), lambda i, j: (0, j)),
            pl.BlockSpec((tf, d), lambda i, j: (j, 0)),
            pl.BlockSpec((1, d), lambda i, j: (0, 0)),
        ],
        out_specs=o_specs,
        out_shape=o_shapes,
        scratch_shapes=[pltpu.VMEM((tm, d), BF16), pltpu.SemaphoreType.DMA(())],
        compiler_params=_cparams("arbitrary", "arbitrary"),
        name="ffn",
    )(*xs, norm_w.reshape(1, d), w1, w3, w2, fw.reshape(1, d))


def _inproj_body(xn_ref, w_ref, wdt_ref, p1_ref, xbc_ref, dt_ref, wb_ref, *,
                 n_full, tail_rows, n_uvz, n_xbc, n_dt):
    j = pl.program_id(0)
    is_xbc = (j >= n_uvz) & (j < n_uvz + n_xbc)

    @pl.when(pl.program_id(1) == 0)
    def _():
        wb_ref[...] = w_ref[...].astype(BF16)

    def run(is_tail, rows):
        @pl.when(j == 0)
        def _():
            dt = _dot_nt(xn_ref[0:rows, :], wdt_ref[...].astype(BF16))
            lane = _iota2(dt.shape, 1)
            dt_ref[0:rows, :] = jnp.where(lane < n_dt, dt, 0.0)

        @pl.when(jnp.logical_not(is_xbc))
        def _():
            p1_ref[0:rows, :] = _dot_nt(xn_ref[0:rows, :], wb_ref[...]).astype(p1_ref.dtype)

        @pl.when(is_xbc)
        def _():
            xbc_ref[0:rows, :] = _dot_nt(xn_ref[0:rows, :], wb_ref[...])

    _tile_branches(n_full, TOKEN_TILE, tail_rows, run, axis=1)


def _inproj(xn, w_t, c_uvz, c_xbc, c_dt):
    n, d = xn.shape
    tm, tn = TOKEN_TILE, PROJ_COLS
    n_full, tail_rows = divmod(n, tm)
    n_tiles = n_full + (1 if tail_rows else 0)
    last_i = n_tiles - 1
    c_gate0 = c_uvz + c_xbc + c_dt
    c_gates = w_t.shape[0] - c_gate0
    n_uvz, n_xbc, n_g = c_uvz // tn, c_xbc // tn, c_gates // tn
    n_main = n_uvz + n_xbc
    assert c_gate0 % SUBLANES == 0 and (c_uvz + c_xbc) % SUBLANES == 0

    def w_row(j, i):
        return (pl.multiple_of(jnp.where(j < n_main, j * tn, c_gate0 + (j - n_main) * tn), SUBLANES), 0)

    def p1_idx(j, i):
        writes = (j < n_uvz) | (j >= n_main)
        col = jnp.where(j < n_uvz, j, jnp.maximum(j - n_xbc, n_uvz - 1))
        return (jnp.where(writes, i, last_i), col)

    def xbc_idx(j, i):
        row = jnp.where(j < n_uvz, 0, jnp.where(j < n_main, i, last_i))
        return (row, jnp.clip(j - n_uvz, 0, n_xbc - 1))

    return pl.pallas_call(
        functools.partial(_inproj_body, n_full=n_full, tail_rows=tail_rows, n_uvz=n_uvz, n_xbc=n_xbc, n_dt=c_dt),
        grid=(n_main + n_g, n_tiles),
        in_specs=[
            pl.BlockSpec((tm, d), lambda j, i: (i, 0)),
            pl.BlockSpec((pl.Element(tn), pl.Element(d)), w_row),
            pl.BlockSpec((pl.Element(LANES), pl.Element(d)), lambda j, i: (c_uvz + c_xbc, 0)),
        ],
        out_specs=[
            pl.BlockSpec((tm, tn), p1_idx),
            pl.BlockSpec((tm, tn), xbc_idx),
            pl.BlockSpec((tm, LANES), lambda j, i: (jnp.where(j == 0, i, last_i), 0)),
        ],
        out_shape=[
            jax.ShapeDtypeStruct((n, c_uvz + c_gates), BF16),
            jax.ShapeDtypeStruct((n, c_xbc), F32),
            jax.ShapeDtypeStruct((n, LANES), F32),
        ],
        scratch_shapes=[pltpu.VMEM((tn, d), BF16)],
        compiler_params=_cparams("arbitrary", "arbitrary"),
        name="in_proj",
    )(xn, w_t, w_t)


def _merge_body(x_ref, yap_ref, ybp_ref, yas_ref, ybs_ref, ga_ref, gb_ref, bg_ref, wa_ref, wb_ref, wo_ref, o_ref,
                *, n_prompt_tiles):
    def run(ya_ref, yb_ref):
        pa = _dot(ya_ref[...], wa_ref[...])
        pb = _dot(yb_ref[...], wb_ref[...])
        ga = jax.nn.sigmoid(ga_ref[...].astype(F32) + bg_ref[0:1, :])
        gb = jax.nn.sigmoid(gb_ref[...].astype(F32) + bg_ref[1:2, :])
        m = (ga * pa + gb * pb).astype(BF16)
        o_ref[...] = x_ref[...] + _dot(m, wo_ref[...])

    i = pl.program_id(0)

    @pl.when(i < n_prompt_tiles)
    def _():
        run(yap_ref, ybp_ref)

    @pl.when(i >= n_prompt_tiles)
    def _():
        run(yas_ref, ybs_ref)


def _merge(x, ya_p, yb_p, ya_s, yb_s, p1, b_gate, wa, wb, wo):
    n, d = x.shape
    tm = MERGE_TILE
    npt = ya_p.shape[0] // tm
    nst = ya_s.shape[0] // tm
    gcol = p1.shape[1] // d - 2
    p_idx = lambda i: (jnp.minimum(i, npt - 1), 0)
    s_idx = lambda i: (jnp.clip(i - npt, 0, nst - 1), 0)
    return pl.pallas_call(
        functools.partial(_merge_body, n_prompt_tiles=npt),
        grid=(n // tm,),
        in_specs=[
            pl.BlockSpec((tm, d), lambda i: (i, 0)),
            pl.BlockSpec((tm, ya_p.shape[1]), p_idx),
            pl.BlockSpec((tm, yb_p.shape[1]), p_idx),
            pl.BlockSpec((tm, ya_s.shape[1]), s_idx),
            pl.BlockSpec((tm, yb_s.shape[1]), s_idx),
            pl.BlockSpec((tm, d), lambda i: (i, gcol)),
            pl.BlockSpec((tm, d), lambda i: (i, gcol + 1)),
            pl.BlockSpec((2, d), lambda i: (0, 0)),
            pl.BlockSpec(wa.shape, lambda i: (0, 0), **_SINGLE),
            pl.BlockSpec(wb.shape, lambda i: (0, 0), **_SINGLE),
            pl.BlockSpec(wo.shape, lambda i: (0, 0), **_SINGLE),
        ],
        out_specs=pl.BlockSpec((tm, d), lambda i: (i, 0)),
        out_shape=jax.ShapeDtypeStruct((n, d), F32),
        compiler_params=_cparams("arbitrary"),
        name="merge",
    )(x, ya_p, yb_p, ya_s, yb_s, p1, p1, b_gate, wa, wb, wo)


def _iota2(shape, dim):
    return lax.broadcasted_iota(jnp.int32, shape, dim)


def _block_masks(seq_len):
    tb = TOKEN_BLOCK
    row = _iota2((tb, tb), 0)
    col = _iota2((tb, tb), 1)
    if seq_len >= tb:
        causal = col <= row
        last = col == tb - 1
    else:
        causal = (col <= row) & ((row // seq_len) == (col // seq_len))
        last = col == (row // seq_len) * seq_len + (seq_len - 1)
    return causal, last


def _init_constants(wm_ref, e_ref, gw_ref, causal):
    for h in range(GM_HEADS):
        wm_ref[h] = jnp.where(causal, gw_ref[h], 0.0).astype(BF16)
    hrow = _iota2(e_ref.shape, 0)
    ccol = _iota2(e_ref.shape, 1)
    e_ref[...] = jnp.where(hrow == ccol // SSD_HEAD_DIM, 1.0, 0.0).astype(BF16)


def _gating(u_ref, v_ref, gvn_ref, bs_ref, wm_ref, ya_ref, vn_ref):
    vg = _gelu(v_ref[...].astype(F32))
    r = lax.rsqrt(jnp.mean(vg * vg, axis=-1, keepdims=True) + EPS)
    vn = (vg * r) * gvn_ref[...]
    if vn_ref is not None:
        vn_ref[...] = vn
    vnb = vn.astype(BF16)
    for h in range(GM_HEADS):
        cs = slice(h * LANES, (h + 1) * LANES)
        s = _dot(wm_ref[h], vnb[:, cs]) + bs_ref[:, h:h + 1]
        ya_ref[:, cs] = (_gelu(u_ref[:, cs].astype(F32)) * s).astype(ya_ref.dtype)


def _conv(xp_ref, cw_ref, cb_ref, xc_ref, halo_refs, seq_len):
    tb = TOKEN_BLOCK
    width = xc_ref.shape[1]
    cw = 512
    tpos = _iota2((tb, 1), 0) % seq_len if seq_len < tb else None
    for c0 in range(0, width, cw):
        cs = slice(c0, c0 + cw)
        acc = None
        for k in range(CONV_W):
            d = CONV_W - 1 - k
            xs = xp_ref[SUBLANES - d:SUBLANES - d + tb, cs]
            if tpos is not None and d > 0:
                xs = jnp.where(tpos >= d, xs, 0.0) + halo_refs[d - 1][:, cs]
            term = cw_ref[k:k + 1, cs] * xs
            acc = term if acc is None else acc + term
        xc_ref[:, cs] = _silu(cb_ref[:, cs] + acc)


def _ssd_prepare(dt_ref, dtb_ref, alog_ref, e_ref, causal, last, dtf_ref, tef_ref, eaf_ref):
    dt = _softplus(dt_ref[...] + dtb_ref[...])
    a = -jnp.exp(alog_ref[...])
    da = dt * a
    lmat = jnp.where(causal, 1.0, 0.0).astype(BF16)
    acum = _sel_left(lmat, da)
    a_last = _sel_left(jnp.where(last, 1.0, 0.0).astype(BF16), acum)
    ea = jnp.exp(acum)
    e = e_ref[...]
    dtf_ref[...] = _sel_right(dt, e)
    tef_ref[...] = _sel_right(dt * jnp.exp(a_last - acum), e)
    eaf_ref[...] = _sel_right(ea, e)
    return acum, acum.T


def _ssd_diag_group(g, xc_ref, dtf_ref, acum, acum_t, causal):
    tb = TOKEN_BLOCK
    width = xc_ref.shape[1] - 2 * SSD_GROUPS * D_STATE
    gw = width // SSD_GROUPS
    bg = xc_ref[:, width + g * D_STATE:width + (g + 1) * D_STATE]
    cg = xc_ref[:, width + (SSD_GROUPS + g) * D_STATE:width + (SSD_GROUPS + g + 1) * D_STATE].astype(BF16)
    cb = _dot_nt(cg, bg.astype(BF16))
    gs = slice(g * gw, (g + 1) * gw)
    xdt = (xc_ref[:, gs] * dtf_ref[:, gs]).astype(BF16)
    lane = _iota2((tb, LANES), 1)
    heads_per_group = gw // SSD_HEAD_DIM
    outs = []
    for jp in range(heads_per_group // 2):
        ms = []
        for hh in range(2):
            h = g * heads_per_group + 2 * jp + hh
            seg = acum[:, h:h + 1] - acum_t[h:h + 1, :]
            dec = jnp.exp(jnp.where(causal, seg, -jnp.inf))
            ms.append((cb * dec).astype(BF16))
        lhs = jnp.concatenate(ms, axis=1)
        xpair = xdt[:, jp * LANES:(jp + 1) * LANES]
        zero = jnp.zeros_like(xpair)
        rhs = jnp.concatenate([jnp.where(lane < SSD_HEAD_DIM, xpair, zero),
                               jnp.where(lane >= SSD_HEAD_DIM, xpair, zero)], axis=0)
        outs.append(_dot(lhs, rhs))
    return jnp.concatenate(outs, axis=1), cg, bg


def _finish_group(g, y, xc_ref, z_ref, dsk_ref, sn_ref, yb_ref):
    gw = yb_ref.shape[1] // SSD_GROUPS
    gs = slice(g * gw, (g + 1) * gw)
    y = y + dsk_ref[:, gs] * xc_ref[:, gs]
    yz = y * _silu(z_ref[:, gs].astype(F32))
    r = lax.rsqrt(jnp.mean(yz * yz, axis=-1, keepdims=True) + EPS)
    yb_ref[:, gs] = ((yz * r) * sn_ref[:, gs]).astype(yb_ref.dtype)


def _prompt_body(u_ref, v_ref, z_ref, xbc_ref, dt_ref, gvn_ref, gw_ref, bs_ref, cw_ref, cb_ref,
                 dtb_ref, alog_ref, dsk_ref, sn_ref,
                 ya_ref, yb_ref, hs_ref, ct_ref,
                 wm_ref, e_ref, xp_ref, xc_ref, dtf_ref, tef_ref, eaf_ref, st_ref):
    tb = TOKEN_BLOCK
    c = pl.program_id(1)
    causal, last = _block_masks(tb)

    @pl.when((pl.program_id(0) == 0) & (c == 0))
    def _():
        _init_constants(wm_ref, e_ref, gw_ref, causal)

    @pl.when(c == 0)
    def _():
        st_ref[...] = jnp.zeros_like(st_ref)
        xp_ref[0:SUBLANES, :] = jnp.zeros((SUBLANES, xp_ref.shape[1]), F32)

    _gating(u_ref, v_ref, gvn_ref, bs_ref, wm_ref, ya_ref, None)

    xp_ref[SUBLANES:SUBLANES + tb, :] = xbc_ref[...]
    _conv(xp_ref, cw_ref, cb_ref, xc_ref, None, tb)
    xp_ref[0:SUBLANES, :] = xbc_ref[tb - SUBLANES:tb, :]

    acum, acum_t = _ssd_prepare(dt_ref, dtb_ref, alog_ref, e_ref, causal, last, dtf_ref, tef_ref, eaf_ref)

    gw = yb_ref.shape[1] // SSD_GROUPS
    for g in range(SSD_GROUPS):
        gs = slice(g * gw, (g + 1) * gw)
        yd, cg, bg = _ssd_diag_group(g, xc_ref, dtf_ref, acum, acum_t, causal)
        st = st_ref[:, gs]
        y = yd + _dot(cg, st.astype(BF16)) * eaf_ref[:, gs]
        _finish_group(g, y, xc_ref, z_ref, dsk_ref, sn_ref, yb_ref)
        xw = (xc_ref[:, gs] * tef_ref[:, gs]).astype(BF16)
        st_ref[:, gs] = eaf_ref[tb - 1:tb, gs] * st + _dot(bg.T.astype(BF16), xw)

    @pl.when(c == pl.num_programs(1) - 1)
    def _():
        for k in range(st_ref.shape[1] // LANES):
            hs_ref[k * LANES:(k + 1) * LANES, :] = st_ref[:, k * LANES:(k + 1) * LANES].T
        ct_ref[...] = xbc_ref[tb - SUBLANES:tb, :]


def _sample_body(u_ref, v_ref, z_ref, xbc_ref, dt_ref, h1_ref, h2_ref, h3_ref, h0_ref,
                 gvn_ref, gw_ref, bs_ref, cw_ref, cb_ref, dtb_ref, alog_ref, dsk_ref, sn_ref,
                 ya_ref, yb_ref, vn_ref, hs_ref,
                 wm_ref, e_ref, et_ref, xp_ref, xc_ref, dtf_ref, tef_ref, eaf_ref,
                 y_ref, yoff_ref, cg_ref, bg_ref, xwt_ref, eat_ref, *, seq_len):
    tb = TOKEN_BLOCK
    k = pl.program_id(1)
    causal, last = _block_masks(seq_len)
    gw = yb_ref.shape[1] // SSD_GROUPS

    @pl.when((pl.program_id(0) == 0) & (k == 0))
    def _():
        _init_constants(wm_ref, e_ref, gw_ref, causal)
        crow = _iota2(et_ref.shape, 0)
        hcol = _iota2(et_ref.shape, 1)
        et_ref[...] = jnp.where(crow // SSD_HEAD_DIM == hcol, 1.0, 0.0).astype(BF16)
        xp_ref[0:SUBLANES, :] = jnp.zeros((SUBLANES, xp_ref.shape[1]), F32)

    @pl.when(k == 0)
    def _():
        _gating(u_ref, v_ref, gvn_ref, bs_ref, wm_ref, ya_ref, vn_ref)
        xp_ref[SUBLANES:SUBLANES + tb, :] = xbc_ref[...]
        _conv(xp_ref, cw_ref, cb_ref, xc_ref, (h1_ref, h2_ref, h3_ref), seq_len)
        acum, acum_t = _ssd_prepare(dt_ref, dtb_ref, alog_ref, e_ref, causal, last, dtf_ref, tef_ref, eaf_ref)
        eat_ref[...] = jnp.exp(acum_t)
        for g in range(SSD_GROUPS):
            gs = slice(g * gw, (g + 1) * gw)
            yd, cg, bg = _ssd_diag_group(g, xc_ref, dtf_ref, acum, acum_t, causal)
            y_ref[:, gs] = yd
            cg_ref[:, g * D_STATE:(g + 1) * D_STATE] = cg.astype(F32)
            bg_ref[:, g * D_STATE:(g + 1) * D_STATE] = bg
            xw = xc_ref[:, gs] * tef_ref[:, gs]
            for q in range(gw // LANES):
                r0 = g * gw + q * LANES
                xwt_ref[r0:r0 + LANES, :] = xw[:, q * LANES:(q + 1) * LANES].T.astype(BF16)
        yoff_ref[...] = jnp.zeros_like(yoff_ref)

    nseq = h0_ref.shape[0]
    rowseq = _iota2((tb, 1), 0) // seq_len
    tok = _iota2((tb, LANES), 0)
    et = et_ref[...]
    for b in range(nseq):
        sb = k * nseq + b
        rmask = rowseq == sb
        onehot = jnp.where(tok == sb * seq_len + (seq_len - 1), 1.0, 0.0).astype(BF16)
        cd_heads = _sel_right(eat_ref[...], onehot)
        cd = _sel_left(et, cd_heads)
        for g in range(SSD_GROUPS):
            gs = slice(g * gw, (g + 1) * gw)
            ds = slice(g * D_STATE, (g + 1) * D_STATE)
            h0 = h0_ref[b, gs, :]
            cm = jnp.where(rmask, cg_ref[:, ds], 0.0).astype(BF16)
            yoff_ref[:, gs] += _dot_nt(cm, h0.astype(BF16))
            bm = jnp.where(rmask, bg_ref[:, ds], 0.0).astype(BF16)
            hs_ref[b, gs, :] = cd[gs, :] * h0 + _dot(xwt_ref[gs, :], bm)

    @pl.when(k == pl.num_programs(1) - 1)
    def _():
        for g in range(SSD_GROUPS):
            gs = slice(g * gw, (g + 1) * gw)
            y = y_ref[:, gs] + yoff_ref[:, gs] * eaf_ref[:, gs]
            _finish_group(g, y, xc_ref, z_ref, dsk_ref, sn_ref, yb_ref)


def _mixer_params(lp):
    (gm_v_norm, gm_ws, gm_bs, conv_w, conv_b, dt_bias, a_log, d_skip, ssd_norm) = lp
    nh = dt_bias.shape[0]
    pad = LANES - nh
    return dict(
        gvn=gm_v_norm.reshape(1, -1),
        cw=conv_w,
        cb=conv_b.reshape(1, -1),
        dtb=jnp.pad(dt_bias, (0, pad)).reshape(1, LANES),
        alog=jnp.pad(a_log, (0, pad)).reshape(1, LANES),
        dsk=jnp.repeat(d_skip, SSD_HEAD_DIM).reshape(1, -1),
        sn=ssd_norm.reshape(1, -1),
    )


def _full_spec(a):
    nd = a.ndim
    return pl.BlockSpec(a.shape, lambda i, j: (0,) * nd)


def _mixer_prompt(p1, xbc, dt, lp, n_seq, seq_len, gm_w, ssd_w, conv_dim):
    tb = TOKEN_BLOCK
    nc = seq_len // tb
    mp = _mixer_params(lp)
    gw_full = lp[1]
    bs_t = lp[2].T
    row = lambda b, c: b * nc + c
    params = [mp["gvn"], gw_full, bs_t, mp["cw"], mp["cb"], mp["dtb"], mp["alog"], mp["dsk"], mp["sn"]]
    return pl.pallas_call(
        _prompt_body,
        grid=(n_seq, nc),
        in_specs=[
            pl.BlockSpec((tb, gm_w), lambda b, c: (row(b, c), 0)),
            pl.BlockSpec((tb, gm_w), lambda b, c: (row(b, c), 1)),
            pl.BlockSpec((tb, ssd_w), lambda b, c: (row(b, c), 1)),
            pl.BlockSpec((tb, conv_dim), lambda b, c: (row(b, c), 0)),
            pl.BlockSpec((tb, LANES), lambda b, c: (row(b, c), 0)),
        ] + [_full_spec(a) for a in params],
        out_specs=[
            pl.BlockSpec((tb, gm_w), lambda b, c: (row(b, c), 0)),
            pl.BlockSpec((tb, ssd_w), lambda b, c: (row(b, c), 0)),
            pl.BlockSpec((None, ssd_w, D_STATE), lambda b, c: (b, 0, 0)),
            pl.BlockSpec((None, SUBLANES, conv_dim), lambda b, c: (b, 0, 0)),
        ],
        out_shape=[
            jax.ShapeDtypeStruct((n_seq * seq_len, gm_w), BF16),
            jax.ShapeDtypeStruct((n_seq * seq_len, ssd_w), BF16),
            jax.ShapeDtypeStruct((n_seq, ssd_w, D_STATE), F32),
            jax.ShapeDtypeStruct((n_seq, SUBLANES, conv_dim), F32),
        ],
        scratch_shapes=[
            pltpu.VMEM((GM_HEADS, tb, tb), BF16),
            pltpu.VMEM((LANES, ssd_w), BF16),
            pltpu.VMEM((SUBLANES + tb, conv_dim), F32),
            pltpu.VMEM((tb, conv_dim), F32),
            pltpu.VMEM((tb, ssd_w), F32),
            pltpu.VMEM((tb, ssd_w), F32),
            pltpu.VMEM((tb, ssd_w), F32),
            pltpu.VMEM((D_STATE, ssd_w), F32),
        ],
        compiler_params=_cparams("arbitrary", "arbitrary"),
        name="mixer_prompt",
    )(p1, p1, p1, xbc, dt, *params)


def _mixer_sample(p1, xbc, dt, lp, h0, halos, row0, n_seq, seq_len, gm_w, ssd_w, conv_dim):
    tb = TOKEN_BLOCK
    seqs_per_block = tb // seq_len
    nblk = n_seq // seqs_per_block
    nsub = seqs_per_block // SAMPLE_SEQS_PER_STEP
    blk0 = row0 // tb
    mp = _mixer_params(lp)
    reps = tb // seq_len
    gw_tiled = jnp.tile(lp[1][:, :seq_len, :seq_len], (1, reps, reps))
    bs_t = jnp.tile(lp[2][:, :seq_len], (1, reps)).T
    params = [mp["gvn"], gw_tiled, bs_t, mp["cw"], mp["cb"], mp["dtb"], mp["alog"], mp["dsk"], mp["sn"]]
    ntok = n_seq * seq_len
    return pl.pallas_call(
        functools.partial(_sample_body, seq_len=seq_len),
        grid=(nblk, nsub),
        in_specs=[
            pl.BlockSpec((tb, gm_w), lambda i, k: (blk0 + i, 0)),
            pl.BlockSpec((tb, gm_w), lambda i, k: (blk0 + i, 1)),
            pl.BlockSpec((tb, ssd_w), lambda i, k: (blk0 + i, 1)),
            pl.BlockSpec((tb, conv_dim), lambda i, k: (blk0 + i, 0)),
            pl.BlockSpec((tb, LANES), lambda i, k: (blk0 + i, 0)),
            pl.BlockSpec((tb, conv_dim), lambda i, k: (i, 0)),
            pl.BlockSpec((tb, conv_dim), lambda i, k: (i, 0)),
            pl.BlockSpec((tb, conv_dim), lambda i, k: (i, 0)),
            pl.BlockSpec((SAMPLE_SEQS_PER_STEP, ssd_w, D_STATE), lambda i, k: (i * nsub + k, 0, 0)),
        ] + [_full_spec(a) for a in params],
        out_specs=[
            pl.BlockSpec((tb, gm_w), lambda i, k: (i, 0)),
            pl.BlockSpec((tb, ssd_w), lambda i, k: (i, 0)),
            pl.BlockSpec((tb, gm_w), lambda i, k: (i, 0)),
            pl.BlockSpec((SAMPLE_SEQS_PER_STEP, ssd_w, D_STATE), lambda i, k: (i * nsub + k, 0, 0)),
        ],
        out_shape=[
            jax.ShapeDtypeStruct((ntok, gm_w), BF16),
            jax.ShapeDtypeStruct((ntok, ssd_w), BF16),
            jax.ShapeDtypeStruct((ntok, gm_w), F32),
            jax.ShapeDtypeStruct((n_seq, ssd_w, D_STATE), F32),
        ],
        scratch_shapes=[
            pltpu.VMEM((GM_HEADS, tb, tb), BF16),
            pltpu.VMEM((LANES, ssd_w), BF16),
            pltpu.VMEM((ssd_w, LANES), BF16),
            pltpu.VMEM((SUBLANES + tb, conv_dim), F32),
            pltpu.VMEM((tb, conv_dim), F32),
            pltpu.VMEM((tb, ssd_w), F32),
            pltpu.VMEM((tb, ssd_w), F32),
            pltpu.VMEM((tb, ssd_w), F32),
            pltpu.VMEM((tb, ssd_w), F32),
            pltpu.VMEM((tb, ssd_w), F32),
            pltpu.VMEM((tb, SSD_GROUPS * D_STATE), F32),
            pltpu.VMEM((tb, SSD_GROUPS * D_STATE), F32),
            pltpu.VMEM((ssd_w, tb), BF16),
            pltpu.VMEM((LANES, tb), F32),
        ],
        compiler_params=_cparams("arbitrary", "arbitrary"),
        name="mixer_sample",
    )(p1, p1, p1, xbc, dt, *halos, h0, *params)


def kernel(x_prompt, x_sample, state_ssm, state_conv, ffn1_norm, ffn1_w1, ffn1_w3, ffn1_w2, mix_norm, w_in,
           b_gate, conv_w, conv_b, dt_bias, a_log, d_skip, ssd_norm, gm_v_norm, gm_ws, gm_bs, w_proj_a,
           w_proj_b, w_out, ffn2_norm, ffn2_w1, ffn2_w3, ffn2_w2, final_norm):
    bp, tp, d = x_prompt.shape
    bs_, ts, _ = x_sample.shape
    depth = w_in.shape[0]
    n_heads = dt_bias.shape[1]
    gm_w = gm_v_norm.shape[1]
    ssd_w = ssd_norm.shape[1]
    conv_dim = conv_w.shape[2]
    np_, ns = bp * tp, bs_ * ts

    xs = [x_prompt.reshape(np_, d), x_sample.reshape(ns, d)]
    ssm_p, conv_p, ssm_s, conv_s, v_s = [], [], [], [], []
    for l in range(depth):
        c_uvz = 2 * gm_w + ssd_w
        x, xn = _ffn(xs, ffn1_norm[l], ffn1_w1[l], ffn1_w3[l], ffn1_w2[l], next_norm_w=mix_norm[l])
        p1, xbc, dt = _inproj(xn, w_in[l].T, c_uvz, conv_dim, n_heads)

        lp = (gm_v_norm[l], gm_ws[l], gm_bs[l], conv_w[l], conv_b[l], dt_bias[l], a_log[l], d_skip[l], ssd_norm[l])
        ya_p, yb_p, hs_p, ct_p = _mixer_prompt(p1, xbc, dt, lp, bp, tp, gm_w, ssd_w, conv_dim)

        st = state_conv[l]
        halos = []
        for dd in range(1, CONV_W):
            hz = jnp.concatenate([st[:, CONV_W - 1 - dd:, :], jnp.zeros((bs_, ts - dd, conv_dim), F32)], axis=1)
            halos.append(hz.reshape(ns, conv_dim))
        ya_s, yb_s, vn_s, hs_s = _mixer_sample(p1, xbc, dt, lp, state_ssm[l].reshape(bs_, ssd_w, D_STATE), halos,
                                               np_, bs_, ts, gm_w, ssd_w, conv_dim)

        x = _merge(x, ya_p, yb_p, ya_s, yb_s, p1, b_gate[l], w_proj_a[l].astype(BF16), w_proj_b[l].astype(BF16),
                   w_out[l].astype(BF16))
        last = l == depth - 1
        xs = _ffn([x], ffn2_norm[l], ffn2_w1[l], ffn2_w3[l], ffn2_w2[l], final_w=final_norm if last else None,
                  split_out=True)

        xbc_s = xbc[np_:].reshape(bs_, ts, conv_dim)
        ssm_p.append(hs_p.reshape(bp, n_heads, SSD_HEAD_DIM, D_STATE))
        conv_p.append(ct_p[:, SUBLANES - (CONV_W - 1):, :])
        ssm_s.append(hs_s.reshape(bs_, n_heads, SSD_HEAD_DIM, D_STATE))
        conv_s.append(jnp.concatenate([st, xbc_s], axis=1)[:, ts:, :])
        v_s.append(vn_s.reshape(bs_, ts, gm_w))

    return (xs[0].reshape(bp, tp, d), xs[1].reshape(bs_, ts, d), jnp.stack(ssm_p), jnp.stack(conv_p),
            jnp.stack(ssm_s), jnp.stack(conv_s), jnp.stack(v_s))
```

```python
import functools

import jax
import jax.numpy as jnp
from jax import lax
from jax.experimental import pallas as pl
from jax.experimental.pallas import tpu as pltpu

F32 = jnp.float32
BF16 = jnp.bfloat16
EPS = 1e-6

LANES = 128
SUBLANES = 8
VMEM_LIMIT_BYTES = 60 * 1024 * 1024

GM_HEADS = 8
SSD_HEAD_DIM = 64
SSD_GROUPS = 4
D_STATE = 128
CONV_W = 4
TOKEN_BLOCK = 128
SAMPLE_SEQS_PER_STEP = 4

TOKEN_TILE = 1024
FFN_COLS = 512
PROJ_COLS = 1024
MERGE_TILE = 256
NORM_ROWS = 64

_SINGLE = dict(pipeline_mode=pl.Buffered(1))


def _cparams(*sem):
    return pltpu.CompilerParams(dimension_semantics=sem, vmem_limit_bytes=VMEM_LIMIT_BYTES)


def _gelu(x):
    return x * (0.5 * (1.0 + jnp.tanh(0.7978845608028654 * (x + 0.044715 * (x * x * x)))))


def _silu(x):
    return x * jax.nn.sigmoid(x)


def _softplus(x):
    return jnp.maximum(x, 0.0) + jnp.log1p(jnp.exp(-jnp.abs(x)))


def _split3(a):
    a1 = a.astype(BF16)
    r1 = a - a1.astype(F32)
    a2 = r1.astype(BF16)
    r2 = r1 - a2.astype(F32)
    return a1, a2, r2.astype(BF16)


def _dot(a, b):
    return jnp.dot(a, b, preferred_element_type=F32)


def _dot_nt(a, b):
    return lax.dot_general(a, b, (((1,), (1,)), ((), ())), preferred_element_type=F32)


def _sel_right(a, sel):
    a1, a2, a3 = _split3(a)
    return (_dot(a1, sel) + _dot(a2, sel)) + _dot(a3, sel)


def _sel_left(sel, a):
    a1, a2, a3 = _split3(a)
    return (_dot(sel, a1) + _dot(sel, a2)) + _dot(sel, a3)


def _row_loop(nrows, fn):
    def body(i, c):
        fn(pl.ds(pl.multiple_of(i * NORM_ROWS, NORM_ROWS), NORM_ROWS))
        return c

    lax.fori_loop(0, nrows // NORM_ROWS, body, 0)


def _rmsnorm_rows(x_ref, w_ref, out_ref, nrows):
    def one(sl):
        x = x_ref[sl, :]
        r = lax.rsqrt(jnp.mean(x * x, axis=-1, keepdims=True) + EPS)
        out_ref[sl, :] = ((x * r) * w_ref[...]).astype(out_ref.dtype)

    _row_loop(nrows, one)


def _tile_branches(n_full, tile, tail_rows, run, axis=0):
    i = pl.program_id(axis)

    @pl.when(i < n_full)
    def _():
        run(False, tile)

    if tail_rows:
        @pl.when(i >= n_full)
        def _():
            run(True, tail_rows)


def _ffn_body(*refs, split_in, split_out, final_norm, emit_norm, has_xn, n_full, tail_rows):
    refs = list(refs)
    x_refs = [refs.pop(0) for _ in range(2 if split_in else 1)]
    xni_ref = refs.pop(0) if has_xn else None
    nw_ref, w1_ref, w3_ref, w2_ref, fn_ref = [refs.pop(0) for _ in range(5)]
    o_refs = [refs.pop(0) for _ in range(2 if split_out else 1)]
    xno_ref = refs.pop(0) if emit_norm else None
    xn_ref, sem = refs
    i = pl.program_id(0)
    j = pl.program_id(1)
    last_j = pl.num_programs(1) - 1
    tm = TOKEN_TILE

    def run(is_tail, rows):
        o_ref = o_refs[-1] if is_tail else o_refs[0]

        @pl.when(j == 0)
        def _():
            if is_tail and split_in:
                src = x_refs[1]
            elif is_tail:
                src = x_refs[0].at[pl.ds(n_full * tm, rows), :]
            else:
                src = x_refs[0].at[pl.ds(pl.multiple_of(i * tm, tm), rows), :]
            copy = pltpu.make_async_copy(src, o_ref.at[pl.ds(0, rows), :], sem)
            copy.start()
            copy.wait()
            if not has_xn:
                _rmsnorm_rows(o_ref, nw_ref, xn_ref, rows)

        xn = (xni_ref if has_xn else xn_ref)[0:rows, :]
        h1 = _dot(xn, w1_ref[...].astype(BF16))
        h3 = _dot(xn, w3_ref[...].astype(BF16))
        g = ((0.5 * _silu(h1)) * h3).astype(BF16)
        o_ref[0:rows, :] += _dot(g, w2_ref[...].astype(BF16))

        if final_norm:
            @pl.when(j == last_j)
            def _():
                _rmsnorm_rows(o_ref, fn_ref, o_ref, rows)

        if emit_norm:
            @pl.when(j == last_j)
            def _():
                _rmsnorm_rows(o_ref, fn_ref, xno_ref, rows)

    _tile_branches(n_full, TOKEN_TILE, tail_rows, run)


def _ffn(xs, norm_w, w1, w3, w2, *, xn=None, final_w=None, next_norm_w=None, split_out=False):
    split_in = len(xs) == 2
    d, f = w1.shape
    tm, tf = TOKEN_TILE, FFN_COLS
    n = sum(x.shape[0] for x in xs)
    n_full, tail_rows = divmod(n, tm)
    n_tiles = n_full + (1 if tail_rows else 0)
    if split_in:
        assert tail_rows == xs[-1].shape[0]
    final_norm = final_w is not None
    emit_norm = next_norm_w is not None
    assert not (final_norm and emit_norm)
    fw = final_w if final_norm else (next_norm_w if emit_norm else norm_w)
    last_full = max(n_full - 1, 0)
    x_specs = [pl.BlockSpec(memory_space=pl.ANY) for _ in xs]
    has_xn = xn is not None
    if has_xn:
        x_specs.append(pl.BlockSpec((tm, d), lambda i, j: (i, 0)))
    if split_out:
        o_specs = [pl.BlockSpec((tm, d), lambda i, j: (jnp.minimum(i, last_full), 0)),
                   pl.BlockSpec((tail_rows, d), lambda i, j: (0, 0), **_SINGLE)]
        o_shapes = [jax.ShapeDtypeStruct((n_full * tm, d), F32), jax.ShapeDtypeStruct((tail_rows, d), F32)]
    else:
        o_specs = [pl.BlockSpec((tm, d), lambda i, j: (i, 0))]
        o_shapes = [jax.ShapeDtypeStruct((n, d), F32)]
    if emit_norm:
        o_specs.append(pl.BlockSpec((tm, d), lambda i, j: (i, 0), **_SINGLE))
        o_shapes.append(jax.ShapeDtypeStruct((n, d), BF16))
    return pl.pallas_call(
        functools.partial(_ffn_body, split_in=split_in, split_out=split_out, final_norm=final_norm,
                          emit_norm=emit_norm, has_xn=has_xn, n_full=n_full, tail_rows=tail_rows),
        grid=(n_tiles, f // tf),
        in_specs=x_specs + [
            pl.BlockSpec((1, d), lambda i, j: (0, 0)),
            pl.BlockSpec((d, tf), lambda i, j: (0, j)),
            pl.BlockSpec((d, tf), lambda i, j: (0, j)),
            pl.BlockSpec((tf, d), lambda i, j: (j, 0)),
            pl.BlockSpec((1, d), lambda i, j: (0, 0)),
        ],
        out_specs=o_specs,
        out_shape=o_shapes,
        scratch_shapes=[pltpu.VMEM((tm, d), BF16), pltpu.SemaphoreType.DMA(())],
        compiler_params=_cparams("arbitrary", "arbitrary"),
        name="ffn",
    )(*xs, *([xn] if has_xn else []), norm_w.reshape(1, d), w1, w3, w2, fw.reshape(1, d))


def _inproj_body(xn_ref, w_ref, wdt_ref, p1_ref, xbc_ref, dt_ref, wb_ref, *,
                 n_full, tail_rows, n_uvz, n_xbc, n_dt):
    j = pl.program_id(0)
    is_xbc = (j >= n_uvz) & (j < n_uvz + n_xbc)

    @pl.when(pl.program_id(1) == 0)
    def _():
        wb_ref[...] = w_ref[...].astype(BF16)

    def run(is_tail, rows):
        @pl.when(j == 0)
        def _():
            dt = _dot_nt(xn_ref[0:rows, :], wdt_ref[...].astype(BF16))
            lane = _iota2(dt.shape, 1)
            dt_ref[0:rows, :] = jnp.where(lane < n_dt, dt, 0.0)

        @pl.when(jnp.logical_not(is_xbc))
        def _():
            p1_ref[0:rows, :] = _dot_nt(xn_ref[0:rows, :], wb_ref[...]).astype(p1_ref.dtype)

        @pl.when(is_xbc)
        def _():
            xbc_ref[0:rows, :] = _dot_nt(xn_ref[0:rows, :], wb_ref[...])

    _tile_branches(n_full, TOKEN_TILE, tail_rows, run, axis=1)


def _inproj(xn, w_t, c_uvz, c_xbc, c_dt):
    n, d = xn.shape
    tm, tn = TOKEN_TILE, PROJ_COLS
    n_full, tail_rows = divmod(n, tm)
    n_tiles = n_full + (1 if tail_rows else 0)
    last_i = n_tiles - 1
    c_gate0 = c_uvz + c_xbc + c_dt
    c_gates = w_t.shape[0] - c_gate0
    n_uvz, n_xbc, n_g = c_uvz // tn, c_xbc // tn, c_gates // tn
    n_main = n_uvz + n_xbc
    assert c_gate0 % SUBLANES == 0 and (c_uvz + c_xbc) % SUBLANES == 0

    def w_row(j, i):
        return (pl.multiple_of(jnp.where(j < n_main, j * tn, c_gate0 + (j - n_main) * tn), SUBLANES), 0)

    def p1_idx(j, i):
        writes = (j < n_uvz) | (j >= n_main)
        col = jnp.where(j < n_uvz, j, jnp.maximum(j - n_xbc, n_uvz - 1))
        return (jnp.where(writes, i, last_i), col)

    def xbc_idx(j, i):
        row = jnp.where(j < n_uvz, 0, jnp.where(j < n_main, i, last_i))
        return (row, jnp.clip(j - n_uvz, 0, n_xbc - 1))

    return pl.pallas_call(
        functools.partial(_inproj_body, n_full=n_full, tail_rows=tail_rows, n_uvz=n_uvz, n_xbc=n_xbc, n_dt=c_dt),
        grid=(n_main + n_g, n_tiles),
        in_specs=[
            pl.BlockSpec((tm, d), lambda j, i: (i, 0)),
            pl.BlockSpec((pl.Element(tn), pl.Element(d)), w_row),
            pl.BlockSpec((pl.Element(LANES), pl.Element(d)), lambda j, i: (c_uvz + c_xbc, 0)),
        ],
        out_specs=[
            pl.BlockSpec((tm, tn), p1_idx),
            pl.BlockSpec((tm, tn), xbc_idx),
            pl.BlockSpec((tm, LANES), lambda j, i: (jnp.where(j == 0, i, last_i), 0)),
        ],
        out_shape=[
            jax.ShapeDtypeStruct((n, c_uvz + c_gates), BF16),
            jax.ShapeDtypeStruct((n, c_xbc), F32),
            jax.ShapeDtypeStruct((n, LANES), F32),
        ],
        scratch_shapes=[pltpu.VMEM((tn, d), BF16)],
        compiler_params=_cparams("arbitrary", "arbitrary"),
        name="in_proj",
    )(xn, w_t, w_t)


def _merge_body(x_ref, yap_ref, ybp_ref, yas_ref, ybs_ref, ga_ref, gb_ref, bg_ref, wa_ref, wb_ref, wo_ref, nw_ref,
                o_ref, xn_ref, *, n_prompt_tiles):
    def run(ya_ref, yb_ref):
        pa = _dot(ya_ref[...], wa_ref[...])
        pb = _dot(yb_ref[...], wb_ref[...])
        ga = jax.nn.sigmoid(ga_ref[...].astype(F32) + bg_ref[0:1, :])
        gb = jax.nn.sigmoid(gb_ref[...].astype(F32) + bg_ref[1:2, :])
        m = (ga * pa + gb * pb).astype(BF16)
        o = x_ref[...] + _dot(m, wo_ref[...])
        o_ref[...] = o
        r = lax.rsqrt(jnp.mean(o * o, axis=-1, keepdims=True) + EPS)
        xn_ref[...] = ((o * r) * nw_ref[...]).astype(xn_ref.dtype)

    i = pl.program_id(0)

    @pl.when(i < n_prompt_tiles)
    def _():
        run(yap_ref, ybp_ref)

    @pl.when(i >= n_prompt_tiles)
    def _():
        run(yas_ref, ybs_ref)


def _merge(x, ya_p, yb_p, ya_s, yb_s, p1, b_gate, wa, wb, wo, next_norm_w):
    n, d = x.shape
    tm = MERGE_TILE
    npt = ya_p.shape[0] // tm
    nst = ya_s.shape[0] // tm
    gcol = p1.shape[1] // d - 2
    p_idx = lambda i: (jnp.minimum(i, npt - 1), 0)
    s_idx = lambda i: (jnp.clip(i - npt, 0, nst - 1), 0)
    return pl.pallas_call(
        functools.partial(_merge_body, n_prompt_tiles=npt),
        grid=(n // tm,),
        in_specs=[
            pl.BlockSpec((tm, d), lambda i: (i, 0)),
            pl.BlockSpec((tm, ya_p.shape[1]), p_idx),
            pl.BlockSpec((tm, yb_p.shape[1]), p_idx),
            pl.BlockSpec((tm, ya_s.shape[1]), s_idx),
            pl.BlockSpec((tm, yb_s.shape[1]), s_idx),
            pl.BlockSpec((tm, d), lambda i: (i, gcol)),
            pl.BlockSpec((tm, d), lambda i: (i, gcol + 1)),
            pl.BlockSpec((2, d), lambda i: (0, 0)),
            pl.BlockSpec(wa.shape, lambda i: (0, 0), **_SINGLE),
            pl.BlockSpec(wb.shape, lambda i: (0, 0), **_SINGLE),
            pl.BlockSpec(wo.shape, lambda i: (0, 0), **_SINGLE),
            pl.BlockSpec((1, d), lambda i: (0, 0)),
        ],
        out_specs=[pl.BlockSpec((tm, d), lambda i: (i, 0)), pl.BlockSpec((tm, d), lambda i: (i, 0))],
        out_shape=[jax.ShapeDtypeStruct((n, d), F32), jax.ShapeDtypeStruct((n, d), BF16)],
        compiler_params=_cparams("arbitrary"),
        name="merge",
    )(x, ya_p, yb_p, ya_s, yb_s, p1, p1, b_gate, wa, wb, wo, next_norm_w.reshape(1, d))


def _iota2(shape, dim):
    return lax.broadcasted_iota(jnp.int32, shape, dim)


def _block_masks(seq_len):
    tb = TOKEN_BLOCK
    row = _iota2((tb, tb), 0)
    col = _iota2((tb, tb), 1)
    if seq_len >= tb:
        causal = col <= row
        last = col == tb - 1
    else:
        causal = (col <= row) & ((row // seq_len) == (col // seq_len))
        last = col == (row // seq_len) * seq_len + (seq_len - 1)
    return causal, last


def _init_constants(wm_ref, e_ref, gw_ref, causal):
    for h in range(GM_HEADS):
        wm_ref[h] = jnp.where(causal, gw_ref[h], 0.0).astype(BF16)
    hrow = _iota2(e_ref.shape, 0)
    ccol = _iota2(e_ref.shape, 1)
    e_ref[...] = jnp.where(hrow == ccol // SSD_HEAD_DIM, 1.0, 0.0).astype(BF16)


def _gating(u_ref, v_ref, gvn_ref, bs_ref, wm_ref, ya_ref, vn_ref):
    vg = _gelu(v_ref[...].astype(F32))
    r = lax.rsqrt(jnp.mean(vg * vg, axis=-1, keepdims=True) + EPS)
    vn = (vg * r) * gvn_ref[...]
    if vn_ref is not None:
        vn_ref[...] = vn
    vnb = vn.astype(BF16)
    for h in range(GM_HEADS):
        cs = slice(h * LANES, (h + 1) * LANES)
        s = _dot(wm_ref[h], vnb[:, cs]) + bs_ref[:, h:h + 1]
        ya_ref[:, cs] = (_gelu(u_ref[:, cs].astype(F32)) * s).astype(ya_ref.dtype)


def _conv(xp_ref, cw_ref, cb_ref, xc_ref, halo_refs, seq_len):
    tb = TOKEN_BLOCK
    width = xc_ref.shape[1]
    cw = 512
    tpos = _iota2((tb, 1), 0) % seq_len if seq_len < tb else None
    for c0 in range(0, width, cw):
        cs = slice(c0, c0 + cw)
        acc = None
        for k in range(CONV_W):
            d = CONV_W - 1 - k
            xs = xp_ref[SUBLANES - d:SUBLANES - d + tb, cs]
            if tpos is not None and d > 0:
                xs = jnp.where(tpos >= d, xs, 0.0) + halo_refs[d - 1][:, cs]
            term = cw_ref[k:k + 1, cs] * xs
            acc = term if acc is None else acc + term
        xc_ref[:, cs] = _silu(cb_ref[:, cs] + acc)


def _ssd_prepare(dt_ref, dtb_ref, alog_ref, e_ref, causal, last, dtf_ref, tef_ref, eaf_ref):
    dt = _softplus(dt_ref[...] + dtb_ref[...])
    a = -jnp.exp(alog_ref[...])
    da = dt * a
    lmat = jnp.where(causal, 1.0, 0.0).astype(BF16)
    acum = _sel_left(lmat, da)
    a_last = _sel_left(jnp.where(last, 1.0, 0.0).astype(BF16), acum)
    ea = jnp.exp(acum)
    e = e_ref[...]
    dtf_ref[...] = _sel_right(dt, e)
    tef_ref[...] = _sel_right(dt * jnp.exp(a_last - acum), e)
    eaf_ref[...] = _sel_right(ea, e)
    return acum, acum.T


def _ssd_diag_group(g, xc_ref, dtf_ref, acum, acum_t, causal):
    tb = TOKEN_BLOCK
    width = xc_ref.shape[1] - 2 * SSD_GROUPS * D_STATE
    gw = width // SSD_GROUPS
    bg = xc_ref[:, width + g * D_STATE:width + (g + 1) * D_STATE]
    cg = xc_ref[:, width + (SSD_GROUPS + g) * D_STATE:width + (SSD_GROUPS + g + 1) * D_STATE].astype(BF16)
    cb = _dot_nt(cg, bg.astype(BF16))
    gs = slice(g * gw, (g + 1) * gw)
    xdt = (xc_ref[:, gs] * dtf_ref[:, gs]).astype(BF16)
    lane = _iota2((tb, LANES), 1)
    heads_per_group = gw // SSD_HEAD_DIM
    outs = []
    for jp in range(heads_per_group // 2):
        ms = []
        for hh in range(2):
            h = g * heads_per_group + 2 * jp + hh
            seg = acum[:, h:h + 1] - acum_t[h:h + 1, :]
            dec = jnp.exp(jnp.where(causal, seg, -jnp.inf))
            ms.append((cb * dec).astype(BF16))
        lhs = jnp.concatenate(ms, axis=1)
        xpair = xdt[:, jp * LANES:(jp + 1) * LANES]
        zero = jnp.zeros_like(xpair)
        rhs = jnp.concatenate([jnp.where(lane < SSD_HEAD_DIM, xpair, zero),
                               jnp.where(lane >= SSD_HEAD_DIM, xpair, zero)], axis=0)
        outs.append(_dot(lhs, rhs))
    return jnp.concatenate(outs, axis=1), cg, bg


def _finish_group(g, y, xc_ref, z_ref, dsk_ref, sn_ref, yb_ref):
    gw = yb_ref.shape[1] // SSD_GROUPS
    gs = slice(g * gw, (g + 1) * gw)
    y = y + dsk_ref[:, gs] * xc_ref[:, gs]
    yz = y * _silu(z_ref[:, gs].astype(F32))
    r = lax.rsqrt(jnp.mean(yz * yz, axis=-1, keepdims=True) + EPS)
    yb_ref[:, gs] = ((yz * r) * sn_ref[:, gs]).astype(yb_ref.dtype)


def _prompt_body(u_ref, v_ref, z_ref, xbc_ref, dt_ref, gvn_ref, gw_ref, bs_ref, cw_ref, cb_ref,
                 dtb_ref, alog_ref, dsk_ref, sn_ref,
                 ya_ref, yb_ref, hs_ref, ct_ref,
                 wm_ref, e_ref, xp_ref, xc_ref, dtf_ref, tef_ref, eaf_ref, st_ref):
    tb = TOKEN_BLOCK
    c = pl.program_id(1)
    causal, last = _block_masks(tb)

    @pl.when((pl.program_id(0) == 0) & (c == 0))
    def _():
        _init_constants(wm_ref, e_ref, gw_ref, causal)

    @pl.when(c == 0)
    def _():
        st_ref[...] = jnp.zeros_like(st_ref)
        xp_ref[0:SUBLANES, :] = jnp.zeros((SUBLANES, xp_ref.shape[1]), F32)

    _gating(u_ref, v_ref, gvn_ref, bs_ref, wm_ref, ya_ref, None)

    xp_ref[SUBLANES:SUBLANES + tb, :] = xbc_ref[...]
    _conv(xp_ref, cw_ref, cb_ref, xc_ref, None, tb)
    xp_ref[0:SUBLANES, :] = xbc_ref[tb - SUBLANES:tb, :]

    acum, acum_t = _ssd_prepare(dt_ref, dtb_ref, alog_ref, e_ref, causal, last, dtf_ref, tef_ref, eaf_ref)

    gw = yb_ref.shape[1] // SSD_GROUPS
    for g in range(SSD_GROUPS):
        gs = slice(g * gw, (g + 1) * gw)
        yd, cg, bg = _ssd_diag_group(g, xc_ref, dtf_ref, acum, acum_t, causal)
        st = st_ref[:, gs]
        y = yd + _dot(cg, st.astype(BF16)) * eaf_ref[:, gs]
        _finish_group(g, y, xc_ref, z_ref, dsk_ref, sn_ref, yb_ref)
        xw = (xc_ref[:, gs] * tef_ref[:, gs]).astype(BF16)
        st_ref[:, gs] = eaf_ref[tb - 1:tb, gs] * st + _dot(bg.T.astype(BF16), xw)

    @pl.when(c == pl.num_programs(1) - 1)
    def _():
        for k in range(st_ref.shape[1] // LANES):
            hs_ref[k * LANES:(k + 1) * LANES, :] = st_ref[:, k * LANES:(k + 1) * LANES].T
        ct_ref[...] = xbc_ref[tb - SUBLANES:tb, :]


def _sample_body(u_ref, v_ref, z_ref, xbc_ref, dt_ref, h1_ref, h2_ref, h3_ref, h0_ref,
                 gvn_ref, gw_ref, bs_ref, cw_ref, cb_ref, dtb_ref, alog_ref, dsk_ref, sn_ref,
                 ya_ref, yb_ref, vn_ref, hs_ref,
                 wm_ref, e_ref, xp_ref, xc_ref, dtf_ref, tef_ref, eaf_ref,
                 y_ref, yoff_ref, cg_ref, bg_ref, xwt_ref, eat_ref, cdh_ref, *, seq_len):
    tb = TOKEN_BLOCK
    k = pl.program_id(1)
    causal, last = _block_masks(seq_len)
    gw = yb_ref.shape[1] // SSD_GROUPS

    @pl.when((pl.program_id(0) == 0) & (k == 0))
    def _():
        _init_constants(wm_ref, e_ref, gw_ref, causal)
        xp_ref[0:SUBLANES, :] = jnp.zeros((SUBLANES, xp_ref.shape[1]), F32)

    @pl.when(k == 0)
    def _():
        _gating(u_ref, v_ref, gvn_ref, bs_ref, wm_ref, ya_ref, vn_ref)
        xp_ref[SUBLANES:SUBLANES + tb, :] = xbc_ref[...]
        _conv(xp_ref, cw_ref, cb_ref, xc_ref, (h1_ref, h2_ref, h3_ref), seq_len)
        acum, acum_t = _ssd_prepare(dt_ref, dtb_ref, alog_ref, e_ref, causal, last, dtf_ref, tef_ref, eaf_ref)
        eat_ref[...] = jnp.exp(acum_t)
        for g in range(SSD_GROUPS):
            gs = slice(g * gw, (g + 1) * gw)
            yd, cg, bg = _ssd_diag_group(g, xc_ref, dtf_ref, acum, acum_t, causal)
            y_ref[:, gs] = yd
            cg_ref[:, g * D_STATE:(g + 1) * D_STATE] = cg.astype(F32)
            bg_ref[:, g * D_STATE:(g + 1) * D_STATE] = bg
            xw = xc_ref[:, gs] * tef_ref[:, gs]
            for q in range(gw // LANES):
                r0 = g * gw + q * LANES
                xwt_ref[r0:r0 + LANES, :] = xw[:, q * LANES:(q + 1) * LANES].T.astype(BF16)

    nseq = h0_ref.shape[0]
    nt = nseq * seq_len
    rows = pl.ds(pl.multiple_of(k * nt, nt), nt)
    rowseq = _iota2((tb, 1), 0) // seq_len
    subseq = _iota2((nt, 1), 0) // seq_len
    tok = _iota2((tb, LANES), 0)
    onehot = jnp.concatenate(
        [jnp.where(tok == (k * nseq + b) * seq_len + (seq_len - 1), 1.0, 0.0) for b in range(nseq)],
        axis=1).astype(BF16)
    cdh_ref[...] = _sel_right(eat_ref[...], onehot)
    hpg = gw // SSD_HEAD_DIM
    for g in range(SSD_GROUPS):
        gs = slice(g * gw, (g + 1) * gw)
        ds = slice(g * D_STATE, (g + 1) * D_STATE)
        c_sub = cg_ref[rows, ds]
        lhs = jnp.concatenate([jnp.where(subseq == b, c_sub, 0.0) for b in range(nseq)], axis=1).astype(BF16)
        h0cat = jnp.concatenate([h0_ref[b, gs, :].astype(BF16) for b in range(nseq)], axis=1)
        yoff_ref[rows, gs] = _dot_nt(lhs, h0cat)
        b_all = bg_ref[:, ds]
        bm = jnp.concatenate([jnp.where(rowseq == k * nseq + b, b_all, 0.0) for b in range(nseq)],
                             axis=1).astype(BF16)
        s_new = _dot(xwt_ref[gs, :], bm)
        for b in range(nseq):
            ls = slice(b * D_STATE, (b + 1) * D_STATE)
            for j in range(hpg):
                h = g * hpg + j
                r0 = g * gw + j * SSD_HEAD_DIM
                hs_ref[b, r0:r0 + SSD_HEAD_DIM, :] = (cdh_ref[h:h + 1, ls] * h0_ref[b, r0:r0 + SSD_HEAD_DIM, :]
                                                      + s_new[j * SSD_HEAD_DIM:(j + 1) * SSD_HEAD_DIM, ls])

    @pl.when(k == pl.num_programs(1) - 1)
    def _():
        for g in range(SSD_GROUPS):
            gs = slice(g * gw, (g + 1) * gw)
            y = y_ref[:, gs] + yoff_ref[:, gs] * eaf_ref[:, gs]
            _finish_group(g, y, xc_ref, z_ref, dsk_ref, sn_ref, yb_ref)


def _mixer_params(lp):
    (gm_v_norm, gm_ws, gm_bs, conv_w, conv_b, dt_bias, a_log, d_skip, ssd_norm) = lp
    nh = dt_bias.shape[0]
    pad = LANES - nh
    return dict(
        gvn=gm_v_norm.reshape(1, -1),
        cw=conv_w,
        cb=conv_b.reshape(1, -1),
        dtb=jnp.pad(dt_bias, (0, pad)).reshape(1, LANES),
        alog=jnp.pad(a_log, (0, pad)).reshape(1, LANES),
        dsk=jnp.repeat(d_skip, SSD_HEAD_DIM).reshape(1, -1),
        sn=ssd_norm.reshape(1, -1),
    )


def _full_spec(a):
    nd = a.ndim
    return pl.BlockSpec(a.shape, lambda i, j: (0,) * nd)


def _mixer_prompt(p1, xbc, dt, lp, n_seq, seq_len, gm_w, ssd_w, conv_dim):
    tb = TOKEN_BLOCK
    nc = seq_len // tb
    mp = _mixer_params(lp)
    gw_full = lp[1]
    bs_t = lp[2].T
    row = lambda b, c: b * nc + c
    params = [mp["gvn"], gw_full, bs_t, mp["cw"], mp["cb"], mp["dtb"], mp["alog"], mp["dsk"], mp["sn"]]
    return pl.pallas_call(
        _prompt_body,
        grid=(n_seq, nc),
        in_specs=[
            pl.BlockSpec((tb, gm_w), lambda b, c: (row(b, c), 0)),
            pl.BlockSpec((tb, gm_w), lambda b, c: (row(b, c), 1)),
            pl.BlockSpec((tb, ssd_w), lambda b, c: (row(b, c), 1)),
            pl.BlockSpec((tb, conv_dim), lambda b, c: (row(b, c), 0)),
            pl.BlockSpec((tb, LANES), lambda b, c: (row(b, c), 0)),
        ] + [_full_spec(a) for a in params],
        out_specs=[
            pl.BlockSpec((tb, gm_w), lambda b, c: (row(b, c), 0)),
            pl.BlockSpec((tb, ssd_w), lambda b, c: (row(b, c), 0)),
            pl.BlockSpec((None, ssd_w, D_STATE), lambda b, c: (b, 0, 0)),
            pl.BlockSpec((None, SUBLANES, conv_dim), lambda b, c: (b, 0, 0)),
        ],
        out_shape=[
            jax.ShapeDtypeStruct((n_seq * seq_len, gm_w), BF16),
            jax.ShapeDtypeStruct((n_seq * seq_len, ssd_w), BF16),
            jax.ShapeDtypeStruct((n_seq, ssd_w, D_STATE), F32),
            jax.ShapeDtypeStruct((n_seq, SUBLANES, conv_dim), F32),
        ],
        scratch_shapes=[
            pltpu.VMEM((GM_HEADS, tb, tb), BF16),
            pltpu.VMEM((LANES, ssd_w), BF16),
            pltpu.VMEM((SUBLANES + tb, conv_dim), F32),
            pltpu.VMEM((tb, conv_dim), F32),
            pltpu.VMEM((tb, ssd_w), F32),
            pltpu.VMEM((tb, ssd_w), F32),
            pltpu.VMEM((tb, ssd_w), F32),
            pltpu.VMEM((D_STATE, ssd_w), F32),
        ],
        compiler_params=_cparams("arbitrary", "arbitrary"),
        name="mixer_prompt",
    )(p1, p1, p1, xbc, dt, *params)


def _mixer_sample(p1, xbc, dt, lp, h0, halos, row0, n_seq, seq_len, gm_w, ssd_w, conv_dim):
    tb = TOKEN_BLOCK
    seqs_per_block = tb // seq_len
    nblk = n_seq // seqs_per_block
    nsub = seqs_per_block // SAMPLE_SEQS_PER_STEP
    blk0 = row0 // tb
    mp = _mixer_params(lp)
    reps = tb // seq_len
    gw_tiled = jnp.tile(lp[1][:, :seq_len, :seq_len], (1, reps, reps))
    bs_t = jnp.tile(lp[2][:, :seq_len], (1, reps)).T
    params = [mp["gvn"], gw_tiled, bs_t, mp["cw"], mp["cb"], mp["dtb"], mp["alog"], mp["dsk"], mp["sn"]]
    ntok = n_seq * seq_len
    return pl.pallas_call(
        functools.partial(_sample_body, seq_len=seq_len),
        grid=(nblk, nsub),
        in_specs=[
            pl.BlockSpec((tb, gm_w), lambda i, k: (blk0 + i, 0)),
            pl.BlockSpec((tb, gm_w), lambda i, k: (blk0 + i, 1)),
            pl.BlockSpec((tb, ssd_w), lambda i, k: (blk0 + i, 1)),
            pl.BlockSpec((tb, conv_dim), lambda i, k: (blk0 + i, 0)),
            pl.BlockSpec((tb, LANES), lambda i, k: (blk0 + i, 0)),
            pl.BlockSpec((tb, conv_dim), lambda i, k: (i, 0)),
            pl.BlockSpec((tb, conv_dim), lambda i, k: (i, 0)),
            pl.BlockSpec((tb, conv_dim), lambda i, k: (i, 0)),
            pl.BlockSpec((SAMPLE_SEQS_PER_STEP, ssd_w, D_STATE), lambda i, k: (i * nsub + k, 0, 0)),
        ] + [_full_spec(a) for a in params],
        out_specs=[
            pl.BlockSpec((tb, gm_w), lambda i, k: (i, 0)),
            pl.BlockSpec((tb, ssd_w), lambda i, k: (i, 0)),
            pl.BlockSpec((tb, gm_w), lambda i, k: (i, 0)),
            pl.BlockSpec((SAMPLE_SEQS_PER_STEP, ssd_w, D_STATE), lambda i, k: (i * nsub + k, 0, 0)),
        ],
        out_shape=[
            jax.ShapeDtypeStruct((ntok, gm_w), BF16),
            jax.ShapeDtypeStruct((ntok, ssd_w), BF16),
            jax.ShapeDtypeStruct((ntok, gm_w), F32),
            jax.ShapeDtypeStruct((n_seq, ssd_w, D_STATE), F32),
        ],
        scratch_shapes=[
            pltpu.VMEM((GM_HEADS, tb, tb), BF16),
            pltpu.VMEM((LANES, ssd_w), BF16),
            pltpu.VMEM((SUBLANES + tb, conv_dim), F32),
            pltpu.VMEM((tb, conv_dim), F32),
            pltpu.VMEM((tb, ssd_w), F32),
            pltpu.VMEM((tb, ssd_w), F32),
            pltpu.VMEM((tb, ssd_w), F32),
            pltpu.VMEM((tb, ssd_w), F32),
            pltpu.VMEM((tb, ssd_w), F32),
            pltpu.VMEM((tb, SSD_GROUPS * D_STATE), F32),
            pltpu.VMEM((tb, SSD_GROUPS * D_STATE), F32),
            pltpu.VMEM((ssd_w, tb), BF16),
            pltpu.VMEM((LANES, tb), F32),
            pltpu.VMEM((LANES, SAMPLE_SEQS_PER_STEP * LANES), F32),
        ],
        compiler_params=_cparams("arbitrary", "arbitrary"),
        name="mixer_sample",
    )(p1, p1, p1, xbc, dt, *halos, h0, *params)


def kernel(x_prompt, x_sample, state_ssm, state_conv, ffn1_norm, ffn1_w1, ffn1_w3, ffn1_w2, mix_norm, w_in,
           b_gate, conv_w, conv_b, dt_bias, a_log, d_skip, ssd_norm, gm_v_norm, gm_ws, gm_bs, w_proj_a,
           w_proj_b, w_out, ffn2_norm, ffn2_w1, ffn2_w3, ffn2_w2, final_norm):
    bp, tp, d = x_prompt.shape
    bs_, ts, _ = x_sample.shape
    depth = w_in.shape[0]
    n_heads = dt_bias.shape[1]
    gm_w = gm_v_norm.shape[1]
    ssd_w = ssd_norm.shape[1]
    conv_dim = conv_w.shape[2]
    np_, ns = bp * tp, bs_ * ts

    xs = [x_prompt.reshape(np_, d), x_sample.reshape(ns, d)]
    ssm_p, conv_p, ssm_s, conv_s, v_s = [], [], [], [], []
    for l in range(depth):
        c_uvz = 2 * gm_w + ssd_w
        x, xn = _ffn(xs, ffn1_norm[l], ffn1_w1[l], ffn1_w3[l], ffn1_w2[l], next_norm_w=mix_norm[l])
        p1, xbc, dt = _inproj(xn, w_in[l].T, c_uvz, conv_dim, n_heads)

        lp = (gm_v_norm[l], gm_ws[l], gm_bs[l], conv_w[l], conv_b[l], dt_bias[l], a_log[l], d_skip[l], ssd_norm[l])
        ya_p, yb_p, hs_p, ct_p = _mixer_prompt(p1, xbc, dt, lp, bp, tp, gm_w, ssd_w, conv_dim)

        st = state_conv[l]
        halos = []
        for dd in range(1, CONV_W):
            hz = jnp.concatenate([st[:, CONV_W - 1 - dd:, :], jnp.zeros((bs_, ts - dd, conv_dim), F32)], axis=1)
            halos.append(hz.reshape(ns, conv_dim))
        ya_s, yb_s, vn_s, hs_s = _mixer_sample(p1, xbc, dt, lp, state_ssm[l].reshape(bs_, ssd_w, D_STATE), halos,
                                               np_, bs_, ts, gm_w, ssd_w, conv_dim)

        x, xn = _merge(x, ya_p, yb_p, ya_s, yb_s, p1, b_gate[l], w_proj_a[l].astype(BF16), w_proj_b[l].astype(BF16),
                       w_out[l].astype(BF16), ffn2_norm[l])
        last = l == depth - 1
        xs = _ffn([x], ffn2_norm[l], ffn2_w1[l], ffn2_w3[l], ffn2_w2[l], xn=xn, final_w=final_norm if last else None,
                  split_out=True)

        xbc_s = xbc[np_:].reshape(bs_, ts, conv_dim)
        ssm_p.append(hs_p.reshape(bp, n_heads, SSD_HEAD_DIM, D_STATE))
        conv_p.append(ct_p[:, SUBLANES - (CONV_W - 1):, :])
        ssm_s.append(hs_s.reshape(bs_, n_heads, SSD_HEAD_DIM, D_STATE))
        conv_s.append(jnp.concatenate([st, xbc_s], axis=1)[:, ts:, :])
        v_s.append(vn_s.reshape(bs_, ts, gm_w))

    return (xs[0].reshape(bp, tp, d), xs[1].reshape(bs_, ts, d), jnp.stack(ssm_p), jnp.stack(conv_p),
            jnp.stack(ssm_s), jnp.stack(conv_s), jnp.stack(v_s))
```

```python
import functools

import jax
import jax.numpy as jnp
from jax import lax
from jax.experimental import pallas as pl
from jax.experimental.pallas import tpu as pltpu

F32 = jnp.float32
BF16 = jnp.bfloat16
EPS = 1e-6

LANES = 128
SUBLANES = 8
VMEM_LIMIT_BYTES = 60 * 1024 * 1024

GM_HEADS = 8
SSD_HEAD_DIM = 64
SSD_GROUPS = 4
D_STATE = 128
CONV_W = 4
TOKEN_BLOCK = 128
SAMPLE_SEQS_PER_STEP = 4

TOKEN_TILE = 1024
FFN_COLS = 512
PROJ_COLS = 1024
MERGE_TILE = 256
NORM_ROWS = 64

_SINGLE = dict(pipeline_mode=pl.Buffered(1))


def _cparams(*sem):
    return pltpu.CompilerParams(dimension_semantics=sem, vmem_limit_bytes=VMEM_LIMIT_BYTES)


_GELU_C = 0.7978845608028654


def _gelu(x):
    t = jnp.tanh(x * (_GELU_C + (_GELU_C * 0.044715) * (x * x)))
    return x * (0.5 + 0.5 * t)


def _silu(x):
    return x * jax.nn.sigmoid(x)


def _softplus(x):
    return jnp.maximum(x, 0.0) + jnp.log1p(jnp.exp(-jnp.abs(x)))


def _split3(a):
    a1 = a.astype(BF16)
    r1 = a - a1.astype(F32)
    a2 = r1.astype(BF16)
    r2 = r1 - a2.astype(F32)
    return a1, a2, r2.astype(BF16)


def _dot(a, b):
    return jnp.dot(a, b, preferred_element_type=F32)


def _dot_nt(a, b):
    return lax.dot_general(a, b, (((1,), (1,)), ((), ())), preferred_element_type=F32)


def _sel_right(a, sel):
    a1, a2, a3 = _split3(a)
    return (_dot(a1, sel) + _dot(a2, sel)) + _dot(a3, sel)


def _sel_left(sel, a):
    a1, a2, a3 = _split3(a)
    return (_dot(sel, a1) + _dot(sel, a2)) + _dot(sel, a3)


def _row_loop(nrows, fn):
    def body(i, c):
        fn(pl.ds(pl.multiple_of(i * NORM_ROWS, NORM_ROWS), NORM_ROWS))
        return c

    lax.fori_loop(0, nrows // NORM_ROWS, body, 0)


def _rmsnorm_rows(x_ref, w_ref, out_ref, nrows):
    def one(sl):
        x = x_ref[sl, :]
        r = lax.rsqrt(jnp.mean(x * x, axis=-1, keepdims=True) + EPS)
        out_ref[sl, :] = ((x * r) * w_ref[...]).astype(out_ref.dtype)

    _row_loop(nrows, one)


def _tile_branches(n_full, tile, tail_rows, run, axis=0):
    i = pl.program_id(axis)

    @pl.when(i < n_full)
    def _():
        run(False, tile)

    if tail_rows:
        @pl.when(i >= n_full)
        def _():
            run(True, tail_rows)


def _ffn_body(*refs, split_in, split_out, final_norm, emit_norm, has_xn, n_full, tail_rows):
    refs = list(refs)
    x_refs = [refs.pop(0) for _ in range(2 if split_in else 1)]
    xni_ref = refs.pop(0) if has_xn else None
    nw_ref, w1_ref, w3_ref, w2_ref, fn_ref = [refs.pop(0) for _ in range(5)]
    o_refs = [refs.pop(0) for _ in range(2 if split_out else 1)]
    xno_ref = refs.pop(0) if emit_norm else None
    xn_ref, sem = refs
    i = pl.program_id(0)
    j = pl.program_id(1)
    last_j = pl.num_programs(1) - 1
    tm = TOKEN_TILE

    def run(is_tail, rows):
        o_ref = o_refs[-1] if is_tail else o_refs[0]

        @pl.when(j == 0)
        def _():
            if is_tail and split_in:
                src = x_refs[1]
            elif is_tail:
                src = x_refs[0].at[pl.ds(n_full * tm, rows), :]
            else:
                src = x_refs[0].at[pl.ds(pl.multiple_of(i * tm, tm), rows), :]
            copy = pltpu.make_async_copy(src, o_ref.at[pl.ds(0, rows), :], sem)
            copy.start()
            copy.wait()
            if not has_xn:
                _rmsnorm_rows(o_ref, nw_ref, xn_ref, rows)

        xn = (xni_ref if has_xn else xn_ref)[0:rows, :]
        h1 = _dot(xn, w1_ref[...].astype(BF16))
        h3 = _dot(xn, w3_ref[...].astype(BF16))
        g = ((0.5 * _silu(h1)) * h3).astype(BF16)
        o_ref[0:rows, :] += _dot(g, w2_ref[...].astype(BF16))

        if final_norm:
            @pl.when(j == last_j)
            def _():
                _rmsnorm_rows(o_ref, fn_ref, o_ref, rows)

        if emit_norm:
            @pl.when(j == last_j)
            def _():
                _rmsnorm_rows(o_ref, fn_ref, xno_ref, rows)

    _tile_branches(n_full, TOKEN_TILE, tail_rows, run)


def _ffn(xs, norm_w, w1, w3, w2, *, xn=None, final_w=None, next_norm_w=None, split_out=False):
    split_in = len(xs) == 2
    d, f = w1.shape
    tm, tf = TOKEN_TILE, FFN_COLS
    n = sum(x.shape[0] for x in xs)
    n_full, tail_rows = divmod(n, tm)
    n_tiles = n_full + (1 if tail_rows else 0)
    if split_in:
        assert tail_rows == xs[-1].shape[0]
    final_norm = final_w is not None
    emit_norm = next_norm_w is not None
    assert not (final_norm and emit_norm)
    fw = final_w if final_norm else (next_norm_w if emit_norm else norm_w)
    last_full = max(n_full - 1, 0)
    x_specs = [pl.BlockSpec(memory_space=pl.ANY) for _ in xs]
    has_xn = xn is not None
    if has_xn:
        x_specs.append(pl.BlockSpec((tm, d), lambda i, j: (i, 0)))
    if split_out:
        o_specs = [pl.BlockSpec((tm, d), lambda i, j: (jnp.minimum(i, last_full), 0)),
                   pl.BlockSpec((tail_rows, d), lambda i, j: (0, 0), **_SINGLE)]
        o_shapes = [jax.ShapeDtypeStruct((n_full * tm, d), F32), jax.ShapeDtypeStruct((tail_rows, d), F32)]
    else:
        o_specs = [pl.BlockSpec((tm, d), lambda i, j: (i, 0))]
        o_shapes = [jax.ShapeDtypeStruct((n, d), F32)]
    if emit_norm:
        o_specs.append(pl.BlockSpec((tm, d), lambda i, j: (i, 0), **_SINGLE))
        o_shapes.append(jax.ShapeDtypeStruct((n, d), BF16))
    return pl.pallas_call(
        functools.partial(_ffn_body, split_in=split_in, split_out=split_out, final_norm=final_norm,
                          emit_norm=emit_norm, has_xn=has_xn, n_full=n_full, tail_rows=tail_rows),
        grid=(n_tiles, f // tf),
        in_specs=x_specs + [
            pl.BlockSpec((1, d), lambda i, j: (0, 0)),
            pl.BlockSpec((d, tf), lambda i, j: (0, j)),
            pl.BlockSpec((d, tf), lambda i, j: (0, j)),
            pl.BlockSpec((tf, d), lambda i, j: (j, 0)),
            pl.BlockSpec((1, d), lambda i, j: (0, 0)),
        ],
        out_specs=o_specs,
        out_shape=o_shapes,
        scratch_shapes=[pltpu.VMEM((tm, d), BF16), pltpu.SemaphoreType.DMA(())],
        compiler_params=_cparams("arbitrary", "arbitrary"),
        name="ffn",
    )(*xs, *([xn] if has_xn else []), norm_w.reshape(1, d), w1, w3, w2, fw.reshape(1, d))


def _inproj_body(xn_ref, w_ref, wdt_ref, p1_ref, xbc_ref, dt_ref, wb_ref, *,
                 n_full, tail_rows, n_uvz, n_xbc, n_dt):
    j = pl.program_id(0)
    is_xbc = (j >= n_uvz) & (j < n_uvz + n_xbc)

    @pl.when(pl.program_id(1) == 0)
    def _():
        wb_ref[...] = w_ref[...].astype(BF16)

    def run(is_tail, rows):
        @pl.when(j == 0)
        def _():
            dt = _dot_nt(xn_ref[0:rows, :], wdt_ref[...].astype(BF16))
            lane = _iota2(dt.shape, 1)
            dt_ref[0:rows, :] = jnp.where(lane < n_dt, dt, 0.0)

        @pl.when(jnp.logical_not(is_xbc))
        def _():
            p1_ref[0:rows, :] = _dot_nt(xn_ref[0:rows, :], wb_ref[...]).astype(p1_ref.dtype)

        @pl.when(is_xbc)
        def _():
            xbc_ref[0:rows, :] = _dot_nt(xn_ref[0:rows, :], wb_ref[...])

    _tile_branches(n_full, TOKEN_TILE, tail_rows, run, axis=1)


def _inproj(xn, w_t, c_uvz, c_xbc, c_dt):
    n, d = xn.shape
    tm, tn = TOKEN_TILE, PROJ_COLS
    n_full, tail_rows = divmod(n, tm)
    n_tiles = n_full + (1 if tail_rows else 0)
    last_i = n_tiles - 1
    c_gate0 = c_uvz + c_xbc + c_dt
    c_gates = w_t.shape[0] - c_gate0
    n_uvz, n_xbc, n_g = c_uvz // tn, c_xbc // tn, c_gates // tn
    n_main = n_uvz + n_xbc
    assert c_gate0 % SUBLANES == 0 and (c_uvz + c_xbc) % SUBLANES == 0

    def w_row(j, i):
        return (pl.multiple_of(jnp.where(j < n_main, j * tn, c_gate0 + (j - n_main) * tn), SUBLANES), 0)

    def p1_idx(j, i):
        writes = (j < n_uvz) | (j >= n_main)
        col = jnp.where(j < n_uvz, j, jnp.maximum(j - n_xbc, n_uvz - 1))
        return (jnp.where(writes, i, last_i), col)

    def xbc_idx(j, i):
        row = jnp.where(j < n_uvz, 0, jnp.where(j < n_main, i, last_i))
        return (row, jnp.clip(j - n_uvz, 0, n_xbc - 1))

    return pl.pallas_call(
        functools.partial(_inproj_body, n_full=n_full, tail_rows=tail_rows, n_uvz=n_uvz, n_xbc=n_xbc, n_dt=c_dt),
        grid=(n_main + n_g, n_tiles),
        in_specs=[
            pl.BlockSpec((tm, d), lambda j, i: (i, 0)),
            pl.BlockSpec((pl.Element(tn), pl.Element(d)), w_row),
            pl.BlockSpec((pl.Element(LANES), pl.Element(d)), lambda j, i: (c_uvz + c_xbc, 0)),
        ],
        out_specs=[
            pl.BlockSpec((tm, tn), p1_idx),
            pl.BlockSpec((tm, tn), xbc_idx),
            pl.BlockSpec((tm, LANES), lambda j, i: (jnp.where(j == 0, i, last_i), 0)),
        ],
        out_shape=[
            jax.ShapeDtypeStruct((n, c_uvz + c_gates), BF16),
            jax.ShapeDtypeStruct((n, c_xbc), F32),
            jax.ShapeDtypeStruct((n, LANES), F32),
        ],
        scratch_shapes=[pltpu.VMEM((tn, d), BF16)],
        compiler_params=_cparams("arbitrary", "arbitrary"),
        name="in_proj",
    )(xn, w_t, w_t)


def _merge_body(x_ref, yap_ref, ybp_ref, yas_ref, ybs_ref, ga_ref, gb_ref, bg_ref, wa_ref, wb_ref, wo_ref, nw_ref,
                o_ref, xn_ref, *, n_prompt_tiles):
    def run(ya_ref, yb_ref):
        pa = _dot(ya_ref[...], wa_ref[...])
        pb = _dot(yb_ref[...], wb_ref[...])
        ga = jax.nn.sigmoid(ga_ref[...].astype(F32) + bg_ref[0:1, :])
        gb = jax.nn.sigmoid(gb_ref[...].astype(F32) + bg_ref[1:2, :])
        m = (ga * pa + gb * pb).astype(BF16)
        o = x_ref[...] + _dot(m, wo_ref[...])
        o_ref[...] = o
        r = lax.rsqrt(jnp.mean(o * o, axis=-1, keepdims=True) + EPS)
        xn_ref[...] = ((o * r) * nw_ref[...]).astype(xn_ref.dtype)

    i = pl.program_id(0)

    @pl.when(i < n_prompt_tiles)
    def _():
        run(yap_ref, ybp_ref)

    @pl.when(i >= n_prompt_tiles)
    def _():
        run(yas_ref, ybs_ref)


def _merge(x, ya_p, yb_p, ya_s, yb_s, p1, b_gate, wa, wb, wo, next_norm_w):
    n, d = x.shape
    tm = MERGE_TILE
    npt = ya_p.shape[0] // tm
    nst = ya_s.shape[0] // tm
    gcol = p1.shape[1] // d - 2
    p_idx = lambda i: (jnp.minimum(i, npt - 1), 0)
    s_idx = lambda i: (jnp.clip(i - npt, 0, nst - 1), 0)
    return pl.pallas_call(
        functools.partial(_merge_body, n_prompt_tiles=npt),
        grid=(n // tm,),
        in_specs=[
            pl.BlockSpec((tm, d), lambda i: (i, 0)),
            pl.BlockSpec((tm, ya_p.shape[1]), p_idx),
            pl.BlockSpec((tm, yb_p.shape[1]), p_idx),
            pl.BlockSpec((tm, ya_s.shape[1]), s_idx),
            pl.BlockSpec((tm, yb_s.shape[1]), s_idx),
            pl.BlockSpec((tm, d), lambda i: (i, gcol)),
            pl.BlockSpec((tm, d), lambda i: (i, gcol + 1)),
            pl.BlockSpec((2, d), lambda i: (0, 0)),
            pl.BlockSpec(wa.shape, lambda i: (0, 0), **_SINGLE),
            pl.BlockSpec(wb.shape, lambda i: (0, 0), **_SINGLE),
            pl.BlockSpec(wo.shape, lambda i: (0, 0), **_SINGLE),
            pl.BlockSpec((1, d), lambda i: (0, 0)),
        ],
        out_specs=[pl.BlockSpec((tm, d), lambda i: (i, 0)), pl.BlockSpec((tm, d), lambda i: (i, 0))],
        out_shape=[jax.ShapeDtypeStruct((n, d), F32), jax.ShapeDtypeStruct((n, d), BF16)],
        compiler_params=_cparams("arbitrary"),
        name="merge",
    )(x, ya_p, yb_p, ya_s, yb_s, p1, p1, b_gate, wa, wb, wo, next_norm_w.reshape(1, d))


def _iota2(shape, dim):
    return lax.broadcasted_iota(jnp.int32, shape, dim)


def _block_masks(seq_len):
    tb = TOKEN_BLOCK
    row = _iota2((tb, tb), 0)
    col = _iota2((tb, tb), 1)
    if seq_len >= tb:
        causal = col <= row
        last = col == tb - 1
    else:
        causal = (col <= row) & ((row // seq_len) == (col // seq_len))
        last = col == (row // seq_len) * seq_len + (seq_len - 1)
    return causal, last


def _init_constants(wm_ref, e_ref, gw_ref, causal):
    for h in range(GM_HEADS):
        wm_ref[h] = jnp.where(causal, gw_ref[h], 0.0).astype(BF16)
    hrow = _iota2(e_ref.shape, 0)
    ccol = _iota2(e_ref.shape, 1)
    e_ref[...] = jnp.where(hrow == ccol // SSD_HEAD_DIM, 1.0, 0.0).astype(BF16)


def _gating(u_ref, v_ref, gvn_ref, bs_ref, wm_ref, ya_ref, vn_ref):
    vg = _gelu(v_ref[...].astype(F32))
    r = lax.rsqrt(jnp.mean(vg * vg, axis=-1, keepdims=True) + EPS)
    vn = (vg * r) * gvn_ref[...]
    if vn_ref is not None:
        vn_ref[...] = vn
    vnb = vn.astype(BF16)
    for h in range(GM_HEADS):
        cs = slice(h * LANES, (h + 1) * LANES)
        s = _dot(wm_ref[h], vnb[:, cs]) + bs_ref[:, h:h + 1]
        ya_ref[:, cs] = (_gelu(u_ref[:, cs].astype(F32)) * s).astype(ya_ref.dtype)


def _conv(xp_ref, cw_ref, cb_ref, xc_ref, cst_ref, seq_len):
    tb = TOKEN_BLOCK
    width = xc_ref.shape[1]
    cw = 512
    ns = CONV_W - 1
    place = None
    if cst_ref is not None:
        tpos = _iota2((tb, 1), 0) % seq_len
        row = _iota2((tb, tb), 0)
        col = _iota2((tb, tb), 1)
        place = [jnp.where((col // ns == row // seq_len) & (col % ns == row % seq_len + ns - d)
                           & (row % seq_len < d), 1.0, 0.0).astype(BF16) for d in range(1, CONV_W)]
        pad = jnp.zeros((tb - cst_ref.shape[0], cw), F32)
    for c0 in range(0, width, cw):
        cs = slice(c0, c0 + cw)
        acc = None
        full = xp_ref[:, cs]
        if place is not None:
            pieces = _split3(jnp.concatenate([cst_ref[:, cs], pad], axis=0))
        for k in range(CONV_W):
            d = CONV_W - 1 - k
            xs = (pltpu.roll(full, d, axis=0) if d else full)[SUBLANES:SUBLANES + tb, :]
            if place is not None and d > 0:
                pd = place[d - 1]
                halo = (_dot(pd, pieces[0]) + _dot(pd, pieces[1])) + _dot(pd, pieces[2])
                xs = jnp.where(tpos >= d, xs, 0.0) + halo
            term = cw_ref[k:k + 1, cs] * xs
            acc = term if acc is None else acc + term
        xc_ref[:, cs] = _silu(cb_ref[:, cs] + acc)


def _ssd_prepare(dt_ref, dtb_ref, alog_ref, e_ref, causal, last, dtf_ref, tef_ref, eaf_ref):
    dt = _softplus(dt_ref[...] + dtb_ref[...])
    a = -jnp.exp(alog_ref[...])
    da = dt * a
    lmat = jnp.where(causal, 1.0, 0.0).astype(BF16)
    acum = _sel_left(lmat, da)
    a_last = _sel_left(jnp.where(last, 1.0, 0.0).astype(BF16), acum)
    ea = jnp.exp(acum)
    e = e_ref[...]
    dtf_ref[...] = _sel_right(dt, e)
    tef_ref[...] = _sel_right(dt * jnp.exp(a_last - acum), e)
    eaf_ref[...] = _sel_right(ea, e)
    return acum, acum.T


def _ssd_diag_group(g, xc_ref, dtf_ref, acum, acum_t, causal):
    tb = TOKEN_BLOCK
    width = xc_ref.shape[1] - 2 * SSD_GROUPS * D_STATE
    gw = width // SSD_GROUPS
    bg = xc_ref[:, width + g * D_STATE:width + (g + 1) * D_STATE]
    cg = xc_ref[:, width + (SSD_GROUPS + g) * D_STATE:width + (SSD_GROUPS + g + 1) * D_STATE].astype(BF16)
    cb = jnp.where(causal, _dot_nt(cg, bg.astype(BF16)), 0.0)
    lane = _iota2((tb, LANES), 1)
    heads_per_group = gw // SSD_HEAD_DIM
    outs = []
    for jp in range(heads_per_group // 2):
        ms = []
        for hh in range(2):
            h = g * heads_per_group + 2 * jp + hh
            seg = jnp.minimum(acum[:, h:h + 1] - acum_t[h:h + 1, :], 0.0)
            ms.append((cb * jnp.exp(seg)).astype(BF16))
        lhs = jnp.concatenate(ms, axis=1)
        ps = slice(g * gw + jp * LANES, g * gw + (jp + 1) * LANES)
        xpair = (xc_ref[:, ps] * dtf_ref[:, ps]).astype(BF16)
        zero = jnp.zeros_like(xpair)
        rhs = jnp.concatenate([jnp.where(lane < SSD_HEAD_DIM, xpair, zero),
                               jnp.where(lane >= SSD_HEAD_DIM, xpair, zero)], axis=0)
        outs.append(_dot(lhs, rhs))
    return jnp.concatenate(outs, axis=1), cg, bg


def _finish_group(g, y, xc_ref, z_ref, dsk_ref, sn_ref, yb_ref):
    gw = yb_ref.shape[1] // SSD_GROUPS
    gs = slice(g * gw, (g + 1) * gw)
    y = y + dsk_ref[:, gs] * xc_ref[:, gs]
    yz = y * _silu(z_ref[:, gs].astype(F32))
    r = lax.rsqrt(jnp.mean(yz * yz, axis=-1, keepdims=True) + EPS)
    yb_ref[:, gs] = ((yz * r) * sn_ref[:, gs]).astype(yb_ref.dtype)


def _prompt_body(u_ref, v_ref, z_ref, xbc_ref, dt_ref, gvn_ref, gw_ref, bs_ref, cw_ref, cb_ref,
                 dtb_ref, alog_ref, dsk_ref, sn_ref,
                 ya_ref, yb_ref, hs_ref, ct_ref,
                 wm_ref, e_ref, xp_ref, xc_ref, dtf_ref, tef_ref, eaf_ref, st_ref):
    tb = TOKEN_BLOCK
    c = pl.program_id(1)
    causal, last = _block_masks(tb)

    @pl.when((pl.program_id(0) == 0) & (c == 0))
    def _():
        _init_constants(wm_ref, e_ref, gw_ref, causal)

    @pl.when(c == 0)
    def _():
        st_ref[...] = jnp.zeros_like(st_ref)
        xp_ref[0:SUBLANES, :] = jnp.zeros((SUBLANES, xp_ref.shape[1]), F32)

    _gating(u_ref, v_ref, gvn_ref, bs_ref, wm_ref, ya_ref, None)

    xp_ref[SUBLANES:SUBLANES + tb, :] = xbc_ref[...]
    _conv(xp_ref, cw_ref, cb_ref, xc_ref, None, tb)
    xp_ref[0:SUBLANES, :] = xbc_ref[tb - SUBLANES:tb, :]

    acum, acum_t = _ssd_prepare(dt_ref, dtb_ref, alog_ref, e_ref, causal, last, dtf_ref, tef_ref, eaf_ref)

    gw = yb_ref.shape[1] // SSD_GROUPS
    for g in range(SSD_GROUPS):
        gs = slice(g * gw, (g + 1) * gw)
        yd, cg, bg = _ssd_diag_group(g, xc_ref, dtf_ref, acum, acum_t, causal)
        st = st_ref[:, gs]
        y = yd + _dot(cg, st.astype(BF16)) * eaf_ref[:, gs]
        _finish_group(g, y, xc_ref, z_ref, dsk_ref, sn_ref, yb_ref)
        xw = (xc_ref[:, gs] * tef_ref[:, gs]).astype(BF16)
        st_ref[:, gs] = eaf_ref[tb - 1:tb, gs] * st + _dot(bg.T.astype(BF16), xw)

    @pl.when(c == pl.num_programs(1) - 1)
    def _():
        for k in range(st_ref.shape[1] // LANES):
            hs_ref[k * LANES:(k + 1) * LANES, :] = st_ref[:, k * LANES:(k + 1) * LANES].T
        ct_ref[...] = xbc_ref[tb - SUBLANES:tb, :]


def _sample_body(u_ref, v_ref, z_ref, xbc_ref, dt_ref, cst_ref, h0_ref,
                 gvn_ref, gw_ref, bs_ref, cw_ref, cb_ref, dtb_ref, alog_ref, dsk_ref, sn_ref,
                 ya_ref, yb_ref, vn_ref, hs_ref,
                 wm_ref, e_ref, xp_ref, xc_ref, dtf_ref, tef_ref, eaf_ref,
                 y_ref, yoff_ref, cg_ref, bg_ref, xwt_ref, eat_ref, cdh_ref, *, seq_len):
    tb = TOKEN_BLOCK
    k = pl.program_id(1)
    causal, last = _block_masks(seq_len)
    gw = yb_ref.shape[1] // SSD_GROUPS

    @pl.when((pl.program_id(0) == 0) & (k == 0))
    def _():
        _init_constants(wm_ref, e_ref, gw_ref, causal)
        xp_ref[0:SUBLANES, :] = jnp.zeros((SUBLANES, xp_ref.shape[1]), F32)

    @pl.when(k == 0)
    def _():
        _gating(u_ref, v_ref, gvn_ref, bs_ref, wm_ref, ya_ref, vn_ref)
        xp_ref[SUBLANES:SUBLANES + tb, :] = xbc_ref[...]
        _conv(xp_ref, cw_ref, cb_ref, xc_ref, cst_ref, seq_len)
        acum, acum_t = _ssd_prepare(dt_ref, dtb_ref, alog_ref, e_ref, causal, last, dtf_ref, tef_ref, eaf_ref)
        eat_ref[...] = jnp.exp(acum_t)
        for g in range(SSD_GROUPS):
            gs = slice(g * gw, (g + 1) * gw)
            yd, cg, bg = _ssd_diag_group(g, xc_ref, dtf_ref, acum, acum_t, causal)
            y_ref[:, gs] = yd
            cg_ref[:, g * D_STATE:(g + 1) * D_STATE] = cg.astype(F32)
            bg_ref[:, g * D_STATE:(g + 1) * D_STATE] = bg
            xw = xc_ref[:, gs] * tef_ref[:, gs]
            for q in range(gw // LANES):
                r0 = g * gw + q * LANES
                xwt_ref[r0:r0 + LANES, :] = xw[:, q * LANES:(q + 1) * LANES].T.astype(BF16)

    nseq = h0_ref.shape[0]
    nt = nseq * seq_len
    rows = pl.ds(pl.multiple_of(k * nt, nt), nt)
    rowseq = _iota2((tb, 1), 0) // seq_len
    subseq = _iota2((nt, 1), 0) // seq_len
    tok = _iota2((tb, LANES), 0)
    onehot = jnp.concatenate(
        [jnp.where(tok == (k * nseq + b) * seq_len + (seq_len - 1), 1.0, 0.0) for b in range(nseq)],
        axis=1).astype(BF16)
    cdh_ref[...] = _sel_right(eat_ref[...], onehot)
    hpg = gw // SSD_HEAD_DIM
    for g in range(SSD_GROUPS):
        gs = slice(g * gw, (g + 1) * gw)
        ds = slice(g * D_STATE, (g + 1) * D_STATE)
        c_sub = cg_ref[rows, ds]
        lhs = jnp.concatenate([jnp.where(subseq == b, c_sub, 0.0) for b in range(nseq)], axis=1).astype(BF16)
        h0cat = jnp.concatenate([h0_ref[b, gs, :].astype(BF16) for b in range(nseq)], axis=1)
        yoff_ref[rows, gs] = _dot_nt(lhs, h0cat)
        b_all = bg_ref[:, ds]
        bm = jnp.concatenate([jnp.where(rowseq == k * nseq + b, b_all, 0.0) for b in range(nseq)],
                             axis=1).astype(BF16)
        s_new = _dot(xwt_ref[gs, :], bm)
        for b in range(nseq):
            ls = slice(b * D_STATE, (b + 1) * D_STATE)
            for j in range(hpg):
                h = g * hpg + j
                r0 = g * gw + j * SSD_HEAD_DIM
                hs_ref[b, r0:r0 + SSD_HEAD_DIM, :] = (cdh_ref[h:h + 1, ls] * h0_ref[b, r0:r0 + SSD_HEAD_DIM, :]
                                                      + s_new[j * SSD_HEAD_DIM:(j + 1) * SSD_HEAD_DIM, ls])

    @pl.when(k == pl.num_programs(1) - 1)
    def _():
        for g in range(SSD_GROUPS):
            gs = slice(g * gw, (g + 1) * gw)
            y = y_ref[:, gs] + yoff_ref[:, gs] * eaf_ref[:, gs]
            _finish_group(g, y, xc_ref, z_ref, dsk_ref, sn_ref, yb_ref)


def _mixer_params(lp):
    (gm_v_norm, gm_ws, gm_bs, conv_w, conv_b, dt_bias, a_log, d_skip, ssd_norm) = lp
    nh = dt_bias.shape[0]
    pad = LANES - nh
    return dict(
        gvn=gm_v_norm.reshape(1, -1),
        cw=conv_w,
        cb=conv_b.reshape(1, -1),
        dtb=jnp.pad(dt_bias, (0, pad)).reshape(1, LANES),
        alog=jnp.pad(a_log, (0, pad)).reshape(1, LANES),
        dsk=jnp.repeat(d_skip, SSD_HEAD_DIM).reshape(1, -1),
        sn=ssd_norm.reshape(1, -1),
    )


def _full_spec(a):
    nd = a.ndim
    return pl.BlockSpec(a.shape, lambda i, j: (0,) * nd)


def _mixer_prompt(p1, xbc, dt, lp, n_seq, seq_len, gm_w, ssd_w, conv_dim):
    tb = TOKEN_BLOCK
    nc = seq_len // tb
    mp = _mixer_params(lp)
    gw_full = lp[1]
    bs_t = lp[2].T
    row = lambda b, c: b * nc + c
    params = [mp["gvn"], gw_full, bs_t, mp["cw"], mp["cb"], mp["dtb"], mp["alog"], mp["dsk"], mp["sn"]]
    return pl.pallas_call(
        _prompt_body,
        grid=(n_seq, nc),
        in_specs=[
            pl.BlockSpec((tb, gm_w), lambda b, c: (row(b, c), 0)),
            pl.BlockSpec((tb, gm_w), lambda b, c: (row(b, c), 1)),
            pl.BlockSpec((tb, ssd_w), lambda b, c: (row(b, c), 1)),
            pl.BlockSpec((tb, conv_dim), lambda b, c: (row(b, c), 0)),
            pl.BlockSpec((tb, LANES), lambda b, c: (row(b, c), 0)),
        ] + [_full_spec(a) for a in params],
        out_specs=[
            pl.BlockSpec((tb, gm_w), lambda b, c: (row(b, c), 0)),
            pl.BlockSpec((tb, ssd_w), lambda b, c: (row(b, c), 0)),
            pl.BlockSpec((None, ssd_w, D_STATE), lambda b, c: (b, 0, 0)),
            pl.BlockSpec((None, SUBLANES, conv_dim), lambda b, c: (b, 0, 0)),
        ],
        out_shape=[
            jax.ShapeDtypeStruct((n_seq * seq_len, gm_w), BF16),
            jax.ShapeDtypeStruct((n_seq * seq_len, ssd_w), BF16),
            jax.ShapeDtypeStruct((n_seq, ssd_w, D_STATE), F32),
            jax.ShapeDtypeStruct((n_seq, SUBLANES, conv_dim), F32),
        ],
        scratch_shapes=[
            pltpu.VMEM((GM_HEADS, tb, tb), BF16),
            pltpu.VMEM((LANES, ssd_w), BF16),
            pltpu.VMEM((SUBLANES + tb, conv_dim), F32),
            pltpu.VMEM((tb, conv_dim), F32),
            pltpu.VMEM((tb, ssd_w), F32),
            pltpu.VMEM((tb, ssd_w), F32),
            pltpu.VMEM((tb, ssd_w), F32),
            pltpu.VMEM((D_STATE, ssd_w), F32),
        ],
        compiler_params=_cparams("arbitrary", "arbitrary"),
        name="mixer_prompt",
    )(p1, p1, p1, xbc, dt, *params)


def _mixer_sample(p1, xbc, dt, lp, h0, conv_state, row0, n_seq, seq_len, gm_w, ssd_w, conv_dim):
    tb = TOKEN_BLOCK
    seqs_per_block = tb // seq_len
    nblk = n_seq // seqs_per_block
    nsub = seqs_per_block // SAMPLE_SEQS_PER_STEP
    blk0 = row0 // tb
    mp = _mixer_params(lp)
    reps = tb // seq_len
    gw_tiled = jnp.tile(lp[1][:, :seq_len, :seq_len], (1, reps, reps))
    bs_t = jnp.tile(lp[2][:, :seq_len], (1, reps)).T
    params = [mp["gvn"], gw_tiled, bs_t, mp["cw"], mp["cb"], mp["dtb"], mp["alog"], mp["dsk"], mp["sn"]]
    ntok = n_seq * seq_len
    return pl.pallas_call(
        functools.partial(_sample_body, seq_len=seq_len),
        grid=(nblk, nsub),
        in_specs=[
            pl.BlockSpec((tb, gm_w), lambda i, k: (blk0 + i, 0)),
            pl.BlockSpec((tb, gm_w), lambda i, k: (blk0 + i, 1)),
            pl.BlockSpec((tb, ssd_w), lambda i, k: (blk0 + i, 1)),
            pl.BlockSpec((tb, conv_dim), lambda i, k: (blk0 + i, 0)),
            pl.BlockSpec((tb, LANES), lambda i, k: (blk0 + i, 0)),
            pl.BlockSpec((seqs_per_block * (CONV_W - 1), conv_dim), lambda i, k: (i, 0)),
            pl.BlockSpec((SAMPLE_SEQS_PER_STEP, ssd_w, D_STATE), lambda i, k: (i * nsub + k, 0, 0)),
        ] + [_full_spec(a) for a in params],
        out_specs=[
            pl.BlockSpec((tb, gm_w), lambda i, k: (i, 0)),
            pl.BlockSpec((tb, ssd_w), lambda i, k: (i, 0)),
            pl.BlockSpec((tb, gm_w), lambda i, k: (i, 0)),
            pl.BlockSpec((SAMPLE_SEQS_PER_STEP, ssd_w, D_STATE), lambda i, k: (i * nsub + k, 0, 0)),
        ],
        out_shape=[
            jax.ShapeDtypeStruct((ntok, gm_w), BF16),
            jax.ShapeDtypeStruct((ntok, ssd_w), BF16),
            jax.ShapeDtypeStruct((ntok, gm_w), F32),
            jax.ShapeDtypeStruct((n_seq, ssd_w, D_STATE), F32),
        ],
        scratch_shapes=[
            pltpu.VMEM((GM_HEADS, tb, tb), BF16),
            pltpu.VMEM((LANES, ssd_w), BF16),
            pltpu.VMEM((SUBLANES + tb, conv_dim), F32),
            pltpu.VMEM((tb, conv_dim), F32),
            pltpu.VMEM((tb, ssd_w), F32),
            pltpu.VMEM((tb, ssd_w), F32),
            pltpu.VMEM((tb, ssd_w), F32),
            pltpu.VMEM((tb, ssd_w), F32),
            pltpu.VMEM((tb, ssd_w), F32),
            pltpu.VMEM((tb, SSD_GROUPS * D_STATE), F32),
            pltpu.VMEM((tb, SSD_GROUPS * D_STATE), F32),
            pltpu.VMEM((ssd_w, tb), BF16),
            pltpu.VMEM((LANES, tb), F32),
            pltpu.VMEM((LANES, SAMPLE_SEQS_PER_STEP * LANES), F32),
        ],
        compiler_params=_cparams("arbitrary", "arbitrary"),
        name="mixer_sample",
    )(p1, p1, p1, xbc, dt, conv_state, h0, *params)


def kernel(x_prompt, x_sample, state_ssm, state_conv, ffn1_norm, ffn1_w1, ffn1_w3, ffn1_w2, mix_norm, w_in,
           b_gate, conv_w, conv_b, dt_bias, a_log, d_skip, ssd_norm, gm_v_norm, gm_ws, gm_bs, w_proj_a,
           w_proj_b, w_out, ffn2_norm, ffn2_w1, ffn2_w3, ffn2_w2, final_norm):
    bp, tp, d = x_prompt.shape
    bs_, ts, _ = x_sample.shape
    depth = w_in.shape[0]
    n_heads = dt_bias.shape[1]
    gm_w = gm_v_norm.shape[1]
    ssd_w = ssd_norm.shape[1]
    conv_dim = conv_w.shape[2]
    np_, ns = bp * tp, bs_ * ts

    xs = [x_prompt.reshape(np_, d), x_sample.reshape(ns, d)]
    ssm_p, conv_p, ssm_s, conv_s, v_s = [], [], [], [], []
    for l in range(depth):
        c_uvz = 2 * gm_w + ssd_w
        x, xn = _ffn(xs, ffn1_norm[l], ffn1_w1[l], ffn1_w3[l], ffn1_w2[l], next_norm_w=mix_norm[l])
        p1, xbc, dt = _inproj(xn, w_in[l].T, c_uvz, conv_dim, n_heads)

        lp = (gm_v_norm[l], gm_ws[l], gm_bs[l], conv_w[l], conv_b[l], dt_bias[l], a_log[l], d_skip[l], ssd_norm[l])
        ya_p, yb_p, hs_p, ct_p = _mixer_prompt(p1, xbc, dt, lp, bp, tp, gm_w, ssd_w, conv_dim)

        st = state_conv[l]
        ya_s, yb_s, vn_s, hs_s = _mixer_sample(p1, xbc, dt, lp, state_ssm[l].reshape(bs_, ssd_w, D_STATE),
                                               st.reshape(bs_ * (CONV_W - 1), conv_dim),
                                               np_, bs_, ts, gm_w, ssd_w, conv_dim)

        x, xn = _merge(x, ya_p, yb_p, ya_s, yb_s, p1, b_gate[l], w_proj_a[l].astype(BF16), w_proj_b[l].astype(BF16),
                       w_out[l].astype(BF16), ffn2_norm[l])
        last = l == depth - 1
        xs = _ffn([x], ffn2_norm[l], ffn2_w1[l], ffn2_w3[l], ffn2_w2[l], xn=xn, final_w=final_norm if last else None,
                  split_out=True)

        xbc_s = xbc[np_:].reshape(bs_, ts, conv_dim)
        ssm_p.append(hs_p.reshape(bp, n_heads, SSD_HEAD_DIM, D_STATE))
        conv_p.append(ct_p[:, SUBLANES - (CONV_W - 1):, :])
        ssm_s.append(hs_s.reshape(bs_, n_heads, SSD_HEAD_DIM, D_STATE))
        conv_s.append(jnp.concatenate([st, xbc_s], axis=1)[:, ts:, :])
        v_s.append(vn_s.reshape(bs_, ts, gm_w))

    return (xs[0].reshape(bp, tp, d), xs[1].reshape(bs_, ts, d), jnp.stack(ssm_p), jnp.stack(conv_p),
            jnp.stack(ssm_s), jnp.stack(conv_s), jnp.stack(v_s))
```

```python
import functools

import jax
import jax.numpy as jnp
from jax import lax
from jax.experimental import pallas as pl
from jax.experimental.pallas import tpu as pltpu

F32 = jnp.float32
BF16 = jnp.bfloat16
EPS = 1e-6

LANES = 128
SUBLANES = 8
VMEM_LIMIT_BYTES = 60 * 1024 * 1024

GM_HEADS = 8
SSD_HEAD_DIM = 64
SSD_GROUPS = 4
D_STATE = 128
CONV_W = 4
TOKEN_BLOCK = 128
SAMPLE_SEQS_PER_STEP = 4

TOKEN_TILE = 1088
FFN_COLS = 512
PROJ_COLS = 1024
MERGE_TILE = 256
NORM_ROWS = 64

_SINGLE = dict(pipeline_mode=pl.Buffered(1))


def _cparams(*sem):
    return pltpu.CompilerParams(dimension_semantics=sem, vmem_limit_bytes=VMEM_LIMIT_BYTES)


_GELU_C = 0.7978845608028654


def _gelu(x):
    t = jnp.tanh(x * (_GELU_C + (_GELU_C * 0.044715) * (x * x)))
    return x * (0.5 + 0.5 * t)


def _silu(x):
    return x * jax.nn.sigmoid(x)


def _softplus(x):
    return jnp.maximum(x, 0.0) + jnp.log1p(jnp.exp(-jnp.abs(x)))


def _split3(a):
    a1 = a.astype(BF16)
    r1 = a - a1.astype(F32)
    a2 = r1.astype(BF16)
    r2 = r1 - a2.astype(F32)
    return a1, a2, r2.astype(BF16)


def _dot(a, b):
    return jnp.dot(a, b, preferred_element_type=F32)


def _dot_nt(a, b):
    return lax.dot_general(a, b, (((1,), (1,)), ((), ())), preferred_element_type=F32)


def _sel_right(a, sel):
    a1, a2, a3 = _split3(a)
    return (_dot(a1, sel) + _dot(a2, sel)) + _dot(a3, sel)


def _sel_left(sel, a):
    a1, a2, a3 = _split3(a)
    return (_dot(sel, a1) + _dot(sel, a2)) + _dot(sel, a3)


def _row_loop(nrows, fn):
    def body(i, c):
        fn(pl.ds(pl.multiple_of(i * NORM_ROWS, NORM_ROWS), NORM_ROWS))
        return c

    lax.fori_loop(0, nrows // NORM_ROWS, body, 0)


def _rmsnorm_rows(x_ref, w_ref, out_ref, nrows):
    def one(sl):
        x = x_ref[sl, :]
        r = lax.rsqrt(jnp.mean(x * x, axis=-1, keepdims=True) + EPS)
        out_ref[sl, :] = ((x * r) * w_ref[...]).astype(out_ref.dtype)

    _row_loop(nrows, one)


def _tile_branches(n_full, tile, tail_rows, run, axis=0):
    i = pl.program_id(axis)

    @pl.when(i < n_full)
    def _():
        run(False, tile)

    if tail_rows:
        @pl.when(i >= n_full)
        def _():
            run(True, tail_rows)


def _tile_dma(action, t, *, hbm_refs, buf, sem, to_hbm, rows_first):
    tm = buf.shape[0]

    def go(pieces):
        for hbm, h0, b0, nr in pieces:
            h = hbm.at[pl.ds(h0, nr), :]
            v = buf.at[pl.ds(b0, nr), :]
            copy = pltpu.make_async_copy(v, h, sem) if to_hbm else pltpu.make_async_copy(h, v, sem)
            getattr(copy, action)()

    if len(hbm_refs) == 1:
        go([(hbm_refs[0], pl.multiple_of(t * tm, SUBLANES), 0, tm)])
        return
    n_pure, head = divmod(rows_first, tm)

    @pl.when(t < n_pure)
    def _():
        go([(hbm_refs[0], pl.multiple_of(t * tm, SUBLANES), 0, tm)])

    @pl.when(t >= n_pure)
    def _():
        go([(hbm_refs[0], n_pure * tm, 0, head), (hbm_refs[1], 0, head, tm - head)])


def _ffn_body(*refs, n_in, n_out, final_norm, emit_norm, has_xn, n_tiles, rows_first_in, rows_first_out):
    refs = list(refs)
    x_refs = [refs.pop(0) for _ in range(n_in)]
    xni_ref = refs.pop(0) if has_xn else None
    nw_ref, w1_ref, w3_ref, w2_ref, fn_ref = [refs.pop(0) for _ in range(5)]
    o_refs = [refs.pop(0) for _ in range(n_out)]
    xno_ref = refs.pop(0) if emit_norm else None
    acc_ref, xn_ref, in_sem, out_sem = refs
    i = pl.program_id(0)
    j = pl.program_id(1)
    nj = pl.num_programs(1)
    tm = acc_ref.shape[1]
    slot = i % 2
    other = 1 - slot
    acc = acc_ref.at[slot]
    dma_in = functools.partial(_tile_dma, hbm_refs=x_refs, to_hbm=False, rows_first=rows_first_in)
    dma_out = functools.partial(_tile_dma, hbm_refs=o_refs, to_hbm=True, rows_first=rows_first_out)

    @pl.when(j == 0)
    def _():
        @pl.when(i == 0)
        def _():
            dma_in("start", i, buf=acc, sem=in_sem.at[slot])

        dma_in("wait", i, buf=acc, sem=in_sem.at[slot])
        if not has_xn:
            _rmsnorm_rows(acc, nw_ref, xn_ref, tm)

    xn = (xni_ref if has_xn else xn_ref)[...]
    h1 = _dot(xn, w1_ref[...].astype(BF16))
    h3 = _dot(xn, w3_ref[...].astype(BF16))
    g = ((0.5 * _silu(h1)) * h3).astype(BF16)
    acc[...] += _dot(g, w2_ref[...].astype(BF16))

    @pl.when((j == nj - 2) & (i + 1 < n_tiles))
    def _():
        @pl.when(i >= 1)
        def _():
            dma_out("wait", i - 1, buf=acc_ref.at[other], sem=out_sem.at[other])

        dma_in("start", i + 1, buf=acc_ref.at[other], sem=in_sem.at[other])

    @pl.when(j == nj - 1)
    def _():
        if final_norm:
            _rmsnorm_rows(acc, fn_ref, acc, tm)
        if emit_norm:
            _rmsnorm_rows(acc, fn_ref, xno_ref, tm)
        dma_out("start", i, buf=acc, sem=out_sem.at[slot])

        @pl.when(i == n_tiles - 1)
        def _():
            if n_tiles > 1:
                dma_out("wait", i - 1, buf=acc_ref.at[other], sem=out_sem.at[other])
            dma_out("wait", i, buf=acc, sem=out_sem.at[slot])


def _ffn(xs, norm_w, w1, w3, w2, *, xn=None, final_w=None, next_norm_w=None, split_rows=None):
    d, f = w1.shape
    tm, tf = TOKEN_TILE, FFN_COLS
    n = sum(x.shape[0] for x in xs)
    assert n % tm == 0 and f % tf == 0 and f // tf >= 2
    n_tiles = n // tm
    for first in ([xs[0].shape[0]] if len(xs) == 2 else []) + ([split_rows] if split_rows else []):
        assert first % tm and (first // tm + 1) * tm == n
    final_norm = final_w is not None
    emit_norm = next_norm_w is not None
    assert not (final_norm and emit_norm)
    fw = final_w if final_norm else (next_norm_w if emit_norm else norm_w)
    has_xn = xn is not None
    any_spec = pl.BlockSpec(memory_space=pl.ANY)
    in_specs = [any_spec for _ in xs] + ([pl.BlockSpec((tm, d), lambda i, j: (i, 0))] if has_xn else [])
    if split_rows:
        o_specs = [any_spec, any_spec]
        o_shapes = [jax.ShapeDtypeStruct((split_rows, d), F32), jax.ShapeDtypeStruct((n - split_rows, d), F32)]
    else:
        o_specs = [any_spec]
        o_shapes = [jax.ShapeDtypeStruct((n, d), F32)]
    n_out = len(o_specs)
    if emit_norm:
        o_specs.append(pl.BlockSpec((tm, d), lambda i, j: (i, 0), **_SINGLE))
        o_shapes.append(jax.ShapeDtypeStruct((n, d), BF16))
    return pl.pallas_call(
        functools.partial(_ffn_body, n_in=len(xs), n_out=n_out, final_norm=final_norm, emit_norm=emit_norm,
                          has_xn=has_xn, n_tiles=n_tiles, rows_first_in=xs[0].shape[0], rows_first_out=split_rows),
        grid=(n_tiles, f // tf),
        in_specs=in_specs + [
            pl.BlockSpec((1, d), lambda i, j: (0, 0)),
            pl.BlockSpec((d, tf), lambda i, j: (0, j)),
            pl.BlockSpec((d, tf), lambda i, j: (0, j)),
            pl.BlockSpec((tf, d), lambda i, j: (j, 0)),
            pl.BlockSpec((1, d), lambda i, j: (0, 0)),
        ],
        out_specs=o_specs,
        out_shape=o_shapes,
        scratch_shapes=[
            pltpu.VMEM((2, tm, d), F32),
            pltpu.VMEM((SUBLANES, LANES) if has_xn else (tm, d), BF16),
            pltpu.SemaphoreType.DMA((2,)),
            pltpu.SemaphoreType.DMA((2,)),
        ],
        compiler_params=_cparams("arbitrary", "arbitrary"),
        name="ffn",
    )(*xs, *([xn] if has_xn else []), norm_w.reshape(1, d), w1, w3, w2, fw.reshape(1, d))


def _inproj_body(xn_ref, w_ref, wdt_ref, p1_ref, xbc_ref, dt_ref, wb_ref, *,
                 n_full, tail_rows, n_uvz, n_xbc, n_dt):
    j = pl.program_id(0)
    is_xbc = (j >= n_uvz) & (j < n_uvz + n_xbc)

    @pl.when(pl.program_id(1) == 0)
    def _():
        wb_ref[...] = w_ref[...].astype(BF16)

    def run(is_tail, rows):
        @pl.when(j == 0)
        def _():
            dt = _dot_nt(xn_ref[0:rows, :], wdt_ref[...].astype(BF16))
            lane = _iota2(dt.shape, 1)
            dt_ref[0:rows, :] = jnp.where(lane < n_dt, dt, 0.0)

        @pl.when(jnp.logical_not(is_xbc))
        def _():
            p1_ref[0:rows, :] = _dot_nt(xn_ref[0:rows, :], wb_ref[...]).astype(p1_ref.dtype)

        @pl.when(is_xbc)
        def _():
            xbc_ref[0:rows, :] = _dot_nt(xn_ref[0:rows, :], wb_ref[...])

    _tile_branches(n_full, TOKEN_TILE, tail_rows, run, axis=1)


def _inproj(xn, w_t, c_uvz, c_xbc, c_dt):
    n, d = xn.shape
    tm, tn = TOKEN_TILE, PROJ_COLS
    n_full, tail_rows = divmod(n, tm)
    n_tiles = n_full + (1 if tail_rows else 0)
    last_i = n_tiles - 1
    c_gate0 = c_uvz + c_xbc + c_dt
    c_gates = w_t.shape[0] - c_gate0
    n_uvz, n_xbc, n_g = c_uvz // tn, c_xbc // tn, c_gates // tn
    n_main = n_uvz + n_xbc
    assert c_gate0 % SUBLANES == 0 and (c_uvz + c_xbc) % SUBLANES == 0

    def w_row(j, i):
        return (pl.multiple_of(jnp.where(j < n_main, j * tn, c_gate0 + (j - n_main) * tn), SUBLANES), 0)

    def p1_idx(j, i):
        writes = (j < n_uvz) | (j >= n_main)
        col = jnp.where(j < n_uvz, j, jnp.maximum(j - n_xbc, n_uvz - 1))
        return (jnp.where(writes, i, last_i), col)

    def xbc_idx(j, i):
        row = jnp.where(j < n_uvz, 0, jnp.where(j < n_main, i, last_i))
        return (row, jnp.clip(j - n_uvz, 0, n_xbc - 1))

    return pl.pallas_call(
        functools.partial(_inproj_body, n_full=n_full, tail_rows=tail_rows, n_uvz=n_uvz, n_xbc=n_xbc, n_dt=c_dt),
        grid=(n_main + n_g, n_tiles),
        in_specs=[
            pl.BlockSpec((tm, d), lambda j, i: (i, 0)),
            pl.BlockSpec((pl.Element(tn), pl.Element(d)), w_row),
            pl.BlockSpec((pl.Element(LANES), pl.Element(d)), lambda j, i: (c_uvz + c_xbc, 0)),
        ],
        out_specs=[
            pl.BlockSpec((tm, tn), p1_idx),
            pl.BlockSpec((tm, tn), xbc_idx),
            pl.BlockSpec((tm, LANES), lambda j, i: (jnp.where(j == 0, i, last_i), 0)),
        ],
        out_shape=[
            jax.ShapeDtypeStruct((n, c_uvz + c_gates), BF16),
            jax.ShapeDtypeStruct((n, c_xbc), F32),
            jax.ShapeDtypeStruct((n, LANES), F32),
        ],
        scratch_shapes=[pltpu.VMEM((tn, d), BF16)],
        compiler_params=_cparams("arbitrary", "arbitrary"),
        name="in_proj",
    )(xn, w_t, w_t)


def _merge_body(x_ref, yap_ref, ybp_ref, yas_ref, ybs_ref, ga_ref, gb_ref, bg_ref, wa_ref, wb_ref, wo_ref, nw_ref,
                o_ref, xn_ref, *, n_prompt_tiles):
    def run(ya_ref, yb_ref):
        pa = _dot(ya_ref[...], wa_ref[...])
        pb = _dot(yb_ref[...], wb_ref[...])
        ga = jax.nn.sigmoid(ga_ref[...].astype(F32) + bg_ref[0:1, :])
        gb = jax.nn.sigmoid(gb_ref[...].astype(F32) + bg_ref[1:2, :])
        m = (ga * pa + gb * pb).astype(BF16)
        o = x_ref[...] + _dot(m, wo_ref[...])
        o_ref[...] = o
        r = lax.rsqrt(jnp.mean(o * o, axis=-1, keepdims=True) + EPS)
        xn_ref[...] = ((o * r) * nw_ref[...]).astype(xn_ref.dtype)

    i = pl.program_id(0)

    @pl.when(i < n_prompt_tiles)
    def _():
        run(yap_ref, ybp_ref)

    @pl.when(i >= n_prompt_tiles)
    def _():
        run(yas_ref, ybs_ref)


def _merge(x, ya_p, yb_p, ya_s, yb_s, p1, b_gate, wa, wb, wo, next_norm_w):
    n, d = x.shape
    tm = MERGE_TILE
    npt = ya_p.shape[0] // tm
    nst = ya_s.shape[0] // tm
    gcol = p1.shape[1] // d - 2
    p_idx = lambda i: (jnp.minimum(i, npt - 1), 0)
    s_idx = lambda i: (jnp.clip(i - npt, 0, nst - 1), 0)
    return pl.pallas_call(
        functools.partial(_merge_body, n_prompt_tiles=npt),
        grid=(n // tm,),
        in_specs=[
            pl.BlockSpec((tm, d), lambda i: (i, 0)),
            pl.BlockSpec((tm, ya_p.shape[1]), p_idx),
            pl.BlockSpec((tm, yb_p.shape[1]), p_idx),
            pl.BlockSpec((tm, ya_s.shape[1]), s_idx),
            pl.BlockSpec((tm, yb_s.shape[1]), s_idx),
            pl.BlockSpec((tm, d), lambda i: (i, gcol)),
            pl.BlockSpec((tm, d), lambda i: (i, gcol + 1)),
            pl.BlockSpec((2, d), lambda i: (0, 0)),
            pl.BlockSpec(wa.shape, lambda i: (0, 0), **_SINGLE),
            pl.BlockSpec(wb.shape, lambda i: (0, 0), **_SINGLE),
            pl.BlockSpec(wo.shape, lambda i: (0, 0), **_SINGLE),
            pl.BlockSpec((1, d), lambda i: (0, 0)),
        ],
        out_specs=[pl.BlockSpec((tm, d), lambda i: (i, 0)), pl.BlockSpec((tm, d), lambda i: (i, 0))],
        out_shape=[jax.ShapeDtypeStruct((n, d), F32), jax.ShapeDtypeStruct((n, d), BF16)],
        compiler_params=_cparams("arbitrary"),
        name="merge",
    )(x, ya_p, yb_p, ya_s, yb_s, p1, p1, b_gate, wa, wb, wo, next_norm_w.reshape(1, d))


def _iota2(shape, dim):
    return lax.broadcasted_iota(jnp.int32, shape, dim)


def _block_masks(seq_len):
    tb = TOKEN_BLOCK
    row = _iota2((tb, tb), 0)
    col = _iota2((tb, tb), 1)
    if seq_len >= tb:
        causal = col <= row
        last = col == tb - 1
    else:
        causal = (col <= row) & ((row // seq_len) == (col // seq_len))
        last = col == (row // seq_len) * seq_len + (seq_len - 1)
    return causal, last


def _init_constants(wm_ref, e_ref, gw_ref, causal):
    for h in range(GM_HEADS):
        wm_ref[h] = jnp.where(causal, gw_ref[h], 0.0).astype(BF16)
    hrow = _iota2(e_ref.shape, 0)
    ccol = _iota2(e_ref.shape, 1)
    e_ref[...] = jnp.where(hrow == ccol // SSD_HEAD_DIM, 1.0, 0.0).astype(BF16)


def _gating(u_ref, v_ref, gvn_ref, bs_ref, wm_ref, ya_ref, vn_ref):
    vg = _gelu(v_ref[...].astype(F32))
    r = lax.rsqrt(jnp.mean(vg * vg, axis=-1, keepdims=True) + EPS)
    vn = (vg * r) * gvn_ref[...]
    if vn_ref is not None:
        vn_ref[...] = vn
    vnb = vn.astype(BF16)
    for h in range(GM_HEADS):
        cs = slice(h * LANES, (h + 1) * LANES)
        s = _dot(wm_ref[h], vnb[:, cs]) + bs_ref[:, h:h + 1]
        ya_ref[:, cs] = (_gelu(u_ref[:, cs].astype(F32)) * s).astype(ya_ref.dtype)


def _conv(xp_ref, cw_ref, cb_ref, xc_ref, cst_ref, seq_len):
    tb = TOKEN_BLOCK
    width = xc_ref.shape[1]
    cw = 512
    ns = CONV_W - 1
    place = None
    if cst_ref is not None:
        tpos = _iota2((tb, 1), 0) % seq_len
        row = _iota2((tb, tb), 0)
        col = _iota2((tb, tb), 1)
        place = [jnp.where((col // ns == row // seq_len) & (col % ns == row % seq_len + ns - d)
                           & (row % seq_len < d), 1.0, 0.0).astype(BF16) for d in range(1, CONV_W)]
        pad = jnp.zeros((tb - cst_ref.shape[0], cw), F32)
    for c0 in range(0, width, cw):
        cs = slice(c0, c0 + cw)
        acc = None
        full = xp_ref[:, cs]
        if place is not None:
            pieces = _split3(jnp.concatenate([cst_ref[:, cs], pad], axis=0))
        for k in range(CONV_W):
            d = CONV_W - 1 - k
            xs = (pltpu.roll(full, d, axis=0) if d else full)[SUBLANES:SUBLANES + tb, :]
            if place is not None and d > 0:
                pd = place[d - 1]
                halo = (_dot(pd, pieces[0]) + _dot(pd, pieces[1])) + _dot(pd, pieces[2])
                xs = jnp.where(tpos >= d, xs, 0.0) + halo
            term = cw_ref[k:k + 1, cs] * xs
            acc = term if acc is None else acc + term
        xc_ref[:, cs] = _silu(cb_ref[:, cs] + acc)


def _ssd_prepare(dt_ref, dtb_ref, alog_ref, e_ref, causal, last, dtf_ref, tef_ref, eaf_ref):
    dt = _softplus(dt_ref[...] + dtb_ref[...])
    a = -jnp.exp(alog_ref[...])
    da = dt * a
    lmat = jnp.where(causal, 1.0, 0.0).astype(BF16)
    acum = _sel_left(lmat, da)
    a_last = _sel_left(jnp.where(last, 1.0, 0.0).astype(BF16), acum)
    ea = jnp.exp(acum)
    e = e_ref[...]
    dtf_ref[...] = _sel_right(dt, e)
    tef_ref[...] = _sel_right(dt * jnp.exp(a_last - acum), e)
    eaf_ref[...] = _sel_right(ea, e)
    return acum, acum.T


def _ssd_diag_group(g, xc_ref, dtf_ref, acum, acum_t, causal):
    tb = TOKEN_BLOCK
    width = xc_ref.shape[1] - 2 * SSD_GROUPS * D_STATE
    gw = width // SSD_GROUPS
    bg = xc_ref[:, width + g * D_STATE:width + (g + 1) * D_STATE]
    cg = xc_ref[:, width + (SSD_GROUPS + g) * D_STATE:width + (SSD_GROUPS + g + 1) * D_STATE].astype(BF16)
    cb = jnp.where(causal, _dot_nt(cg, bg.astype(BF16)), 0.0)
    lane = _iota2((tb, LANES), 1)
    heads_per_group = gw // SSD_HEAD_DIM
    outs = []
    for jp in range(heads_per_group // 2):
        ms = []
        for hh in range(2):
            h = g * heads_per_group + 2 * jp + hh
            seg = jnp.minimum(acum[:, h:h + 1] - acum_t[h:h + 1, :], 0.0)
            ms.append((cb * jnp.exp(seg)).astype(BF16))
        lhs = jnp.concatenate(ms, axis=1)
        ps = slice(g * gw + jp * LANES, g * gw + (jp + 1) * LANES)
        xpair = (xc_ref[:, ps] * dtf_ref[:, ps]).astype(BF16)
        zero = jnp.zeros_like(xpair)
        rhs = jnp.concatenate([jnp.where(lane < SSD_HEAD_DIM, xpair, zero),
                               jnp.where(lane >= SSD_HEAD_DIM, xpair, zero)], axis=0)
        outs.append(_dot(lhs, rhs))
    return jnp.concatenate(outs, axis=1), cg, bg


def _finish_group(g, y, xc_ref, z_ref, dsk_ref, sn_ref, yb_ref):
    gw = yb_ref.shape[1] // SSD_GROUPS
    gs = slice(g * gw, (g + 1) * gw)
    y = y + dsk_ref[:, gs] * xc_ref[:, gs]
    yz = y * _silu(z_ref[:, gs].astype(F32))
    r = lax.rsqrt(jnp.mean(yz * yz, axis=-1, keepdims=True) + EPS)
    yb_ref[:, gs] = ((yz * r) * sn_ref[:, gs]).astype(yb_ref.dtype)


def _prompt_body(u_ref, v_ref, z_ref, xbc_ref, dt_ref, gvn_ref, gw_ref, bs_ref, cw_ref, cb_ref,
                 dtb_ref, alog_ref, dsk_ref, sn_ref,
                 ya_ref, yb_ref, hs_ref, ct_ref,
                 wm_ref, e_ref, xp_ref, xc_ref, dtf_ref, tef_ref, eaf_ref, st_ref):
    tb = TOKEN_BLOCK
    c = pl.program_id(1)
    causal, last = _block_masks(tb)

    @pl.when((pl.program_id(0) == 0) & (c == 0))
    def _():
        _init_constants(wm_ref, e_ref, gw_ref, causal)

    @pl.when(c == 0)
    def _():
        st_ref[...] = jnp.zeros_like(st_ref)
        xp_ref[0:SUBLANES, :] = jnp.zeros((SUBLANES, xp_ref.shape[1]), F32)

    _gating(u_ref, v_ref, gvn_ref, bs_ref, wm_ref, ya_ref, None)

    xp_ref[SUBLANES:SUBLANES + tb, :] = xbc_ref[...]
    _conv(xp_ref, cw_ref, cb_ref, xc_ref, None, tb)
    xp_ref[0:SUBLANES, :] = xbc_ref[tb - SUBLANES:tb, :]

    acum, acum_t = _ssd_prepare(dt_ref, dtb_ref, alog_ref, e_ref, causal, last, dtf_ref, tef_ref, eaf_ref)

    gw = yb_ref.shape[1] // SSD_GROUPS
    for g in range(SSD_GROUPS):
        gs = slice(g * gw, (g + 1) * gw)
        yd, cg, bg = _ssd_diag_group(g, xc_ref, dtf_ref, acum, acum_t, causal)
        st = st_ref[:, gs]
        y = yd + _dot(cg, st.astype(BF16)) * eaf_ref[:, gs]
        _finish_group(g, y, xc_ref, z_ref, dsk_ref, sn_ref, yb_ref)
        xw = (xc_ref[:, gs] * tef_ref[:, gs]).astype(BF16)
        st_ref[:, gs] = eaf_ref[tb - 1:tb, gs] * st + _dot(bg.T.astype(BF16), xw)

    @pl.when(c == pl.num_programs(1) - 1)
    def _():
        for k in range(st_ref.shape[1] // LANES):
            hs_ref[k * LANES:(k + 1) * LANES, :] = st_ref[:, k * LANES:(k + 1) * LANES].T
        ct_ref[...] = xbc_ref[tb - SUBLANES:tb, :]


def _sample_body(u_ref, v_ref, z_ref, xbc_ref, dt_ref, cst_ref, h0_ref,
                 gvn_ref, gw_ref, bs_ref, cw_ref, cb_ref, dtb_ref, alog_ref, dsk_ref, sn_ref,
                 ya_ref, yb_ref, vn_ref, hs_ref,
                 wm_ref, e_ref, xp_ref, xc_ref, dtf_ref, tef_ref, eaf_ref,
                 y_ref, yoff_ref, cg_ref, bg_ref, xwt_ref, eat_ref, cdh_ref, *, seq_len):
    tb = TOKEN_BLOCK
    k = pl.program_id(1)
    causal, last = _block_masks(seq_len)
    gw = yb_ref.shape[1] // SSD_GROUPS

    @pl.when((pl.program_id(0) == 0) & (k == 0))
    def _():
        _init_constants(wm_ref, e_ref, gw_ref, causal)
        xp_ref[0:SUBLANES, :] = jnp.zeros((SUBLANES, xp_ref.shape[1]), F32)

    @pl.when(k == 0)
    def _():
        _gating(u_ref, v_ref, gvn_ref, bs_ref, wm_ref, ya_ref, vn_ref)
        xp_ref[SUBLANES:SUBLANES + tb, :] = xbc_ref[...]
        _conv(xp_ref, cw_ref, cb_ref, xc_ref, cst_ref, seq_len)
        acum, acum_t = _ssd_prepare(dt_ref, dtb_ref, alog_ref, e_ref, causal, last, dtf_ref, tef_ref, eaf_ref)
        eat_ref[...] = jnp.exp(acum_t)
        for g in range(SSD_GROUPS):
            gs = slice(g * gw, (g + 1) * gw)
            yd, cg, bg = _ssd_diag_group(g, xc_ref, dtf_ref, acum, acum_t, causal)
            y_ref[:, gs] = yd
            cg_ref[:, g * D_STATE:(g + 1) * D_STATE] = cg.astype(F32)
            bg_ref[:, g * D_STATE:(g + 1) * D_STATE] = bg
            xw = xc_ref[:, gs] * tef_ref[:, gs]
            for q in range(gw // LANES):
                r0 = g * gw + q * LANES
                xwt_ref[r0:r0 + LANES, :] = xw[:, q * LANES:(q + 1) * LANES].T.astype(BF16)

    nseq = h0_ref.shape[0]
    nt = nseq * seq_len
    rows = pl.ds(pl.multiple_of(k * nt, nt), nt)
    rowseq = _iota2((tb, 1), 0) // seq_len
    subseq = _iota2((nt, 1), 0) // seq_len
    tok = _iota2((tb, LANES), 0)
    onehot = jnp.concatenate(
        [jnp.where(tok == (k * nseq + b) * seq_len + (seq_len - 1), 1.0, 0.0) for b in range(nseq)],
        axis=1).astype(BF16)
    cdh_ref[...] = _sel_right(eat_ref[...], onehot)
    hpg = gw // SSD_HEAD_DIM
    for g in range(SSD_GROUPS):
        gs = slice(g * gw, (g + 1) * gw)
        ds = slice(g * D_STATE, (g + 1) * D_STATE)
        c_sub = cg_ref[rows, ds]
        lhs = jnp.concatenate([jnp.where(subseq == b, c_sub, 0.0) for b in range(nseq)], axis=1).astype(BF16)
        h0cat = jnp.concatenate([h0_ref[b, gs, :].astype(BF16) for b in range(nseq)], axis=1)
        yoff_ref[rows, gs] = _dot_nt(lhs, h0cat)
        b_all = bg_ref[:, ds]
        bm = jnp.concatenate([jnp.where(rowseq == k * nseq + b, b_all, 0.0) for b in range(nseq)],
                             axis=1).astype(BF16)
        s_new = _dot(xwt_ref[gs, :], bm)
        for b in range(nseq):
            ls = slice(b * D_STATE, (b + 1) * D_STATE)
            for j in range(hpg):
                h = g * hpg + j
                r0 = g * gw + j * SSD_HEAD_DIM
                hs_ref[b, r0:r0 + SSD_HEAD_DIM, :] = (cdh_ref[h:h + 1, ls] * h0_ref[b, r0:r0 + SSD_HEAD_DIM, :]
                                                      + s_new[j * SSD_HEAD_DIM:(j + 1) * SSD_HEAD_DIM, ls])

    @pl.when(k == pl.num_programs(1) - 1)
    def _():
        for g in range(SSD_GROUPS):
            gs = slice(g * gw, (g + 1) * gw)
            y = y_ref[:, gs] + yoff_ref[:, gs] * eaf_ref[:, gs]
            _finish_group(g, y, xc_ref, z_ref, dsk_ref, sn_ref, yb_ref)


def _mixer_params(lp):
    (gm_v_norm, gm_ws, gm_bs, conv_w, conv_b, dt_bias, a_log, d_skip, ssd_norm) = lp
    nh = dt_bias.shape[0]
    pad = LANES - nh
    return dict(
        gvn=gm_v_norm.reshape(1, -1),
        cw=conv_w,
        cb=conv_b.reshape(1, -1),
        dtb=jnp.pad(dt_bias, (0, pad)).reshape(1, LANES),
        alog=jnp.pad(a_log, (0, pad)).reshape(1, LANES),
        dsk=jnp.repeat(d_skip, SSD_HEAD_DIM).reshape(1, -1),
        sn=ssd_norm.reshape(1, -1),
    )


def _full_spec(a):
    nd = a.ndim
    return pl.BlockSpec(a.shape, lambda i, j: (0,) * nd)


def _mixer_prompt(p1, xbc, dt, lp, n_seq, seq_len, gm_w, ssd_w, conv_dim):
    tb = TOKEN_BLOCK
    nc = seq_len // tb
    mp = _mixer_params(lp)
    gw_full = lp[1]
    bs_t = lp[2].T
    row = lambda b, c: b * nc + c
    params = [mp["gvn"], gw_full, bs_t, mp["cw"], mp["cb"], mp["dtb"], mp["alog"], mp["dsk"], mp["sn"]]
    return pl.pallas_call(
        _prompt_body,
        grid=(n_seq, nc),
        in_specs=[
            pl.BlockSpec((tb, gm_w), lambda b, c: (row(b, c), 0)),
            pl.BlockSpec((tb, gm_w), lambda b, c: (row(b, c), 1)),
            pl.BlockSpec((tb, ssd_w), lambda b, c: (row(b, c), 1)),
            pl.BlockSpec((tb, conv_dim), lambda b, c: (row(b, c), 0)),
            pl.BlockSpec((tb, LANES), lambda b, c: (row(b, c), 0)),
        ] + [_full_spec(a) for a in params],
        out_specs=[
            pl.BlockSpec((tb, gm_w), lambda b, c: (row(b, c), 0)),
            pl.BlockSpec((tb, ssd_w), lambda b, c: (row(b, c), 0)),
            pl.BlockSpec((None, ssd_w, D_STATE), lambda b, c: (b, 0, 0)),
            pl.BlockSpec((None, SUBLANES, conv_dim), lambda b, c: (b, 0, 0)),
        ],
        out_shape=[
            jax.ShapeDtypeStruct((n_seq * seq_len, gm_w), BF16),
            jax.ShapeDtypeStruct((n_seq * seq_len, ssd_w), BF16),
            jax.ShapeDtypeStruct((n_seq, ssd_w, D_STATE), F32),
            jax.ShapeDtypeStruct((n_seq, SUBLANES, conv_dim), F32),
        ],
        scratch_shapes=[
            pltpu.VMEM((GM_HEADS, tb, tb), BF16),
            pltpu.VMEM((LANES, ssd_w), BF16),
            pltpu.VMEM((SUBLANES + tb, conv_dim), F32),
            pltpu.VMEM((tb, conv_dim), F32),
            pltpu.VMEM((tb, ssd_w), F32),
            pltpu.VMEM((tb, ssd_w), F32),
            pltpu.VMEM((tb, ssd_w), F32),
            pltpu.VMEM((D_STATE, ssd_w), F32),
        ],
        compiler_params=_cparams("arbitrary", "arbitrary"),
        name="mixer_prompt",
    )(p1, p1, p1, xbc, dt, *params)


def _mixer_sample(p1, xbc, dt, lp, h0, conv_state, row0, n_seq, seq_len, gm_w, ssd_w, conv_dim):
    tb = TOKEN_BLOCK
    seqs_per_block = tb // seq_len
    nblk = n_seq // seqs_per_block
    nsub = seqs_per_block // SAMPLE_SEQS_PER_STEP
    blk0 = row0 // tb
    mp = _mixer_params(lp)
    reps = tb // seq_len
    gw_tiled = jnp.tile(lp[1][:, :seq_len, :seq_len], (1, reps, reps))
    bs_t = jnp.tile(lp[2][:, :seq_len], (1, reps)).T
    params = [mp["gvn"], gw_tiled, bs_t, mp["cw"], mp["cb"], mp["dtb"], mp["alog"], mp["dsk"], mp["sn"]]
    ntok = n_seq * seq_len
    return pl.pallas_call(
        functools.partial(_sample_body, seq_len=seq_len),
        grid=(nblk, nsub),
        in_specs=[
            pl.BlockSpec((tb, gm_w), lambda i, k: (blk0 + i, 0)),
            pl.BlockSpec((tb, gm_w), lambda i, k: (blk0 + i, 1)),
            pl.BlockSpec((tb, ssd_w), lambda i, k: (blk0 + i, 1)),
            pl.BlockSpec((tb, conv_dim), lambda i, k: (blk0 + i, 0)),
            pl.BlockSpec((tb, LANES), lambda i, k: (blk0 + i, 0)),
            pl.BlockSpec((seqs_per_block * (CONV_W - 1), conv_dim), lambda i, k: (i, 0)),
            pl.BlockSpec((SAMPLE_SEQS_PER_STEP, ssd_w, D_STATE), lambda i, k: (i * nsub + k, 0, 0)),
        ] + [_full_spec(a) for a in params],
        out_specs=[
            pl.BlockSpec((tb, gm_w), lambda i, k: (i, 0)),
            pl.BlockSpec((tb, ssd_w), lambda i, k: (i, 0)),
            pl.BlockSpec((tb, gm_w), lambda i, k: (i, 0)),
            pl.BlockSpec((SAMPLE_SEQS_PER_STEP, ssd_w, D_STATE), lambda i, k: (i * nsub + k, 0, 0)),
        ],
        out_shape=[
            jax.ShapeDtypeStruct((ntok, gm_w), BF16),
            jax.ShapeDtypeStruct((ntok, ssd_w), BF16),
            jax.ShapeDtypeStruct((ntok, gm_w), F32),
            jax.ShapeDtypeStruct((n_seq, ssd_w, D_STATE), F32),
        ],
        scratch_shapes=[
            pltpu.VMEM((GM_HEADS, tb, tb), BF16),
            pltpu.VMEM((LANES, ssd_w), BF16),
            pltpu.VMEM((SUBLANES + tb, conv_dim), F32),
            pltpu.VMEM((tb, conv_dim), F32),
            pltpu.VMEM((tb, ssd_w), F32),
            pltpu.VMEM((tb, ssd_w), F32),
            pltpu.VMEM((tb, ssd_w), F32),
            pltpu.VMEM((tb, ssd_w), F32),
            pltpu.VMEM((tb, ssd_w), F32),
            pltpu.VMEM((tb, SSD_GROUPS * D_STATE), F32),
            pltpu.VMEM((tb, SSD_GROUPS * D_STATE), F32),
            pltpu.VMEM((ssd_w, tb), BF16),
            pltpu.VMEM((LANES, tb), F32),
            pltpu.VMEM((LANES, SAMPLE_SEQS_PER_STEP * LANES), F32),
        ],
        compiler_params=_cparams("arbitrary", "arbitrary"),
        name="mixer_sample",
    )(p1, p1, p1, xbc, dt, conv_state, h0, *params)


def kernel(x_prompt, x_sample, state_ssm, state_conv, ffn1_norm, ffn1_w1, ffn1_w3, ffn1_w2, mix_norm, w_in,
           b_gate, conv_w, conv_b, dt_bias, a_log, d_skip, ssd_norm, gm_v_norm, gm_ws, gm_bs, w_proj_a,
           w_proj_b, w_out, ffn2_norm, ffn2_w1, ffn2_w3, ffn2_w2, final_norm):
    bp, tp, d = x_prompt.shape
    bs_, ts, _ = x_sample.shape
    depth = w_in.shape[0]
    n_heads = dt_bias.shape[1]
    gm_w = gm_v_norm.shape[1]
    ssd_w = ssd_norm.shape[1]
    conv_dim = conv_w.shape[2]
    np_, ns = bp * tp, bs_ * ts

    xs = [x_prompt.reshape(np_, d), x_sample.reshape(ns, d)]
    ssm_p, conv_p, ssm_s, conv_s, v_s = [], [], [], [], []
    for l in range(depth):
        c_uvz = 2 * gm_w + ssd_w
        x, xn = _ffn(xs, ffn1_norm[l], ffn1_w1[l], ffn1_w3[l], ffn1_w2[l], next_norm_w=mix_norm[l])
        p1, xbc, dt = _inproj(xn, w_in[l].T, c_uvz, conv_dim, n_heads)

        lp = (gm_v_norm[l], gm_ws[l], gm_bs[l], conv_w[l], conv_b[l], dt_bias[l], a_log[l], d_skip[l], ssd_norm[l])
        ya_p, yb_p, hs_p, ct_p = _mixer_prompt(p1, xbc, dt, lp, bp, tp, gm_w, ssd_w, conv_dim)

        st = state_conv[l]
        ya_s, yb_s, vn_s, hs_s = _mixer_sample(p1, xbc, dt, lp, state_ssm[l].reshape(bs_, ssd_w, D_STATE),
                                               st.reshape(bs_ * (CONV_W - 1), conv_dim),
                                               np_, bs_, ts, gm_w, ssd_w, conv_dim)

        x, xn = _merge(x, ya_p, yb_p, ya_s, yb_s, p1, b_gate[l], w_proj_a[l].astype(BF16), w_proj_b[l].astype(BF16),
                       w_out[l].astype(BF16), ffn2_norm[l])
        last = l == depth - 1
        xs = _ffn([x], ffn2_norm[l], ffn2_w1[l], ffn2_w3[l], ffn2_w2[l], xn=xn, final_w=final_norm if last else None,
                  split_rows=np_)

        xbc_s = xbc[np_:].reshape(bs_, ts, conv_dim)
        ssm_p.append(hs_p.reshape(bp, n_heads, SSD_HEAD_DIM, D_STATE))
        conv_p.append(ct_p[:, SUBLANES - (CONV_W - 1):, :])
        ssm_s.append(hs_s.reshape(bs_, n_heads, SSD_HEAD_DIM, D_STATE))
        conv_s.append(jnp.concatenate([st, xbc_s], axis=1)[:, ts:, :])
        v_s.append(vn_s.reshape(bs_, ts, gm_w))

    return (xs[0].reshape(bp, tp, d), xs[1].reshape(bs_, ts, d), jnp.stack(ssm_p), jnp.stack(conv_p),
            jnp.stack(ssm_s), jnp.stack(conv_s), jnp.stack(v_s))
```

```python
import functools

import jax
import jax.numpy as jnp
from jax import lax
from jax.experimental import pallas as pl
from jax.experimental.pallas import tpu as pltpu

F32 = jnp.float32
BF16 = jnp.bfloat16
EPS = 1e-6

LANES = 128
SUBLANES = 8
VMEM_LIMIT_BYTES = 60 * 1024 * 1024

GM_HEADS = 8
SSD_HEAD_DIM = 64
SSD_GROUPS = 4
D_STATE = 128
CONV_W = 4
TOKEN_BLOCK = 128
SAMPLE_SEQS_PER_STEP = 4

TOKEN_TILE = 1088
FFN_COLS = 512
PROJ_COLS = 1024
MERGE_TILE = 256
NORM_ROWS = 272
ROUND_ROWS = 64

_SINGLE = dict(pipeline_mode=pl.Buffered(1))


def _cparams(*sem):
    return pltpu.CompilerParams(dimension_semantics=sem, vmem_limit_bytes=VMEM_LIMIT_BYTES)


_GELU_C = 0.7978845608028654


def _gelu(x):
    t = jnp.tanh(x * (_GELU_C + (_GELU_C * 0.044715) * (x * x)))
    return x * (0.5 + 0.5 * t)


def _silu(x):
    return x * jax.nn.sigmoid(x)


def _softplus(x):
    return jnp.maximum(x, 0.0) + jnp.log1p(jnp.exp(-jnp.abs(x)))


def _split3(a):
    a1 = a.astype(BF16)
    r1 = a - a1.astype(F32)
    a2 = r1.astype(BF16)
    r2 = r1 - a2.astype(F32)
    return a1, a2, r2.astype(BF16)


def _dot(a, b):
    return jnp.dot(a, b, preferred_element_type=F32)


def _dot_nt(a, b):
    return lax.dot_general(a, b, (((1,), (1,)), ((), ())), preferred_element_type=F32)


def _sel_right(a, sel):
    a1, a2, a3 = _split3(a)
    return (_dot(a1, sel) + _dot(a2, sel)) + _dot(a3, sel)


def _sel_left(sel, a):
    a1, a2, a3 = _split3(a)
    return (_dot(sel, a1) + _dot(sel, a2)) + _dot(sel, a3)


def _row_loop(nrows, fn):
    def body(i, c):
        fn(pl.ds(pl.multiple_of(i * NORM_ROWS, NORM_ROWS), NORM_ROWS))
        return c

    lax.fori_loop(0, nrows // NORM_ROWS, body, 0)


def _rmsnorm_rows(x_ref, w_ref, out_ref, nrows):
    def one(sl):
        x = x_ref[sl, :]
        r = lax.rsqrt(jnp.mean(x * x, axis=-1, keepdims=True) + EPS)
        out_ref[sl, :] = ((x * r) * w_ref[...]).astype(out_ref.dtype)

    _row_loop(nrows, one)


def _tile_branches(n_full, tile, tail_rows, run, axis=0):
    i = pl.program_id(axis)

    @pl.when(i < n_full)
    def _():
        run(False, tile)

    if tail_rows:
        @pl.when(i >= n_full)
        def _():
            run(True, tail_rows)


def _tile_dma(action, t, *, hbm_refs, buf, sem, to_hbm, rows_first):
    tm = buf.shape[0]

    def go(pieces):
        for hbm, h0, b0, nr in pieces:
            h = hbm.at[pl.ds(h0, nr), :]
            v = buf.at[pl.ds(b0, nr), :]
            copy = pltpu.make_async_copy(v, h, sem) if to_hbm else pltpu.make_async_copy(h, v, sem)
            getattr(copy, action)()

    if len(hbm_refs) == 1:
        go([(hbm_refs[0], pl.multiple_of(t * tm, SUBLANES), 0, tm)])
        return
    n_pure, head = divmod(rows_first, tm)

    @pl.when(t < n_pure)
    def _():
        go([(hbm_refs[0], pl.multiple_of(t * tm, SUBLANES), 0, tm)])

    @pl.when(t >= n_pure)
    def _():
        go([(hbm_refs[0], n_pure * tm, 0, head), (hbm_refs[1], 0, head, tm - head)])


def _ffn_body(*refs, n_in, n_out, final_norm, emit_norm, has_xn, n_tiles, rows_first_in, rows_first_out):
    refs = list(refs)
    x_refs = [refs.pop(0) for _ in range(n_in)]
    xni_ref = refs.pop(0) if has_xn else None
    nw_ref, w1_ref, w3_ref, w2_ref, fn_ref = [refs.pop(0) for _ in range(5)]
    o_refs = [refs.pop(0) for _ in range(n_out)]
    xno_ref = refs.pop(0) if emit_norm else None
    acc_ref, xn_ref, in_sem, out_sem = refs
    i = pl.program_id(0)
    j = pl.program_id(1)
    nj = pl.num_programs(1)
    tm = acc_ref.shape[1]
    slot = i % 2
    other = 1 - slot
    acc = acc_ref.at[slot]
    dma_in = functools.partial(_tile_dma, hbm_refs=x_refs, to_hbm=False, rows_first=rows_first_in)
    dma_out = functools.partial(_tile_dma, hbm_refs=o_refs, to_hbm=True, rows_first=rows_first_out)

    @pl.when(j == 0)
    def _():
        @pl.when(i == 0)
        def _():
            dma_in("start", i, buf=acc, sem=in_sem.at[slot])

        dma_in("wait", i, buf=acc, sem=in_sem.at[slot])
        if not has_xn:
            _rmsnorm_rows(acc, nw_ref, xn_ref, tm)

    xn = (xni_ref if has_xn else xn_ref)[...]
    h1 = _dot(xn, w1_ref[...].astype(BF16))
    h3 = _dot(xn, w3_ref[...].astype(BF16))
    g = ((0.5 * _silu(h1)) * h3).astype(BF16)
    acc[...] += _dot(g, w2_ref[...].astype(BF16))

    @pl.when((j == nj - 2) & (i + 1 < n_tiles))
    def _():
        @pl.when(i >= 1)
        def _():
            dma_out("wait", i - 1, buf=acc_ref.at[other], sem=out_sem.at[other])

        dma_in("start", i + 1, buf=acc_ref.at[other], sem=in_sem.at[other])

    @pl.when(j == nj - 1)
    def _():
        if final_norm:
            _rmsnorm_rows(acc, fn_ref, acc, tm)
        if emit_norm:
            _rmsnorm_rows(acc, fn_ref, xno_ref, tm)
        dma_out("start", i, buf=acc, sem=out_sem.at[slot])

        @pl.when(i == n_tiles - 1)
        def _():
            if n_tiles > 1:
                dma_out("wait", i - 1, buf=acc_ref.at[other], sem=out_sem.at[other])
            dma_out("wait", i, buf=acc, sem=out_sem.at[slot])


def _ffn(xs, norm_w, w1, w3, w2, *, xn=None, final_w=None, next_norm_w=None, split_rows=None):
    d, f = w1.shape
    tm, tf = TOKEN_TILE, FFN_COLS
    n = sum(x.shape[0] for x in xs)
    assert n % tm == 0 and f % tf == 0 and f // tf >= 2
    n_tiles = n // tm
    for first in ([xs[0].shape[0]] if len(xs) == 2 else []) + ([split_rows] if split_rows else []):
        assert first % tm and (first // tm + 1) * tm == n
    final_norm = final_w is not None
    emit_norm = next_norm_w is not None
    assert not (final_norm and emit_norm)
    fw = final_w if final_norm else (next_norm_w if emit_norm else norm_w)
    has_xn = xn is not None
    any_spec = pl.BlockSpec(memory_space=pl.ANY)
    in_specs = [any_spec for _ in xs] + ([pl.BlockSpec((tm, d), lambda i, j: (i, 0))] if has_xn else [])
    if split_rows:
        o_specs = [any_spec, any_spec]
        o_shapes = [jax.ShapeDtypeStruct((split_rows, d), F32), jax.ShapeDtypeStruct((n - split_rows, d), F32)]
    else:
        o_specs = [any_spec]
        o_shapes = [jax.ShapeDtypeStruct((n, d), F32)]
    n_out = len(o_specs)
    if emit_norm:
        o_specs.append(pl.BlockSpec((tm, d), lambda i, j: (i, 0), **_SINGLE))
        o_shapes.append(jax.ShapeDtypeStruct((n, d), BF16))
    return pl.pallas_call(
        functools.partial(_ffn_body, n_in=len(xs), n_out=n_out, final_norm=final_norm, emit_norm=emit_norm,
                          has_xn=has_xn, n_tiles=n_tiles, rows_first_in=xs[0].shape[0], rows_first_out=split_rows),
        grid=(n_tiles, f // tf),
        in_specs=in_specs + [
            pl.BlockSpec((1, d), lambda i, j: (0, 0)),
            pl.BlockSpec((d, tf), lambda i, j: (0, j)),
            pl.BlockSpec((d, tf), lambda i, j: (0, j)),
            pl.BlockSpec((tf, d), lambda i, j: (j, 0)),
            pl.BlockSpec((1, d), lambda i, j: (0, 0)),
        ],
        out_specs=o_specs,
        out_shape=o_shapes,
        scratch_shapes=[
            pltpu.VMEM((2, tm, d), F32),
            pltpu.VMEM((SUBLANES, LANES) if has_xn else (tm, d), BF16),
            pltpu.SemaphoreType.DMA((2,)),
            pltpu.SemaphoreType.DMA((2,)),
        ],
        compiler_params=_cparams("arbitrary", "arbitrary"),
        name="ffn",
    )(*xs, *([xn] if has_xn else []), norm_w.reshape(1, d), w1, w3, w2, fw.reshape(1, d))


def _inproj_body(*refs, n_full, tail_rows, n_uvz, n_xbc, n_dt, n_tiles, rounds):
    nr = len(rounds)
    xn_ref, w_ref, wdt_ref = refs[:3]
    rin_refs = refs[3:3 + nr]
    p1_ref, xbc_ref, dt_ref = refs[3 + nr:6 + nr]
    rout_refs = refs[6 + nr:6 + 2 * nr]
    wb_ref = refs[6 + 2 * nr]
    j = pl.program_id(0)

    step = j * n_tiles + pl.program_id(1)
    for (s0, nblk), rin, rout in zip(rounds, rin_refs, rout_refs):
        @pl.when((step >= s0) & (step < s0 + nblk))
        def _(rin=rin, rout=rout):
            rout[...] = rin[...].astype(rout.dtype)

    is_xbc = (j >= n_uvz) & (j < n_uvz + n_xbc)

    @pl.when(pl.program_id(1) == 0)
    def _():
        wb_ref[...] = w_ref[...].astype(BF16)

    def run(is_tail, rows):
        @pl.when(j == 0)
        def _():
            dt = _dot_nt(xn_ref[0:rows, :], wdt_ref[...].astype(BF16))
            lane = _iota2(dt.shape, 1)
            dt_ref[0:rows, :] = jnp.where(lane < n_dt, dt, 0.0)

        @pl.when(jnp.logical_not(is_xbc))
        def _():
            p1_ref[0:rows, :] = _dot_nt(xn_ref[0:rows, :], wb_ref[...]).astype(p1_ref.dtype)

        @pl.when(is_xbc)
        def _():
            xbc_ref[0:rows, :] = _dot_nt(xn_ref[0:rows, :], wb_ref[...])

    _tile_branches(n_full, TOKEN_TILE, tail_rows, run, axis=1)


def _inproj(xn, w_t, c_uvz, c_xbc, c_dt, round_ws=()):
    n, d = xn.shape
    tm, tn = TOKEN_TILE, PROJ_COLS
    n_full, tail_rows = divmod(n, tm)
    n_tiles = n_full + (1 if tail_rows else 0)
    last_i = n_tiles - 1
    c_gate0 = c_uvz + c_xbc + c_dt
    c_gates = w_t.shape[0] - c_gate0
    n_uvz, n_xbc, n_g = c_uvz // tn, c_xbc // tn, c_gates // tn
    n_main = n_uvz + n_xbc
    assert c_gate0 % SUBLANES == 0 and (c_uvz + c_xbc) % SUBLANES == 0

    def w_row(j, i):
        return (pl.multiple_of(jnp.where(j < n_main, j * tn, c_gate0 + (j - n_main) * tn), SUBLANES), 0)

    def p1_idx(j, i):
        writes = (j < n_uvz) | (j >= n_main)
        col = jnp.where(j < n_uvz, j, jnp.maximum(j - n_xbc, n_uvz - 1))
        return (jnp.where(writes, i, last_i), col)

    def xbc_idx(j, i):
        row = jnp.where(j < n_uvz, 0, jnp.where(j < n_main, i, last_i))
        return (row, jnp.clip(j - n_uvz, 0, n_xbc - 1))

    rounds, s0 = [], 0
    for w in round_ws:
        assert w.shape[0] % ROUND_ROWS == 0
        rounds.append((s0, w.shape[0] // ROUND_ROWS))
        s0 += w.shape[0] // ROUND_ROWS
    assert s0 <= (n_main + n_g) * n_tiles

    def r_specs():
        return [pl.BlockSpec((ROUND_ROWS, w.shape[1]),
                             lambda j, i, s0=s0, nblk=nblk: (jnp.clip(j * n_tiles + i - s0, 0, nblk - 1), 0))
                for w, (s0, nblk) in zip(round_ws, rounds)]

    return pl.pallas_call(
        functools.partial(_inproj_body, n_full=n_full, tail_rows=tail_rows, n_uvz=n_uvz, n_xbc=n_xbc, n_dt=c_dt,
                          n_tiles=n_tiles, rounds=tuple(rounds)),
        grid=(n_main + n_g, n_tiles),
        in_specs=[
            pl.BlockSpec((tm, d), lambda j, i: (i, 0)),
            pl.BlockSpec((pl.Element(tn), pl.Element(d)), w_row),
            pl.BlockSpec((pl.Element(LANES), pl.Element(d)), lambda j, i: (c_uvz + c_xbc, 0)),
        ] + r_specs(),
        out_specs=[
            pl.BlockSpec((tm, tn), p1_idx),
            pl.BlockSpec((tm, tn), xbc_idx),
            pl.BlockSpec((tm, LANES), lambda j, i: (jnp.where(j == 0, i, last_i), 0)),
        ] + r_specs(),
        out_shape=[
            jax.ShapeDtypeStruct((n, c_uvz + c_gates), BF16),
            jax.ShapeDtypeStruct((n, c_xbc), F32),
            jax.ShapeDtypeStruct((n, LANES), F32),
        ] + [jax.ShapeDtypeStruct(w.shape, BF16) for w in round_ws],
        scratch_shapes=[pltpu.VMEM((tn, d), BF16)],
        compiler_params=_cparams("arbitrary", "arbitrary"),
        name="in_proj",
    )(xn, w_t, w_t, *round_ws)


def _merge_body(x_ref, yap_ref, ybp_ref, yas_ref, ybs_ref, ga_ref, gb_ref, bg_ref, wa_ref, wb_ref, wo_ref, nw_ref,
                o_ref, xn_ref, *, n_prompt_tiles):
    def run(ya_ref, yb_ref):
        pa = _dot(ya_ref[...], wa_ref[...])
        pb = _dot(yb_ref[...], wb_ref[...])
        ga = jax.nn.sigmoid(ga_ref[...].astype(F32) + bg_ref[0:1, :])
        gb = jax.nn.sigmoid(gb_ref[...].astype(F32) + bg_ref[1:2, :])
        m = (ga * pa + gb * pb).astype(BF16)
        o = x_ref[...] + _dot(m, wo_ref[...])
        o_ref[...] = o
        r = lax.rsqrt(jnp.mean(o * o, axis=-1, keepdims=True) + EPS)
        xn_ref[...] = ((o * r) * nw_ref[...]).astype(xn_ref.dtype)

    i = pl.program_id(0)

    @pl.when(i < n_prompt_tiles)
    def _():
        run(yap_ref, ybp_ref)

    @pl.when(i >= n_prompt_tiles)
    def _():
        run(yas_ref, ybs_ref)


def _merge(x, ya_p, yb_p, ya_s, yb_s, p1, b_gate, wa, wb, wo, next_norm_w):
    n, d = x.shape
    tm = MERGE_TILE
    npt = ya_p.shape[0] // tm
    nst = ya_s.shape[0] // tm
    gcol = p1.shape[1] // d - 2
    p_idx = lambda i: (jnp.minimum(i, npt - 1), 0)
    s_idx = lambda i: (jnp.clip(i - npt, 0, nst - 1), 0)
    return pl.pallas_call(
        functools.partial(_merge_body, n_prompt_tiles=npt),
        grid=(n // tm,),
        in_specs=[
            pl.BlockSpec((tm, d), lambda i: (i, 0)),
            pl.BlockSpec((tm, ya_p.shape[1]), p_idx),
            pl.BlockSpec((tm, yb_p.shape[1]), p_idx),
            pl.BlockSpec((tm, ya_s.shape[1]), s_idx),
            pl.BlockSpec((tm, yb_s.shape[1]), s_idx),
            pl.BlockSpec((tm, d), lambda i: (i, gcol)),
            pl.BlockSpec((tm, d), lambda i: (i, gcol + 1)),
            pl.BlockSpec((2, d), lambda i: (0, 0)),
            pl.BlockSpec(wa.shape, lambda i: (0, 0), **_SINGLE),
            pl.BlockSpec(wb.shape, lambda i: (0, 0), **_SINGLE),
            pl.BlockSpec(wo.shape, lambda i: (0, 0), **_SINGLE),
            pl.BlockSpec((1, d), lambda i: (0, 0)),
        ],
        out_specs=[pl.BlockSpec((tm, d), lambda i: (i, 0)), pl.BlockSpec((tm, d), lambda i: (i, 0))],
        out_shape=[jax.ShapeDtypeStruct((n, d), F32), jax.ShapeDtypeStruct((n, d), BF16)],
        compiler_params=_cparams("arbitrary"),
        name="merge",
    )(x, ya_p, yb_p, ya_s, yb_s, p1, p1, b_gate, wa, wb, wo, next_norm_w.reshape(1, d))


def _iota2(shape, dim):
    return lax.broadcasted_iota(jnp.int32, shape, dim)


def _block_masks(seq_len):
    tb = TOKEN_BLOCK
    row = _iota2((tb, tb), 0)
    col = _iota2((tb, tb), 1)
    if seq_len >= tb:
        causal = col <= row
        last = col == tb - 1
    else:
        causal = (col <= row) & ((row // seq_len) == (col // seq_len))
        last = col == (row // seq_len) * seq_len + (seq_len - 1)
    return causal, last


def _init_constants(wm_ref, e_ref, gw_ref, causal):
    for h in range(GM_HEADS):
        wm_ref[h] = jnp.where(causal, gw_ref[h], 0.0).astype(BF16)
    hrow = _iota2(e_ref.shape, 0)
    ccol = _iota2(e_ref.shape, 1)
    e_ref[...] = jnp.where(hrow == ccol // SSD_HEAD_DIM, 1.0, 0.0).astype(BF16)


def _gating(u_ref, v_ref, gvn_ref, bs_ref, wm_ref, ya_ref, vn_ref):
    vg = _gelu(v_ref[...].astype(F32))
    r = lax.rsqrt(jnp.mean(vg * vg, axis=-1, keepdims=True) + EPS)
    vn = (vg * r) * gvn_ref[...]
    if vn_ref is not None:
        vn_ref[...] = vn
    vnb = vn.astype(BF16)
    for h in range(GM_HEADS):
        cs = slice(h * LANES, (h + 1) * LANES)
        s = _dot(wm_ref[h], vnb[:, cs]) + bs_ref[:, h:h + 1]
        ya_ref[:, cs] = (_gelu(u_ref[:, cs].astype(F32)) * s).astype(ya_ref.dtype)


def _conv(xp_ref, cw_ref, cb_ref, xc_ref, cst_ref, seq_len):
    tb = TOKEN_BLOCK
    width = xc_ref.shape[1]
    cw = 512
    ns = CONV_W - 1
    place = None
    if cst_ref is not None:
        nseq = cst_ref.shape[1]
        tpos = _iota2((tb, 1), 0) % seq_len
        row = _iota2((tb, nseq), 0)
        col = _iota2((tb, nseq), 1)
        place = {(d, r): jnp.where((col == row // seq_len) & (row % seq_len == r + d - ns), 1.0, 0.0).astype(BF16)
                 for d in range(1, CONV_W) for r in range(ns - d, ns)}
    for c0 in range(0, width, cw):
        cs = slice(c0, c0 + cw)
        acc = None
        full = xp_ref[:, cs]
        if place is not None:
            pieces = [_split3(cst_ref[r, :, cs]) for r in range(ns)]
        for k in range(CONV_W):
            d = CONV_W - 1 - k
            xs = (pltpu.roll(full, d, axis=0) if d else full)[SUBLANES:SUBLANES + tb, :]
            if place is not None and d > 0:
                xs = jnp.where(tpos >= d, xs, 0.0)
                for r in range(ns - d, ns):
                    pd = place[(d, r)]
                    xs = xs + ((_dot(pd, pieces[r][0]) + _dot(pd, pieces[r][1])) + _dot(pd, pieces[r][2]))
            term = cw_ref[k:k + 1, cs] * xs
            acc = term if acc is None else acc + term
        xc_ref[:, cs] = _silu(cb_ref[:, cs] + acc)


def _ssd_prepare(dt_ref, dtb_ref, alog_ref, e_ref, causal, last, dtf_ref, tef_ref, eaf_ref):
    dt = _softplus(dt_ref[...] + dtb_ref[...])
    a = -jnp.exp(alog_ref[...])
    da = dt * a
    lmat = jnp.where(causal, 1.0, 0.0).astype(BF16)
    acum = _sel_left(lmat, da)
    a_last = _sel_left(jnp.where(last, 1.0, 0.0).astype(BF16), acum)
    ea = jnp.exp(acum)
    e = e_ref[...]
    dtf_ref[...] = _sel_right(dt, e)
    tef_ref[...] = _sel_right(dt * jnp.exp(a_last - acum), e)
    eaf_ref[...] = _sel_right(ea, e)
    return acum, acum.T


def _ssd_diag_group(g, xc_ref, dtf_ref, acum, acum_t, causal):
    tb = TOKEN_BLOCK
    width = xc_ref.shape[1] - 2 * SSD_GROUPS * D_STATE
    gw = width // SSD_GROUPS
    bg = xc_ref[:, width + g * D_STATE:width + (g + 1) * D_STATE]
    cg = xc_ref[:, width + (SSD_GROUPS + g) * D_STATE:width + (SSD_GROUPS + g + 1) * D_STATE].astype(BF16)
    cb = jnp.where(causal, _dot_nt(cg, bg.astype(BF16)), 0.0)
    lane = _iota2((tb, LANES), 1)
    heads_per_group = gw // SSD_HEAD_DIM
    outs = []
    for jp in range(heads_per_group // 2):
        ms = []
        for hh in range(2):
            h = g * heads_per_group + 2 * jp + hh
            seg = jnp.minimum(acum[:, h:h + 1] - acum_t[h:h + 1, :], 0.0)
            ms.append((cb * jnp.exp(seg)).astype(BF16))
        lhs = jnp.concatenate(ms, axis=1)
        ps = slice(g * gw + jp * LANES, g * gw + (jp + 1) * LANES)
        xpair = (xc_ref[:, ps] * dtf_ref[:, ps]).astype(BF16)
        zero = jnp.zeros_like(xpair)
        rhs = jnp.concatenate([jnp.where(lane < SSD_HEAD_DIM, xpair, zero),
                               jnp.where(lane >= SSD_HEAD_DIM, xpair, zero)], axis=0)
        outs.append(_dot(lhs, rhs))
    return jnp.concatenate(outs, axis=1), cg, bg


def _finish_group(g, y, xc_ref, z_ref, dsk_ref, sn_ref, yb_ref):
    gw = yb_ref.shape[1] // SSD_GROUPS
    gs = slice(g * gw, (g + 1) * gw)
    y = y + dsk_ref[:, gs] * xc_ref[:, gs]
    yz = y * _silu(z_ref[:, gs].astype(F32))
    r = lax.rsqrt(jnp.mean(yz * yz, axis=-1, keepdims=True) + EPS)
    yb_ref[:, gs] = ((yz * r) * sn_ref[:, gs]).astype(yb_ref.dtype)


def _prompt_body(u_ref, v_ref, z_ref, xbc_ref, dt_ref, gvn_ref, gw_ref, bs_ref, cw_ref, cb_ref,
                 dtb_ref, alog_ref, dsk_ref, sn_ref,
                 ya_ref, yb_ref, hs_ref, ct_ref,
                 wm_ref, e_ref, xp_ref, xc_ref, dtf_ref, tef_ref, eaf_ref, st_ref):
    tb = TOKEN_BLOCK
    c = pl.program_id(1)
    causal, last = _block_masks(tb)

    @pl.when((pl.program_id(0) == 0) & (c == 0))
    def _():
        _init_constants(wm_ref, e_ref, gw_ref, causal)

    @pl.when(c == 0)
    def _():
        st_ref[...] = jnp.zeros_like(st_ref)
        xp_ref[0:SUBLANES, :] = jnp.zeros((SUBLANES, xp_ref.shape[1]), F32)

    _gating(u_ref, v_ref, gvn_ref, bs_ref, wm_ref, ya_ref, None)

    xp_ref[SUBLANES:SUBLANES + tb, :] = xbc_ref[...]
    _conv(xp_ref, cw_ref, cb_ref, xc_ref, None, tb)
    xp_ref[0:SUBLANES, :] = xbc_ref[tb - SUBLANES:tb, :]

    acum, acum_t = _ssd_prepare(dt_ref, dtb_ref, alog_ref, e_ref, causal, last, dtf_ref, tef_ref, eaf_ref)

    gw = yb_ref.shape[1] // SSD_GROUPS
    for g in range(SSD_GROUPS):
        gs = slice(g * gw, (g + 1) * gw)
        yd, cg, bg = _ssd_diag_group(g, xc_ref, dtf_ref, acum, acum_t, causal)
        st = st_ref[:, gs]
        y = yd + _dot(cg, st.astype(BF16)) * eaf_ref[:, gs]
        _finish_group(g, y, xc_ref, z_ref, dsk_ref, sn_ref, yb_ref)
        xw = (xc_ref[:, gs] * tef_ref[:, gs]).astype(BF16)
        st_ref[:, gs] = eaf_ref[tb - 1:tb, gs] * st + _dot(bg.T.astype(BF16), xw)

    @pl.when(c == pl.num_programs(1) - 1)
    def _():
        for k in range(st_ref.shape[1] // LANES):
            hs_ref[k * LANES:(k + 1) * LANES, :] = st_ref[:, k * LANES:(k + 1) * LANES].T
        ct_ref[...] = xbc_ref[tb - SUBLANES:tb, :]


def _sample_body(u_ref, v_ref, z_ref, xbc_ref, dt_ref, cst_ref, h0_ref,
                 gvn_ref, gw_ref, bs_ref, cw_ref, cb_ref, dtb_ref, alog_ref, dsk_ref, sn_ref,
                 ya_ref, yb_ref, vn_ref, hs_ref, cs_ref,
                 wm_ref, e_ref, xp_ref, xc_ref, dtf_ref, tef_ref, eaf_ref,
                 y_ref, yoff_ref, cg_ref, bg_ref, xwt_ref, eat_ref, cdh_ref, *, seq_len):
    tb = TOKEN_BLOCK
    k = pl.program_id(1)
    causal, last = _block_masks(seq_len)
    gw = yb_ref.shape[1] // SSD_GROUPS

    @pl.when((pl.program_id(0) == 0) & (k == 0))
    def _():
        _init_constants(wm_ref, e_ref, gw_ref, causal)
        xp_ref[0:SUBLANES, :] = jnp.zeros((SUBLANES, xp_ref.shape[1]), F32)

    @pl.when(k == 0)
    def _():
        _gating(u_ref, v_ref, gvn_ref, bs_ref, wm_ref, ya_ref, vn_ref)
        xp_ref[SUBLANES:SUBLANES + tb, :] = xbc_ref[...]
        _conv(xp_ref, cw_ref, cb_ref, xc_ref, cst_ref, seq_len)
        nsb = tb // seq_len
        pick = [jnp.where(_iota2((nsb, tb), 1) == _iota2((nsb, tb), 0) * seq_len + (seq_len - (CONV_W - 1) + r),
                          1.0, 0.0).astype(BF16) for r in range(CONV_W - 1)]
        for c0 in range(0, xbc_ref.shape[1], 512):
            pieces = _split3(xbc_ref[:, c0:c0 + 512])
            for r in range(CONV_W - 1):
                cs_ref[r, :, c0:c0 + 512] = ((_dot(pick[r], pieces[0]) + _dot(pick[r], pieces[1]))
                                             + _dot(pick[r], pieces[2]))
        acum, acum_t = _ssd_prepare(dt_ref, dtb_ref, alog_ref, e_ref, causal, last, dtf_ref, tef_ref, eaf_ref)
        eat_ref[...] = jnp.exp(acum_t)
        for g in range(SSD_GROUPS):
            gs = slice(g * gw, (g + 1) * gw)
            yd, cg, bg = _ssd_diag_group(g, xc_ref, dtf_ref, acum, acum_t, causal)
            y_ref[:, gs] = yd
            cg_ref[:, g * D_STATE:(g + 1) * D_STATE] = cg.astype(F32)
            bg_ref[:, g * D_STATE:(g + 1) * D_STATE] = bg
            xw = xc_ref[:, gs] * tef_ref[:, gs]
            for q in range(gw // LANES):
                r0 = g * gw + q * LANES
                xwt_ref[r0:r0 + LANES, :] = xw[:, q * LANES:(q + 1) * LANES].T.astype(BF16)

    nseq = h0_ref.shape[0]
    nt = nseq * seq_len
    rows = pl.ds(pl.multiple_of(k * nt, nt), nt)
    rowseq = _iota2((tb, 1), 0) // seq_len
    subseq = _iota2((nt, 1), 0) // seq_len
    tok = _iota2((tb, LANES), 0)
    onehot = jnp.concatenate(
        [jnp.where(tok == (k * nseq + b) * seq_len + (seq_len - 1), 1.0, 0.0) for b in range(nseq)],
        axis=1).astype(BF16)
    cdh_ref[...] = _sel_right(eat_ref[...], onehot)
    hpg = gw // SSD_HEAD_DIM
    for g in range(SSD_GROUPS):
        gs = slice(g * gw, (g + 1) * gw)
        ds = slice(g * D_STATE, (g + 1) * D_STATE)
        c_sub = cg_ref[rows, ds]
        lhs = jnp.concatenate([jnp.where(subseq == b, c_sub, 0.0) for b in range(nseq)], axis=1).astype(BF16)
        h0cat = jnp.concatenate([h0_ref[b, gs, :].astype(BF16) for b in range(nseq)], axis=1)
        yoff_ref[rows, gs] = _dot_nt(lhs, h0cat)
        b_all = bg_ref[:, ds]
        bm = jnp.concatenate([jnp.where(rowseq == k * nseq + b, b_all, 0.0) for b in range(nseq)],
                             axis=1).astype(BF16)
        s_new = _dot(xwt_ref[gs, :], bm)
        for b in range(nseq):
            ls = slice(b * D_STATE, (b + 1) * D_STATE)
            for j in range(hpg):
                h = g * hpg + j
                r0 = g * gw + j * SSD_HEAD_DIM
                hs_ref[b, r0:r0 + SSD_HEAD_DIM, :] = (cdh_ref[h:h + 1, ls] * h0_ref[b, r0:r0 + SSD_HEAD_DIM, :]
                                                      + s_new[j * SSD_HEAD_DIM:(j + 1) * SSD_HEAD_DIM, ls])

    @pl.when(k == pl.num_programs(1) - 1)
    def _():
        for g in range(SSD_GROUPS):
            gs = slice(g * gw, (g + 1) * gw)
            y = y_ref[:, gs] + yoff_ref[:, gs] * eaf_ref[:, gs]
            _finish_group(g, y, xc_ref, z_ref, dsk_ref, sn_ref, yb_ref)


def _mixer_params(lp):
    (gm_v_norm, gm_ws, gm_bs, conv_w, conv_b, dt_bias, a_log, d_skip, ssd_norm) = lp
    nh = dt_bias.shape[0]
    pad = LANES - nh
    return dict(
        gvn=gm_v_norm.reshape(1, -1),
        cw=conv_w,
        cb=conv_b.reshape(1, -1),
        dtb=jnp.pad(dt_bias, (0, pad)).reshape(1, LANES),
        alog=jnp.pad(a_log, (0, pad)).reshape(1, LANES),
        dsk=jnp.repeat(d_skip, SSD_HEAD_DIM).reshape(1, -1),
        sn=ssd_norm.reshape(1, -1),
    )


def _full_spec(a):
    nd = a.ndim
    return pl.BlockSpec(a.shape, lambda i, j: (0,) * nd)


def _mixer_prompt(p1, xbc, dt, lp, n_seq, seq_len, gm_w, ssd_w, conv_dim):
    tb = TOKEN_BLOCK
    nc = seq_len // tb
    mp = _mixer_params(lp)
    gw_full = lp[1]
    bs_t = lp[2].T
    row = lambda b, c: b * nc + c
    params = [mp["gvn"], gw_full, bs_t, mp["cw"], mp["cb"], mp["dtb"], mp["alog"], mp["dsk"], mp["sn"]]
    return pl.pallas_call(
        _prompt_body,
        grid=(n_seq, nc),
        in_specs=[
            pl.BlockSpec((tb, gm_w), lambda b, c: (row(b, c), 0)),
            pl.BlockSpec((tb, gm_w), lambda b, c: (row(b, c), 1)),
            pl.BlockSpec((tb, ssd_w), lambda b, c: (row(b, c), 1)),
            pl.BlockSpec((tb, conv_dim), lambda b, c: (row(b, c), 0)),
            pl.BlockSpec((tb, LANES), lambda b, c: (row(b, c), 0)),
        ] + [_full_spec(a) for a in params],
        out_specs=[
            pl.BlockSpec((tb, gm_w), lambda b, c: (row(b, c), 0)),
            pl.BlockSpec((tb, ssd_w), lambda b, c: (row(b, c), 0)),
            pl.BlockSpec((None, ssd_w, D_STATE), lambda b, c: (b, 0, 0)),
            pl.BlockSpec((None, SUBLANES, conv_dim), lambda b, c: (b, 0, 0)),
        ],
        out_shape=[
            jax.ShapeDtypeStruct((n_seq * seq_len, gm_w), BF16),
            jax.ShapeDtypeStruct((n_seq * seq_len, ssd_w), BF16),
            jax.ShapeDtypeStruct((n_seq, ssd_w, D_STATE), F32),
            jax.ShapeDtypeStruct((n_seq, SUBLANES, conv_dim), F32),
        ],
        scratch_shapes=[
            pltpu.VMEM((GM_HEADS, tb, tb), BF16),
            pltpu.VMEM((LANES, ssd_w), BF16),
            pltpu.VMEM((SUBLANES + tb, conv_dim), F32),
            pltpu.VMEM((tb, conv_dim), F32),
            pltpu.VMEM((tb, ssd_w), F32),
            pltpu.VMEM((tb, ssd_w), F32),
            pltpu.VMEM((tb, ssd_w), F32),
            pltpu.VMEM((D_STATE, ssd_w), F32),
        ],
        compiler_params=_cparams("arbitrary", "arbitrary"),
        name="mixer_prompt",
    )(p1, p1, p1, xbc, dt, *params)


def _mixer_sample(p1, xbc, dt, lp, h0, conv_state, row0, n_seq, seq_len, gm_w, ssd_w, conv_dim):
    tb = TOKEN_BLOCK
    seqs_per_block = tb // seq_len
    nblk = n_seq // seqs_per_block
    nsub = seqs_per_block // SAMPLE_SEQS_PER_STEP
    blk0 = row0 // tb
    mp = _mixer_params(lp)
    reps = tb // seq_len
    gw_tiled = jnp.tile(lp[1][:, :seq_len, :seq_len], (1, reps, reps))
    bs_t = jnp.tile(lp[2][:, :seq_len], (1, reps)).T
    params = [mp["gvn"], gw_tiled, bs_t, mp["cw"], mp["cb"], mp["dtb"], mp["alog"], mp["dsk"], mp["sn"]]
    ntok = n_seq * seq_len
    return pl.pallas_call(
        functools.partial(_sample_body, seq_len=seq_len),
        grid=(nblk, nsub),
        in_specs=[
            pl.BlockSpec((tb, gm_w), lambda i, k: (blk0 + i, 0)),
            pl.BlockSpec((tb, gm_w), lambda i, k: (blk0 + i, 1)),
            pl.BlockSpec((tb, ssd_w), lambda i, k: (blk0 + i, 1)),
            pl.BlockSpec((tb, conv_dim), lambda i, k: (blk0 + i, 0)),
            pl.BlockSpec((tb, LANES), lambda i, k: (blk0 + i, 0)),
            pl.BlockSpec((CONV_W - 1, seqs_per_block, conv_dim), lambda i, k: (0, i, 0)),
            pl.BlockSpec((SAMPLE_SEQS_PER_STEP, ssd_w, D_STATE), lambda i, k: (i * nsub + k, 0, 0)),
        ] + [_full_spec(a) for a in params],
        out_specs=[
            pl.BlockSpec((tb, gm_w), lambda i, k: (i, 0)),
            pl.BlockSpec((tb, ssd_w), lambda i, k: (i, 0)),
            pl.BlockSpec((tb, gm_w), lambda i, k: (i, 0)),
            pl.BlockSpec((SAMPLE_SEQS_PER_STEP, ssd_w, D_STATE), lambda i, k: (i * nsub + k, 0, 0)),
            pl.BlockSpec((CONV_W - 1, seqs_per_block, conv_dim), lambda i, k: (0, i, 0)),
        ],
        out_shape=[
            jax.ShapeDtypeStruct((ntok, gm_w), BF16),
            jax.ShapeDtypeStruct((ntok, ssd_w), BF16),
            jax.ShapeDtypeStruct((ntok, gm_w), F32),
            jax.ShapeDtypeStruct((n_seq, ssd_w, D_STATE), F32),
            jax.ShapeDtypeStruct((CONV_W - 1, n_seq, conv_dim), F32),
        ],
        scratch_shapes=[
            pltpu.VMEM((GM_HEADS, tb, tb), BF16),
            pltpu.VMEM((LANES, ssd_w), BF16),
            pltpu.VMEM((SUBLANES + tb, conv_dim), F32),
            pltpu.VMEM((tb, conv_dim), F32),
            pltpu.VMEM((tb, ssd_w), F32),
            pltpu.VMEM((tb, ssd_w), F32),
            pltpu.VMEM((tb, ssd_w), F32),
            pltpu.VMEM((tb, ssd_w), F32),
            pltpu.VMEM((tb, ssd_w), F32),
            pltpu.VMEM((tb, SSD_GROUPS * D_STATE), F32),
            pltpu.VMEM((tb, SSD_GROUPS * D_STATE), F32),
            pltpu.VMEM((ssd_w, tb), BF16),
            pltpu.VMEM((LANES, tb), F32),
            pltpu.VMEM((LANES, SAMPLE_SEQS_PER_STEP * LANES), F32),
        ],
        compiler_params=_cparams("arbitrary", "arbitrary"),
        name="mixer_sample",
    )(p1, p1, p1, xbc, dt, conv_state, h0, *params)


def kernel(x_prompt, x_sample, state_ssm, state_conv, ffn1_norm, ffn1_w1, ffn1_w3, ffn1_w2, mix_norm, w_in,
           b_gate, conv_w, conv_b, dt_bias, a_log, d_skip, ssd_norm, gm_v_norm, gm_ws, gm_bs, w_proj_a,
           w_proj_b, w_out, ffn2_norm, ffn2_w1, ffn2_w3, ffn2_w2, final_norm):
    bp, tp, d = x_prompt.shape
    bs_, ts, _ = x_sample.shape
    depth = w_in.shape[0]
    n_heads = dt_bias.shape[1]
    gm_w = gm_v_norm.shape[1]
    ssd_w = ssd_norm.shape[1]
    conv_dim = conv_w.shape[2]
    np_, ns = bp * tp, bs_ * ts

    xs = [x_prompt.reshape(np_, d), x_sample.reshape(ns, d)]
    ssm_p, conv_p, ssm_s, conv_s, v_s = [], [], [], [], []
    for l in range(depth):
        c_uvz = 2 * gm_w + ssd_w
        x, xn = _ffn(xs, ffn1_norm[l], ffn1_w1[l], ffn1_w3[l], ffn1_w2[l], next_norm_w=mix_norm[l])
        p1, xbc, dt, wa_b, wb_b, wo_b = _inproj(xn, w_in[l].T, c_uvz, conv_dim, n_heads,
                                                round_ws=(w_proj_a[l], w_proj_b[l], w_out[l]))

        lp = (gm_v_norm[l], gm_ws[l], gm_bs[l], conv_w[l], conv_b[l], dt_bias[l], a_log[l], d_skip[l], ssd_norm[l])
        ya_p, yb_p, hs_p, ct_p = _mixer_prompt(p1, xbc, dt, lp, bp, tp, gm_w, ssd_w, conv_dim)

        assert ts >= CONV_W - 1
        st = jnp.transpose(state_conv[l], (1, 0, 2))
        ya_s, yb_s, vn_s, hs_s, cs_s = _mixer_sample(p1, xbc, dt, lp, state_ssm[l].reshape(bs_, ssd_w, D_STATE), st,
                                                     np_, bs_, ts, gm_w, ssd_w, conv_dim)

        x, xn = _merge(x, ya_p, yb_p, ya_s, yb_s, p1, b_gate[l], wa_b, wb_b, wo_b, ffn2_norm[l])
        last = l == depth - 1
        xs = _ffn([x], ffn2_norm[l], ffn2_w1[l], ffn2_w3[l], ffn2_w2[l], xn=xn, final_w=final_norm if last else None,
                  split_rows=np_)

        ssm_p.append(hs_p.reshape(bp, n_heads, SSD_HEAD_DIM, D_STATE))
        conv_p.append(ct_p[:, SUBLANES - (CONV_W - 1):, :])
        ssm_s.append(hs_s.reshape(bs_, n_heads, SSD_HEAD_DIM, D_STATE))
        conv_s.append(jnp.transpose(cs_s, (1, 0, 2)))
        v_s.append(vn_s.reshape(bs_, ts, gm_w))

    return (xs[0].reshape(bp, tp, d), xs[1].reshape(bs_, ts, d), jnp.stack(ssm_p), jnp.stack(conv_p),
            jnp.stack(ssm_s), jnp.stack(conv_s), jnp.stack(v_s))
```

```python
import functools

import jax
import jax.numpy as jnp
from jax import lax
from jax.experimental import pallas as pl
from jax.experimental.pallas import tpu as pltpu

F32 = jnp.float32
BF16 = jnp.bfloat16
EPS = 1e-6

LANES = 128
SUBLANES = 8
VMEM_LIMIT_BYTES = 60 * 1024 * 1024

GM_HEADS = 8
SSD_HEAD_DIM = 64
SSD_GROUPS = 4
D_STATE = 128
CONV_W = 4
TOKEN_BLOCK = 128
SAMPLE_SEQS_PER_STEP = 8
PROMPT_SEQS_PER_STEP = 2

TOKEN_TILE = 1088
FFN_COLS = 512
PROJ_COLS = 1024
MERGE_TILE = 256
NORM_ROWS = 272
ROUND_ROWS = 64

_SINGLE = dict(pipeline_mode=pl.Buffered(1))


def _cparams(*sem):
    return pltpu.CompilerParams(dimension_semantics=sem, vmem_limit_bytes=VMEM_LIMIT_BYTES)


_GELU_C = 0.7978845608028654


def _gelu(x):
    t = jnp.tanh(x * (_GELU_C + (_GELU_C * 0.044715) * (x * x)))
    return x * (0.5 + 0.5 * t)


def _silu(x):
    return x * jax.nn.sigmoid(x)


def _softplus(x):
    return jnp.maximum(x, 0.0) + jnp.log1p(jnp.exp(-jnp.abs(x)))


def _split3(a):
    a1 = a.astype(BF16)
    r1 = a - a1.astype(F32)
    a2 = r1.astype(BF16)
    r2 = r1 - a2.astype(F32)
    return a1, a2, r2.astype(BF16)


def _dot(a, b):
    return jnp.dot(a, b, preferred_element_type=F32)


def _dot_nt(a, b):
    return lax.dot_general(a, b, (((1,), (1,)), ((), ())), preferred_element_type=F32)


def _sel_right(a, sel):
    a1, a2, a3 = _split3(a)
    return (_dot(a1, sel) + _dot(a2, sel)) + _dot(a3, sel)


def _sel_left(sel, a):
    a1, a2, a3 = _split3(a)
    return (_dot(sel, a1) + _dot(sel, a2)) + _dot(sel, a3)


def _row_loop(nrows, fn):
    def body(i, c):
        fn(pl.ds(pl.multiple_of(i * NORM_ROWS, NORM_ROWS), NORM_ROWS))
        return c

    lax.fori_loop(0, nrows // NORM_ROWS, body, 0)


def _rmsnorm_rows(x_ref, w_ref, out_ref, nrows):
    def one(sl):
        x = x_ref[sl, :]
        r = lax.rsqrt(jnp.mean(x * x, axis=-1, keepdims=True) + EPS)
        out_ref[sl, :] = ((x * r) * w_ref[...]).astype(out_ref.dtype)

    _row_loop(nrows, one)


def _tile_branches(n_full, tile, tail_rows, run, axis=0):
    i = pl.program_id(axis)

    @pl.when(i < n_full)
    def _():
        run(False, tile)

    if tail_rows:
        @pl.when(i >= n_full)
        def _():
            run(True, tail_rows)


def _tile_dma(action, t, *, hbm_refs, buf, sem, to_hbm, rows_first):
    tm = buf.shape[0]

    def go(pieces):
        for hbm, h0, b0, nr in pieces:
            h = hbm.at[pl.ds(h0, nr), :]
            v = buf.at[pl.ds(b0, nr), :]
            copy = pltpu.make_async_copy(v, h, sem) if to_hbm else pltpu.make_async_copy(h, v, sem)
            getattr(copy, action)()

    if len(hbm_refs) == 1:
        go([(hbm_refs[0], pl.multiple_of(t * tm, SUBLANES), 0, tm)])
        return
    n_pure, head = divmod(rows_first, tm)

    @pl.when(t < n_pure)
    def _():
        go([(hbm_refs[0], pl.multiple_of(t * tm, SUBLANES), 0, tm)])

    @pl.when(t >= n_pure)
    def _():
        go([(hbm_refs[0], n_pure * tm, 0, head), (hbm_refs[1], 0, head, tm - head)])


def _ffn_body(*refs, n_in, n_out, final_norm, emit_norm, has_xn, n_tiles, rows_first_in, rows_first_out):
    refs = list(refs)
    x_refs = [refs.pop(0) for _ in range(n_in)]
    xni_ref = refs.pop(0) if has_xn else None
    nw_ref, w1_ref, w3_ref, w2_ref, fn_ref = [refs.pop(0) for _ in range(5)]
    o_refs = [refs.pop(0) for _ in range(n_out)]
    xno_ref = refs.pop(0) if emit_norm else None
    acc_ref, xn_ref, in_sem, out_sem = refs
    i = pl.program_id(0)
    j = pl.program_id(1)
    nj = pl.num_programs(1)
    tm = acc_ref.shape[1]
    slot = i % 2
    other = 1 - slot
    acc = acc_ref.at[slot]
    dma_in = functools.partial(_tile_dma, hbm_refs=x_refs, to_hbm=False, rows_first=rows_first_in)
    dma_out = functools.partial(_tile_dma, hbm_refs=o_refs, to_hbm=True, rows_first=rows_first_out)

    @pl.when(j == 0)
    def _():
        @pl.when(i == 0)
        def _():
            dma_in("start", i, buf=acc, sem=in_sem.at[slot])

        dma_in("wait", i, buf=acc, sem=in_sem.at[slot])
        if not has_xn:
            _rmsnorm_rows(acc, nw_ref, xn_ref, tm)

    xn = (xni_ref if has_xn else xn_ref)[...]
    h1 = _dot(xn, w1_ref[...].astype(BF16))
    h3 = _dot(xn, w3_ref[...].astype(BF16))
    g = ((0.5 * _silu(h1)) * h3).astype(BF16)
    acc[...] += _dot(g, w2_ref[...].astype(BF16))

    @pl.when((j == nj - 2) & (i + 1 < n_tiles))
    def _():
        @pl.when(i >= 1)
        def _():
            dma_out("wait", i - 1, buf=acc_ref.at[other], sem=out_sem.at[other])

        dma_in("start", i + 1, buf=acc_ref.at[other], sem=in_sem.at[other])

    @pl.when(j == nj - 1)
    def _():
        if final_norm:
            _rmsnorm_rows(acc, fn_ref, acc, tm)
        if emit_norm:
            _rmsnorm_rows(acc, fn_ref, xno_ref, tm)
        dma_out("start", i, buf=acc, sem=out_sem.at[slot])

        @pl.when(i == n_tiles - 1)
        def _():
            if n_tiles > 1:
                dma_out("wait", i - 1, buf=acc_ref.at[other], sem=out_sem.at[other])
            dma_out("wait", i, buf=acc, sem=out_sem.at[slot])


def _ffn(xs, norm_w, w1, w3, w2, *, xn=None, final_w=None, next_norm_w=None, split_rows=None):
    d, f = w1.shape
    tm, tf = TOKEN_TILE, FFN_COLS
    n = sum(x.shape[0] for x in xs)
    assert n % tm == 0 and f % tf == 0 and f // tf >= 2
    n_tiles = n // tm
    for first in ([xs[0].shape[0]] if len(xs) == 2 else []) + ([split_rows] if split_rows else []):
        assert first % tm and (first // tm + 1) * tm == n
    final_norm = final_w is not None
    emit_norm = next_norm_w is not None
    assert not (final_norm and emit_norm)
    fw = final_w if final_norm else (next_norm_w if emit_norm else norm_w)
    has_xn = xn is not None
    any_spec = pl.BlockSpec(memory_space=pl.ANY)
    in_specs = [any_spec for _ in xs] + ([pl.BlockSpec((tm, d), lambda i, j: (i, 0))] if has_xn else [])
    if split_rows:
        o_specs = [any_spec, any_spec]
        o_shapes = [jax.ShapeDtypeStruct((split_rows, d), F32), jax.ShapeDtypeStruct((n - split_rows, d), F32)]
    else:
        o_specs = [any_spec]
        o_shapes = [jax.ShapeDtypeStruct((n, d), F32)]
    n_out = len(o_specs)
    if emit_norm:
        o_specs.append(pl.BlockSpec((tm, d), lambda i, j: (i, 0), **_SINGLE))
        o_shapes.append(jax.ShapeDtypeStruct((n, d), BF16))
    return pl.pallas_call(
        functools.partial(_ffn_body, n_in=len(xs), n_out=n_out, final_norm=final_norm, emit_norm=emit_norm,
                          has_xn=has_xn, n_tiles=n_tiles, rows_first_in=xs[0].shape[0], rows_first_out=split_rows),
        grid=(n_tiles, f // tf),
        in_specs=in_specs + [
            pl.BlockSpec((1, d), lambda i, j: (0, 0)),
            pl.BlockSpec((d, tf), lambda i, j: (0, j)),
            pl.BlockSpec((d, tf), lambda i, j: (0, j)),
            pl.BlockSpec((tf, d), lambda i, j: (j, 0)),
            pl.BlockSpec((1, d), lambda i, j: (0, 0)),
        ],
        out_specs=o_specs,
        out_shape=o_shapes,
        scratch_shapes=[
            pltpu.VMEM((2, tm, d), F32),
            pltpu.VMEM((SUBLANES, LANES) if has_xn else (tm, d), BF16),
            pltpu.SemaphoreType.DMA((2,)),
            pltpu.SemaphoreType.DMA((2,)),
        ],
        compiler_params=_cparams("arbitrary", "arbitrary"),
        name="ffn",
    )(*xs, *([xn] if has_xn else []), norm_w.reshape(1, d), w1, w3, w2, fw.reshape(1, d))


def _inproj_body(*refs, n_full, tail_rows, n_uvz, n_xbc, n_dt, n_tiles, rounds):
    nr = len(rounds)
    xn_ref, w_ref, wdt_ref = refs[:3]
    rin_refs = refs[3:3 + nr]
    p1_ref, xbc_ref, dt_ref = refs[3 + nr:6 + nr]
    rout_refs = refs[6 + nr:6 + 2 * nr]
    wb_ref = refs[6 + 2 * nr]
    j = pl.program_id(0)

    step = j * n_tiles + pl.program_id(1)
    for (s0, nblk), rin, rout in zip(rounds, rin_refs, rout_refs):
        @pl.when((step >= s0) & (step < s0 + nblk))
        def _(rin=rin, rout=rout):
            rout[...] = rin[...].astype(rout.dtype)

    is_xbc = (j >= n_uvz) & (j < n_uvz + n_xbc)

    @pl.when(pl.program_id(1) == 0)
    def _():
        wb_ref[...] = w_ref[...].astype(BF16)

    def run(is_tail, rows):
        @pl.when(j == 0)
        def _():
            dt = _dot_nt(xn_ref[0:rows, :], wdt_ref[...].astype(BF16))
            lane = _iota2(dt.shape, 1)
            dt_ref[0:rows, :] = jnp.where(lane < n_dt, dt, 0.0)

        @pl.when(jnp.logical_not(is_xbc))
        def _():
            p1_ref[0:rows, :] = _dot_nt(xn_ref[0:rows, :], wb_ref[...]).astype(p1_ref.dtype)

        @pl.when(is_xbc)
        def _():
            xbc_ref[0:rows, :] = _dot_nt(xn_ref[0:rows, :], wb_ref[...])

    _tile_branches(n_full, TOKEN_TILE, tail_rows, run, axis=1)


def _inproj(xn, w_t, c_uvz, c_xbc, c_dt, round_ws=()):
    n, d = xn.shape
    tm, tn = TOKEN_TILE, PROJ_COLS
    n_full, tail_rows = divmod(n, tm)
    n_tiles = n_full + (1 if tail_rows else 0)
    last_i = n_tiles - 1
    c_gate0 = c_uvz + c_xbc + c_dt
    c_gates = w_t.shape[0] - c_gate0
    n_uvz, n_xbc, n_g = c_uvz // tn, c_xbc // tn, c_gates // tn
    n_main = n_uvz + n_xbc
    assert c_gate0 % SUBLANES == 0 and (c_uvz + c_xbc) % SUBLANES == 0

    def w_row(j, i):
        return (pl.multiple_of(jnp.where(j < n_main, j * tn, c_gate0 + (j - n_main) * tn), SUBLANES), 0)

    def p1_idx(j, i):
        writes = (j < n_uvz) | (j >= n_main)
        col = jnp.where(j < n_uvz, j, jnp.maximum(j - n_xbc, n_uvz - 1))
        return (jnp.where(writes, i, last_i), col)

    def xbc_idx(j, i):
        row = jnp.where(j < n_uvz, 0, jnp.where(j < n_main, i, last_i))
        return (row, jnp.clip(j - n_uvz, 0, n_xbc - 1))

    rounds, s0 = [], 0
    for w in round_ws:
        assert w.shape[0] % ROUND_ROWS == 0
        rounds.append((s0, w.shape[0] // ROUND_ROWS))
        s0 += w.shape[0] // ROUND_ROWS
    assert s0 <= (n_main + n_g) * n_tiles

    def r_specs():
        return [pl.BlockSpec((ROUND_ROWS, w.shape[1]),
                             lambda j, i, s0=s0, nblk=nblk: (jnp.clip(j * n_tiles + i - s0, 0, nblk - 1), 0))
                for w, (s0, nblk) in zip(round_ws, rounds)]

    return pl.pallas_call(
        functools.partial(_inproj_body, n_full=n_full, tail_rows=tail_rows, n_uvz=n_uvz, n_xbc=n_xbc, n_dt=c_dt,
                          n_tiles=n_tiles, rounds=tuple(rounds)),
        grid=(n_main + n_g, n_tiles),
        in_specs=[
            pl.BlockSpec((tm, d), lambda j, i: (i, 0)),
            pl.BlockSpec((pl.Element(tn), pl.Element(d)), w_row),
            pl.BlockSpec((pl.Element(LANES), pl.Element(d)), lambda j, i: (c_uvz + c_xbc, 0)),
        ] + r_specs(),
        out_specs=[
            pl.BlockSpec((tm, tn), p1_idx),
            pl.BlockSpec((tm, tn), xbc_idx),
            pl.BlockSpec((tm, LANES), lambda j, i: (jnp.where(j == 0, i, last_i), 0)),
        ] + r_specs(),
        out_shape=[
            jax.ShapeDtypeStruct((n, c_uvz + c_gates), BF16),
            jax.ShapeDtypeStruct((n, c_xbc), F32),
            jax.ShapeDtypeStruct((n, LANES), F32),
        ] + [jax.ShapeDtypeStruct(w.shape, BF16) for w in round_ws],
        scratch_shapes=[pltpu.VMEM((tn, d), BF16)],
        compiler_params=_cparams("arbitrary", "arbitrary"),
        name="in_proj",
    )(xn, w_t, w_t, *round_ws)


def _merge_body(x_ref, yap_ref, ybp_ref, yas_ref, ybs_ref, ga_ref, gb_ref, bg_ref, wa_ref, wb_ref, wo_ref, nw_ref,
                o_ref, xn_ref, *, n_prompt_tiles):
    def run(ya_ref, yb_ref):
        pa = _dot(ya_ref[...], wa_ref[...])
        pb = _dot(yb_ref[...], wb_ref[...])
        ga = jax.nn.sigmoid(ga_ref[...].astype(F32) + bg_ref[0:1, :])
        gb = jax.nn.sigmoid(gb_ref[...].astype(F32) + bg_ref[1:2, :])
        m = (ga * pa + gb * pb).astype(BF16)
        o = x_ref[...] + _dot(m, wo_ref[...])
        o_ref[...] = o
        r = lax.rsqrt(jnp.mean(o * o, axis=-1, keepdims=True) + EPS)
        xn_ref[...] = ((o * r) * nw_ref[...]).astype(xn_ref.dtype)

    i = pl.program_id(0)

    @pl.when(i < n_prompt_tiles)
    def _():
        run(yap_ref, ybp_ref)

    @pl.when(i >= n_prompt_tiles)
    def _():
        run(yas_ref, ybs_ref)


def _merge(x, ya_p, yb_p, ya_s, yb_s, p1, b_gate, wa, wb, wo, next_norm_w):
    n, d = x.shape
    tm = MERGE_TILE
    npt = ya_p.shape[0] // tm
    nst = ya_s.shape[0] // tm
    gcol = p1.shape[1] // d - 2
    p_idx = lambda i: (jnp.minimum(i, npt - 1), 0)
    s_idx = lambda i: (jnp.clip(i - npt, 0, nst - 1), 0)
    return pl.pallas_call(
        functools.partial(_merge_body, n_prompt_tiles=npt),
        grid=(n // tm,),
        in_specs=[
            pl.BlockSpec((tm, d), lambda i: (i, 0)),
            pl.BlockSpec((tm, ya_p.shape[1]), p_idx),
            pl.BlockSpec((tm, yb_p.shape[1]), p_idx),
            pl.BlockSpec((tm, ya_s.shape[1]), s_idx),
            pl.BlockSpec((tm, yb_s.shape[1]), s_idx),
            pl.BlockSpec((tm, d), lambda i: (i, gcol)),
            pl.BlockSpec((tm, d), lambda i: (i, gcol + 1)),
            pl.BlockSpec((2, d), lambda i: (0, 0)),
            pl.BlockSpec(wa.shape, lambda i: (0, 0), **_SINGLE),
            pl.BlockSpec(wb.shape, lambda i: (0, 0), **_SINGLE),
            pl.BlockSpec(wo.shape, lambda i: (0, 0), **_SINGLE),
            pl.BlockSpec((1, d), lambda i: (0, 0)),
        ],
        out_specs=[pl.BlockSpec((tm, d), lambda i: (i, 0)), pl.BlockSpec((tm, d), lambda i: (i, 0))],
        out_shape=[jax.ShapeDtypeStruct((n, d), F32), jax.ShapeDtypeStruct((n, d), BF16)],
        compiler_params=_cparams("arbitrary"),
        name="merge",
    )(x, ya_p, yb_p, ya_s, yb_s, p1, p1, b_gate, wa, wb, wo, next_norm_w.reshape(1, d))


def _iota2(shape, dim):
    return lax.broadcasted_iota(jnp.int32, shape, dim)


def _block_masks(seq_len):
    tb = TOKEN_BLOCK
    row = _iota2((tb, tb), 0)
    col = _iota2((tb, tb), 1)
    if seq_len >= tb:
        causal = col <= row
        last = col == tb - 1
    else:
        causal = (col <= row) & ((row // seq_len) == (col // seq_len))
        last = col == (row // seq_len) * seq_len + (seq_len - 1)
    return causal, last


def _init_constants(wm_ref, e_ref, gw_ref, causal):
    for h in range(GM_HEADS):
        wm_ref[h] = jnp.where(causal, gw_ref[h], 0.0).astype(BF16)
    hrow = _iota2(e_ref.shape, 0)
    ccol = _iota2(e_ref.shape, 1)
    e_ref[...] = jnp.where(hrow == ccol // SSD_HEAD_DIM, 1.0, 0.0).astype(BF16)


def _gating(u_ref, v_ref, gvn_ref, bs_ref, wm_ref, ya_ref, vn_ref):
    vg = _gelu(v_ref[...].astype(F32))
    r = lax.rsqrt(jnp.mean(vg * vg, axis=-1, keepdims=True) + EPS)
    vn = (vg * r) * gvn_ref[...]
    if vn_ref is not None:
        vn_ref[...] = vn
    vnb = vn.astype(BF16)
    for h in range(GM_HEADS):
        cs = slice(h * LANES, (h + 1) * LANES)
        s = _dot(wm_ref[h], vnb[:, cs]) + bs_ref[:, h:h + 1]
        ya_ref[:, cs] = (_gelu(u_ref[:, cs].astype(F32)) * s).astype(ya_ref.dtype)


def _conv(xp_ref, cw_ref, cb_ref, xc_ref, cst_ref, seq_len):
    tb = TOKEN_BLOCK
    width = xc_ref.shape[1]
    cw = 512
    ns = CONV_W - 1
    place = None
    if cst_ref is not None:
        nseq = cst_ref.shape[1]
        tpos = _iota2((tb, 1), 0) % seq_len
        row = _iota2((tb, nseq), 0)
        col = _iota2((tb, nseq), 1)
        place = {(d, r): jnp.where((col == row // seq_len) & (row % seq_len == r + d - ns), 1.0, 0.0).astype(BF16)
                 for d in range(1, CONV_W) for r in range(ns - d, ns)}
    for c0 in range(0, width, cw):
        cs = slice(c0, c0 + cw)
        acc = None
        full = xp_ref[:, cs]
        if place is not None:
            pieces = [_split3(cst_ref[r, :, cs]) for r in range(ns)]
        for k in range(CONV_W):
            d = CONV_W - 1 - k
            xs = (pltpu.roll(full, d, axis=0) if d else full)[SUBLANES:SUBLANES + tb, :]
            if place is not None and d > 0:
                xs = jnp.where(tpos >= d, xs, 0.0)
                for r in range(ns - d, ns):
                    pd = place[(d, r)]
                    xs = xs + ((_dot(pd, pieces[r][0]) + _dot(pd, pieces[r][1])) + _dot(pd, pieces[r][2]))
            term = cw_ref[k:k + 1, cs] * xs
            acc = term if acc is None else acc + term
        xc_ref[:, cs] = _silu(cb_ref[:, cs] + acc)


def _ssd_prepare(dt_ref, dtb_ref, alog_ref, e_ref, causal, last, dtf_ref, tef_ref, eaf_ref):
    dt = _softplus(dt_ref[...] + dtb_ref[...])
    a = -jnp.exp(alog_ref[...])
    da = dt * a
    lmat = jnp.where(causal, 1.0, 0.0).astype(BF16)
    acum = _sel_left(lmat, da)
    a_last = _sel_left(jnp.where(last, 1.0, 0.0).astype(BF16), acum)
    ea = jnp.exp(acum)
    e = e_ref[...]
    dtf_ref[...] = _sel_right(dt, e)
    tef_ref[...] = _sel_right(dt * jnp.exp(a_last - acum), e)
    eaf_ref[...] = _sel_right(ea, e)
    return acum, acum.T


def _ssd_diag_group(g, xc_ref, dtf_ref, acum, acum_t, causal):
    tb = TOKEN_BLOCK
    width = xc_ref.shape[1] - 2 * SSD_GROUPS * D_STATE
    gw = width // SSD_GROUPS
    bg = xc_ref[:, width + g * D_STATE:width + (g + 1) * D_STATE]
    cg = xc_ref[:, width + (SSD_GROUPS + g) * D_STATE:width + (SSD_GROUPS + g + 1) * D_STATE].astype(BF16)
    cb = jnp.where(causal, _dot_nt(cg, bg.astype(BF16)), 0.0)
    lane = _iota2((tb, LANES), 1)
    heads_per_group = gw // SSD_HEAD_DIM
    outs = []
    for jp in range(heads_per_group // 2):
        ms = []
        for hh in range(2):
            h = g * heads_per_group + 2 * jp + hh
            seg = jnp.minimum(acum[:, h:h + 1] - acum_t[h:h + 1, :], 0.0)
            ms.append((cb * jnp.exp(seg)).astype(BF16))
        lhs = jnp.concatenate(ms, axis=1)
        ps = slice(g * gw + jp * LANES, g * gw + (jp + 1) * LANES)
        xpair = (xc_ref[:, ps] * dtf_ref[:, ps]).astype(BF16)
        zero = jnp.zeros_like(xpair)
        rhs = jnp.concatenate([jnp.where(lane < SSD_HEAD_DIM, xpair, zero),
                               jnp.where(lane >= SSD_HEAD_DIM, xpair, zero)], axis=0)
        outs.append(_dot(lhs, rhs))
    return jnp.concatenate(outs, axis=1), cg, bg


def _finish_group(g, y, xc_ref, z_ref, dsk_ref, sn_ref, yb_ref):
    gw = yb_ref.shape[1] // SSD_GROUPS
    gs = slice(g * gw, (g + 1) * gw)
    y = y + dsk_ref[:, gs] * xc_ref[:, gs]
    yz = y * _silu(z_ref[:, gs].astype(F32))
    r = lax.rsqrt(jnp.mean(yz * yz, axis=-1, keepdims=True) + EPS)
    yb_ref[:, gs] = ((yz * r) * sn_ref[:, gs]).astype(yb_ref.dtype)


def _prompt_body(*refs, n_par):
    refs = list(refs)
    ins = [[refs.pop(0) for _ in range(5)] for _ in range(n_par)]
    gvn_ref, gw_ref, bs_ref, cw_ref, cb_ref, dtb_ref, alog_ref, dsk_ref, sn_ref = [refs.pop(0) for _ in range(9)]
    out4 = [refs.pop(0) for _ in range(4)]
    outs = [[o.at[p] for o in out4] for p in range(n_par)]
    wm_ref, e_ref = refs.pop(0), refs.pop(0)
    scr = [[refs.pop(0) for _ in range(6)] for _ in range(n_par)]
    tb = TOKEN_BLOCK
    c = pl.program_id(1)
    causal, last = _block_masks(tb)

    @pl.when((pl.program_id(0) == 0) & (c == 0))
    def _():
        _init_constants(wm_ref, e_ref, gw_ref, causal)

    @pl.when(c == 0)
    def _():
        for xp_ref, _, _, _, _, st_ref in scr:
            st_ref[...] = jnp.zeros_like(st_ref)
            xp_ref[0:SUBLANES, :] = jnp.zeros((SUBLANES, xp_ref.shape[1]), F32)

    for (u_ref, v_ref, z_ref, xbc_ref, dt_ref), (ya_ref, yb_ref, _, _), \
            (xp_ref, xc_ref, dtf_ref, tef_ref, eaf_ref, st_ref) in zip(ins, outs, scr):
        _gating(u_ref, v_ref, gvn_ref, bs_ref, wm_ref, ya_ref, None)

        xp_ref[SUBLANES:SUBLANES + tb, :] = xbc_ref[...]
        _conv(xp_ref, cw_ref, cb_ref, xc_ref, None, tb)
        xp_ref[0:SUBLANES, :] = xbc_ref[tb - SUBLANES:tb, :]

        acum, acum_t = _ssd_prepare(dt_ref, dtb_ref, alog_ref, e_ref, causal, last, dtf_ref, tef_ref, eaf_ref)

        gw = yb_ref.shape[1] // SSD_GROUPS
        for g in range(SSD_GROUPS):
            gs = slice(g * gw, (g + 1) * gw)
            yd, cg, bg = _ssd_diag_group(g, xc_ref, dtf_ref, acum, acum_t, causal)
            st = st_ref[:, gs]
            y = yd + _dot(cg, st.astype(BF16)) * eaf_ref[:, gs]
            _finish_group(g, y, xc_ref, z_ref, dsk_ref, sn_ref, yb_ref)
            xw = (xc_ref[:, gs] * tef_ref[:, gs]).astype(BF16)
            st_ref[:, gs] = eaf_ref[tb - 1:tb, gs] * st + _dot(bg.T.astype(BF16), xw)

    @pl.when(c == pl.num_programs(1) - 1)
    def _():
        for (_, _, _, xbc_ref, _), (_, _, hs_ref, ct_ref), (_, _, _, _, _, st_ref) in zip(ins, outs, scr):
            for k in range(st_ref.shape[1] // LANES):
                hs_ref[k * LANES:(k + 1) * LANES, :] = st_ref[:, k * LANES:(k + 1) * LANES].T
            ct_ref[...] = xbc_ref[tb - SUBLANES:tb, :]


def _sample_body(u_ref, v_ref, z_ref, xbc_ref, dt_ref, cst_ref, h0_ref,
                 gvn_ref, gw_ref, bs_ref, cw_ref, cb_ref, dtb_ref, alog_ref, dsk_ref, sn_ref,
                 ya_ref, yb_ref, vn_ref, hs_ref, cs_ref,
                 wm_ref, e_ref, xp_ref, xc_ref, dtf_ref, tef_ref, eaf_ref,
                 y_ref, yoff_ref, cg_ref, bg_ref, xwt_ref, eat_ref, cdh_ref, *, seq_len):
    tb = TOKEN_BLOCK
    k = pl.program_id(1)
    causal, last = _block_masks(seq_len)
    gw = yb_ref.shape[1] // SSD_GROUPS

    @pl.when((pl.program_id(0) == 0) & (k == 0))
    def _():
        _init_constants(wm_ref, e_ref, gw_ref, causal)
        xp_ref[0:SUBLANES, :] = jnp.zeros((SUBLANES, xp_ref.shape[1]), F32)

    @pl.when(k == 0)
    def _():
        _gating(u_ref, v_ref, gvn_ref, bs_ref, wm_ref, ya_ref, vn_ref)
        xp_ref[SUBLANES:SUBLANES + tb, :] = xbc_ref[...]
        _conv(xp_ref, cw_ref, cb_ref, xc_ref, cst_ref, seq_len)
        nsb = tb // seq_len
        pick = [jnp.where(_iota2((nsb, tb), 1) == _iota2((nsb, tb), 0) * seq_len + (seq_len - (CONV_W - 1) + r),
                          1.0, 0.0).astype(BF16) for r in range(CONV_W - 1)]
        for c0 in range(0, xbc_ref.shape[1], 512):
            pieces = _split3(xbc_ref[:, c0:c0 + 512])
            for r in range(CONV_W - 1):
                cs_ref[r, :, c0:c0 + 512] = ((_dot(pick[r], pieces[0]) + _dot(pick[r], pieces[1]))
                                             + _dot(pick[r], pieces[2]))
        acum, acum_t = _ssd_prepare(dt_ref, dtb_ref, alog_ref, e_ref, causal, last, dtf_ref, tef_ref, eaf_ref)
        eat_ref[...] = jnp.exp(acum_t)
        for g in range(SSD_GROUPS):
            gs = slice(g * gw, (g + 1) * gw)
            yd, cg, bg = _ssd_diag_group(g, xc_ref, dtf_ref, acum, acum_t, causal)
            y_ref[:, gs] = yd
            cg_ref[:, g * D_STATE:(g + 1) * D_STATE] = cg.astype(F32)
            bg_ref[:, g * D_STATE:(g + 1) * D_STATE] = bg
            xw = xc_ref[:, gs] * tef_ref[:, gs]
            for q in range(gw // LANES):
                r0 = g * gw + q * LANES
                xwt_ref[r0:r0 + LANES, :] = xw[:, q * LANES:(q + 1) * LANES].T.astype(BF16)

    nseq = h0_ref.shape[0]
    nt = nseq * seq_len
    rows = pl.ds(pl.multiple_of(k * nt, nt), nt)
    rowseq = _iota2((tb, 1), 0) // seq_len
    subseq = _iota2((nt, 1), 0) // seq_len
    tok = _iota2((tb, LANES), 0)
    onehot = jnp.concatenate(
        [jnp.where(tok == (k * nseq + b) * seq_len + (seq_len - 1), 1.0, 0.0) for b in range(nseq)],
        axis=1).astype(BF16)
    cdh_ref[...] = _sel_right(eat_ref[...], onehot)
    hpg = gw // SSD_HEAD_DIM
    for g in range(SSD_GROUPS):
        gs = slice(g * gw, (g + 1) * gw)
        ds = slice(g * D_STATE, (g + 1) * D_STATE)
        c_sub = cg_ref[rows, ds]
        lhs = jnp.concatenate([jnp.where(subseq == b, c_sub, 0.0) for b in range(nseq)], axis=1).astype(BF16)
        h0cat = jnp.concatenate([h0_ref[b, gs, :].astype(BF16) for b in range(nseq)], axis=1)
        yoff_ref[rows, gs] = _dot_nt(lhs, h0cat)
        b_all = bg_ref[:, ds]
        bm = jnp.concatenate([jnp.where(rowseq == k * nseq + b, b_all, 0.0) for b in range(nseq)],
                             axis=1).astype(BF16)
        s_new = _dot(xwt_ref[gs, :], bm)
        for b in range(nseq):
            ls = slice(b * D_STATE, (b + 1) * D_STATE)
            for j in range(hpg):
                h = g * hpg + j
                r0 = g * gw + j * SSD_HEAD_DIM
                hs_ref[b, r0:r0 + SSD_HEAD_DIM, :] = (cdh_ref[h:h + 1, ls] * h0_ref[b, r0:r0 + SSD_HEAD_DIM, :]
                                                      + s_new[j * SSD_HEAD_DIM:(j + 1) * SSD_HEAD_DIM, ls])

    @pl.when(k == pl.num_programs(1) - 1)
    def _():
        for g in range(SSD_GROUPS):
            gs = slice(g * gw, (g + 1) * gw)
            y = y_ref[:, gs] + yoff_ref[:, gs] * eaf_ref[:, gs]
            _finish_group(g, y, xc_ref, z_ref, dsk_ref, sn_ref, yb_ref)


def _mixer_params(lp):
    (gm_v_norm, gm_ws, gm_bs, conv_w, conv_b, dt_bias, a_log, d_skip, ssd_norm) = lp
    nh = dt_bias.shape[0]
    pad = LANES - nh
    return dict(
        gvn=gm_v_norm.reshape(1, -1),
        cw=conv_w,
        cb=conv_b.reshape(1, -1),
        dtb=jnp.pad(dt_bias, (0, pad)).reshape(1, LANES),
        alog=jnp.pad(a_log, (0, pad)).reshape(1, LANES),
        dsk=jnp.repeat(d_skip, SSD_HEAD_DIM).reshape(1, -1),
        sn=ssd_norm.reshape(1, -1),
    )


def _full_spec(a):
    nd = a.ndim
    return pl.BlockSpec(a.shape, lambda i, j: (0,) * nd)


def _mixer_prompt(p1, xbc, dt, lp, n_seq, seq_len, gm_w, ssd_w, conv_dim):
    tb = TOKEN_BLOCK
    nc = seq_len // tb
    n_par = PROMPT_SEQS_PER_STEP if n_seq % PROMPT_SEQS_PER_STEP == 0 else 1
    mp = _mixer_params(lp)
    gw_full = lp[1]
    bs_t = lp[2].T
    params = [mp["gvn"], gw_full, bs_t, mp["cw"], mp["cb"], mp["dtb"], mp["alog"], mp["dsk"], mp["sn"]]
    in_specs, scratch, operands = [], [], []
    for p in range(n_par):
        row = lambda b, c, p=p: (b * n_par + p) * nc + c
        in_specs += [
            pl.BlockSpec((tb, gm_w), lambda b, c, row=row: (row(b, c), 0)),
            pl.BlockSpec((tb, gm_w), lambda b, c, row=row: (row(b, c), 1)),
            pl.BlockSpec((tb, ssd_w), lambda b, c, row=row: (row(b, c), 1)),
            pl.BlockSpec((tb, conv_dim), lambda b, c, row=row: (row(b, c), 0)),
            pl.BlockSpec((tb, LANES), lambda b, c, row=row: (row(b, c), 0)),
        ]
        operands += [p1, p1, p1, xbc, dt]
        scratch += [
            pltpu.VMEM((SUBLANES + tb, conv_dim), F32),
            pltpu.VMEM((tb, conv_dim), F32),
            pltpu.VMEM((tb, ssd_w), F32),
            pltpu.VMEM((tb, ssd_w), F32),
            pltpu.VMEM((tb, ssd_w), F32),
            pltpu.VMEM((D_STATE, ssd_w), F32),
        ]
    res = pl.pallas_call(
        functools.partial(_prompt_body, n_par=n_par),
        grid=(n_seq // n_par, nc),
        in_specs=in_specs + [_full_spec(a) for a in params],
        out_specs=[
            pl.BlockSpec((n_par, tb, gm_w), lambda b, c: (b, c, 0)),
            pl.BlockSpec((n_par, tb, ssd_w), lambda b, c: (b, c, 0)),
            pl.BlockSpec((n_par, ssd_w, D_STATE), lambda b, c: (b, 0, 0)),
            pl.BlockSpec((n_par, SUBLANES, conv_dim), lambda b, c: (b, 0, 0)),
        ],
        out_shape=[
            jax.ShapeDtypeStruct((n_seq, seq_len, gm_w), BF16),
            jax.ShapeDtypeStruct((n_seq, seq_len, ssd_w), BF16),
            jax.ShapeDtypeStruct((n_seq, ssd_w, D_STATE), F32),
            jax.ShapeDtypeStruct((n_seq, SUBLANES, conv_dim), F32),
        ],
        scratch_shapes=[
            pltpu.VMEM((GM_HEADS, tb, tb), BF16),
            pltpu.VMEM((LANES, ssd_w), BF16),
        ] + scratch,
        compiler_params=_cparams("arbitrary", "arbitrary"),
        name="mixer_prompt",
    )(*operands, *params)
    ya, yb, hs, ct = res
    return ya.reshape(n_seq * seq_len, gm_w), yb.reshape(n_seq * seq_len, ssd_w), hs, ct


def _mixer_sample(p1, xbc, dt, lp, h0, conv_state, row0, n_seq, seq_len, gm_w, ssd_w, conv_dim):
    tb = TOKEN_BLOCK
    seqs_per_block = tb // seq_len
    nblk = n_seq // seqs_per_block
    nsub = seqs_per_block // SAMPLE_SEQS_PER_STEP
    blk0 = row0 // tb
    mp = _mixer_params(lp)
    reps = tb // seq_len
    gw_tiled = jnp.tile(lp[1][:, :seq_len, :seq_len], (1, reps, reps))
    bs_t = jnp.tile(lp[2][:, :seq_len], (1, reps)).T
    params = [mp["gvn"], gw_tiled, bs_t, mp["cw"], mp["cb"], mp["dtb"], mp["alog"], mp["dsk"], mp["sn"]]
    ntok = n_seq * seq_len
    return pl.pallas_call(
        functools.partial(_sample_body, seq_len=seq_len),
        grid=(nblk, nsub),
        in_specs=[
            pl.BlockSpec((tb, gm_w), lambda i, k: (blk0 + i, 0)),
            pl.BlockSpec((tb, gm_w), lambda i, k: (blk0 + i, 1)),
            pl.BlockSpec((tb, ssd_w), lambda i, k: (blk0 + i, 1)),
            pl.BlockSpec((tb, conv_dim), lambda i, k: (blk0 + i, 0)),
            pl.BlockSpec((tb, LANES), lambda i, k: (blk0 + i, 0)),
            pl.BlockSpec((CONV_W - 1, seqs_per_block, conv_dim), lambda i, k: (0, i, 0)),
            pl.BlockSpec((SAMPLE_SEQS_PER_STEP, ssd_w, D_STATE), lambda i, k: (i * nsub + k, 0, 0)),
        ] + [_full_spec(a) for a in params],
        out_specs=[
            pl.BlockSpec((tb, gm_w), lambda i, k: (i, 0)),
            pl.BlockSpec((tb, ssd_w), lambda i, k: (i, 0)),
            pl.BlockSpec((tb, gm_w), lambda i, k: (i, 0)),
            pl.BlockSpec((SAMPLE_SEQS_PER_STEP, ssd_w, D_STATE), lambda i, k: (i * nsub + k, 0, 0)),
            pl.BlockSpec((CONV_W - 1, seqs_per_block, conv_dim), lambda i, k: (0, i, 0)),
        ],
        out_shape=[
            jax.ShapeDtypeStruct((ntok, gm_w), BF16),
            jax.ShapeDtypeStruct((ntok, ssd_w), BF16),
            jax.ShapeDtypeStruct((ntok, gm_w), F32),
            jax.ShapeDtypeStruct((n_seq, ssd_w, D_STATE), F32),
            jax.ShapeDtypeStruct((CONV_W - 1, n_seq, conv_dim), F32),
        ],
        scratch_shapes=[
            pltpu.VMEM((GM_HEADS, tb, tb), BF16),
            pltpu.VMEM((LANES, ssd_w), BF16),
            pltpu.VMEM((SUBLANES + tb, conv_dim), F32),
            pltpu.VMEM((tb, conv_dim), F32),
            pltpu.VMEM((tb, ssd_w), F32),
            pltpu.VMEM((tb, ssd_w), F32),
            pltpu.VMEM((tb, ssd_w), F32),
            pltpu.VMEM((tb, ssd_w), F32),
            pltpu.VMEM((tb, ssd_w), F32),
            pltpu.VMEM((tb, SSD_GROUPS * D_STATE), F32),
            pltpu.VMEM((tb, SSD_GROUPS * D_STATE), F32),
            pltpu.VMEM((ssd_w, tb), BF16),
            pltpu.VMEM((LANES, tb), F32),
            pltpu.VMEM((LANES, SAMPLE_SEQS_PER_STEP * LANES), F32),
        ],
        compiler_params=_cparams("arbitrary", "arbitrary"),
        name="mixer_sample",
    )(p1, p1, p1, xbc, dt, conv_state, h0, *params)


def kernel(x_prompt, x_sample, state_ssm, state_conv, ffn1_norm, ffn1_w1, ffn1_w3, ffn1_w2, mix_norm, w_in,
           b_gate, conv_w, conv_b, dt_bias, a_log, d_skip, ssd_norm, gm_v_norm, gm_ws, gm_bs, w_proj_a,
           w_proj_b, w_out, ffn2_norm, ffn2_w1, ffn2_w3, ffn2_w2, final_norm):
    bp, tp, d = x_prompt.shape
    bs_, ts, _ = x_sample.shape
    depth = w_in.shape[0]
    n_heads = dt_bias.shape[1]
    gm_w = gm_v_norm.shape[1]
    ssd_w = ssd_norm.shape[1]
    conv_dim = conv_w.shape[2]
    np_, ns = bp * tp, bs_ * ts

    xs = [x_prompt.reshape(np_, d), x_sample.reshape(ns, d)]
    ssm_p, conv_p, ssm_s, conv_s, v_s = [], [], [], [], []
    for l in range(depth):
        c_uvz = 2 * gm_w + ssd_w
        x, xn = _ffn(xs, ffn1_norm[l], ffn1_w1[l], ffn1_w3[l], ffn1_w2[l], next_norm_w=mix_norm[l])
        p1, xbc, dt, wa_b, wb_b, wo_b = _inproj(xn, w_in[l].T, c_uvz, conv_dim, n_heads,
                                                round_ws=(w_proj_a[l], w_proj_b[l], w_out[l]))

        lp = (gm_v_norm[l], gm_ws[l], gm_bs[l], conv_w[l], conv_b[l], dt_bias[l], a_log[l], d_skip[l], ssd_norm[l])
        ya_p, yb_p, hs_p, ct_p = _mixer_prompt(p1, xbc, dt, lp, bp, tp, gm_w, ssd_w, conv_dim)

        assert ts >= CONV_W - 1
        st = jnp.transpose(state_conv[l], (1, 0, 2))
        ya_s, yb_s, vn_s, hs_s, cs_s = _mixer_sample(p1, xbc, dt, lp, state_ssm[l].reshape(bs_, ssd_w, D_STATE), st,
                                                     np_, bs_, ts, gm_w, ssd_w, conv_dim)

        x, xn = _merge(x, ya_p, yb_p, ya_s, yb_s, p1, b_gate[l], wa_b, wb_b, wo_b, ffn2_norm[l])
        last = l == depth - 1
        xs = _ffn([x], ffn2_norm[l], ffn2_w1[l], ffn2_w3[l], ffn2_w2[l], xn=xn, final_w=final_norm if last else None,
                  split_rows=np_)

        ssm_p.append(hs_p.reshape(bp, n_heads, SSD_HEAD_DIM, D_STATE))
        conv_p.append(ct_p[:, SUBLANES - (CONV_W - 1):, :])
        ssm_s.append(hs_s.reshape(bs_, n_heads, SSD_HEAD_DIM, D_STATE))
        conv_s.append(jnp.transpose(cs_s, (1, 0, 2)))
        v_s.append(vn_s.reshape(bs_, ts, gm_w))

    return (xs[0].reshape(bp, tp, d), xs[1].reshape(bs_, ts, d), jnp.stack(ssm_p), jnp.stack(conv_p),
            jnp.stack(ssm_s), jnp.stack(conv_s), jnp.stack(v_s))
```

```python
import functools

import jax
import jax.numpy as jnp
from jax import lax
from jax.experimental import pallas as pl
from jax.experimental.pallas import tpu as pltpu

F32 = jnp.float32
BF16 = jnp.bfloat16
EPS = 1e-6

LANES = 128
SUBLANES = 8
VMEM_LIMIT_BYTES = 60 * 1024 * 1024

GM_HEADS = 8
SSD_HEAD_DIM = 64
SSD_GROUPS = 4
D_STATE = 128
CONV_W = 4
TOKEN_BLOCK = 128
SAMPLE_SEQS_PER_STEP = 8
PROMPT_SEQS_PER_STEP = 2

TOKEN_TILE = 1088
FFN_COLS = 512
PROJ_COLS = 1024
MERGE_TILE = 256
NORM_ROWS = 272
ROUND_ROW_ALIGN = 16
CHUNK_COLS = 512

_SINGLE = dict(pipeline_mode=pl.Buffered(1))


def _cparams(*sem):
    return pltpu.CompilerParams(dimension_semantics=sem, vmem_limit_bytes=VMEM_LIMIT_BYTES)


_GELU_C = 0.7978845608028654


def _gelu(x):
    t = jnp.tanh(x * (_GELU_C + (_GELU_C * 0.044715) * (x * x)))
    return x * (0.5 + 0.5 * t)


def _silu(x):
    return x * jax.nn.sigmoid(x)


def _softplus(x):
    return jnp.maximum(x, 0.0) + jnp.log1p(jnp.exp(-jnp.abs(x)))


def _split3(a):
    a1 = a.astype(BF16)
    r1 = a - a1.astype(F32)
    a2 = r1.astype(BF16)
    r2 = r1 - a2.astype(F32)
    return a1, a2, r2.astype(BF16)


def _dot(a, b):
    return jnp.dot(a, b, preferred_element_type=F32)


def _dot_nt(a, b):
    return lax.dot_general(a, b, (((1,), (1,)), ((), ())), preferred_element_type=F32)


def _sel_right(a, sel):
    a1, a2, a3 = _split3(a)
    return (_dot(a1, sel) + _dot(a2, sel)) + _dot(a3, sel)


def _sel_left(sel, a):
    a1, a2, a3 = _split3(a)
    return (_dot(sel, a1) + _dot(sel, a2)) + _dot(sel, a3)


def _row_loop(nrows, fn):
    def body(i, c):
        fn(pl.ds(pl.multiple_of(i * NORM_ROWS, NORM_ROWS), NORM_ROWS))
        return c

    lax.fori_loop(0, nrows // NORM_ROWS, body, 0)


def _rmsnorm_rows(x_ref, w_ref, out_ref, nrows):
    def one(sl):
        x = x_ref[sl, :]
        r = lax.rsqrt(jnp.mean(x * x, axis=-1, keepdims=True) + EPS)
        out_ref[sl, :] = ((x * r) * w_ref[...]).astype(out_ref.dtype)

    _row_loop(nrows, one)


def _tile_branches(n_full, tile, tail_rows, run, axis=0):
    i = pl.program_id(axis)

    @pl.when(i < n_full)
    def _():
        run(False, tile)

    if tail_rows:
        @pl.when(i >= n_full)
        def _():
            run(True, tail_rows)


def _tile_dma(action, t, *, hbm_refs, buf, sem, to_hbm, rows_first):
    tm = buf.shape[0]

    def go(pieces):
        for hbm, h0, b0, nr in pieces:
            h = hbm.at[pl.ds(h0, nr), :]
            v = buf.at[pl.ds(b0, nr), :]
            copy = pltpu.make_async_copy(v, h, sem) if to_hbm else pltpu.make_async_copy(h, v, sem)
            getattr(copy, action)()

    if len(hbm_refs) == 1:
        go([(hbm_refs[0], pl.multiple_of(t * tm, SUBLANES), 0, tm)])
        return
    n_pure, head = divmod(rows_first, tm)

    @pl.when(t < n_pure)
    def _():
        go([(hbm_refs[0], pl.multiple_of(t * tm, SUBLANES), 0, tm)])

    @pl.when(t >= n_pure)
    def _():
        go([(hbm_refs[0], n_pure * tm, 0, head), (hbm_refs[1], 0, head, tm - head)])


def _ffn_body(*refs, n_in, n_out, final_norm, emit_norm, has_xn, n_tiles, rows_first_in, rows_first_out):
    refs = list(refs)
    x_refs = [refs.pop(0) for _ in range(n_in)]
    xni_ref = refs.pop(0) if has_xn else None
    nw_ref, w1_ref, w3_ref, w2_ref, fn_ref = [refs.pop(0) for _ in range(5)]
    o_refs = [refs.pop(0) for _ in range(n_out)]
    xno_ref = refs.pop(0) if emit_norm else None
    acc_ref, xn_ref, in_sem, out_sem = refs
    i = pl.program_id(0)
    j = pl.program_id(1)
    nj = pl.num_programs(1)
    tm = acc_ref.shape[1]
    slot = i % 2
    other = 1 - slot
    acc = acc_ref.at[slot]
    dma_in = functools.partial(_tile_dma, hbm_refs=x_refs, to_hbm=False, rows_first=rows_first_in)
    dma_out = functools.partial(_tile_dma, hbm_refs=o_refs, to_hbm=True, rows_first=rows_first_out)

    @pl.when(j == 0)
    def _():
        @pl.when(i == 0)
        def _():
            dma_in("start", i, buf=acc, sem=in_sem.at[slot])

        dma_in("wait", i, buf=acc, sem=in_sem.at[slot])
        if not has_xn:
            _rmsnorm_rows(acc, nw_ref, xn_ref, tm)

    xn = (xni_ref if has_xn else xn_ref)[...]
    h1 = _dot(xn, w1_ref[...].astype(BF16))
    h3 = _dot(xn, w3_ref[...].astype(BF16))
    g = ((0.5 * _silu(h1)) * h3).astype(BF16)
    acc[...] += _dot(g, w2_ref[...].astype(BF16))

    @pl.when((j == nj - 2) & (i + 1 < n_tiles))
    def _():
        @pl.when(i >= 1)
        def _():
            dma_out("wait", i - 1, buf=acc_ref.at[other], sem=out_sem.at[other])

        dma_in("start", i + 1, buf=acc_ref.at[other], sem=in_sem.at[other])

    @pl.when(j == nj - 1)
    def _():
        if final_norm:
            _rmsnorm_rows(acc, fn_ref, acc, tm)
        if emit_norm:
            _rmsnorm_rows(acc, fn_ref, xno_ref, tm)
        dma_out("start", i, buf=acc, sem=out_sem.at[slot])

        @pl.when(i == n_tiles - 1)
        def _():
            if n_tiles > 1:
                dma_out("wait", i - 1, buf=acc_ref.at[other], sem=out_sem.at[other])
            dma_out("wait", i, buf=acc, sem=out_sem.at[slot])


def _ffn(xs, norm_w, w1, w3, w2, *, xn=None, final_w=None, next_norm_w=None, split_rows=None):
    d, f = w1.shape
    tm, tf = TOKEN_TILE, FFN_COLS
    n = sum(x.shape[0] for x in xs)
    assert n % tm == 0 and f % tf == 0 and f // tf >= 2
    n_tiles = n // tm
    for first in ([xs[0].shape[0]] if len(xs) == 2 else []) + ([split_rows] if split_rows else []):
        assert first % tm and (first // tm + 1) * tm == n
    final_norm = final_w is not None
    emit_norm = next_norm_w is not None
    assert not (final_norm and emit_norm)
    fw = final_w if final_norm else (next_norm_w if emit_norm else norm_w)
    has_xn = xn is not None
    any_spec = pl.BlockSpec(memory_space=pl.ANY)
    in_specs = [any_spec for _ in xs] + ([pl.BlockSpec((tm, d), lambda i, j: (i, 0))] if has_xn else [])
    if split_rows:
        o_specs = [any_spec, any_spec]
        o_shapes = [jax.ShapeDtypeStruct((split_rows, d), F32), jax.ShapeDtypeStruct((n - split_rows, d), F32)]
    else:
        o_specs = [any_spec]
        o_shapes = [jax.ShapeDtypeStruct((n, d), F32)]
    n_out = len(o_specs)
    if emit_norm:
        o_specs.append(pl.BlockSpec((tm, d), lambda i, j: (i, 0), **_SINGLE))
        o_shapes.append(jax.ShapeDtypeStruct((n, d), BF16))
    return pl.pallas_call(
        functools.partial(_ffn_body, n_in=len(xs), n_out=n_out, final_norm=final_norm, emit_norm=emit_norm,
                          has_xn=has_xn, n_tiles=n_tiles, rows_first_in=xs[0].shape[0], rows_first_out=split_rows),
        grid=(n_tiles, f // tf),
        in_specs=in_specs + [
            pl.BlockSpec((1, d), lambda i, j: (0, 0)),
            pl.BlockSpec((d, tf), lambda i, j: (0, j)),
            pl.BlockSpec((d, tf), lambda i, j: (0, j)),
            pl.BlockSpec((tf, d), lambda i, j: (j, 0)),
            pl.BlockSpec((1, d), lambda i, j: (0, 0)),
        ],
        out_specs=o_specs,
        out_shape=o_shapes,
        scratch_shapes=[
            pltpu.VMEM((2, tm, d), F32),
            pltpu.VMEM((SUBLANES, LANES) if has_xn else (tm, d), BF16),
            pltpu.SemaphoreType.DMA((2,)),
            pltpu.SemaphoreType.DMA((2,)),
        ],
        compiler_params=_cparams("arbitrary", "arbitrary"),
        name="ffn",
    )(*xs, *([xn] if has_xn else []), norm_w.reshape(1, d), w1, w3, w2, fw.reshape(1, d))


def _inproj_body(*refs, n_full, tail_rows, n_uvz, n_xbc, n_dt, n_tiles, rounds):
    nr = len(rounds)
    xn_ref, w_ref, wdt_ref = refs[:3]
    rin_refs = refs[3:3 + nr]
    p1_ref, xbc_ref, dt_ref = refs[3 + nr:6 + nr]
    rout_refs = refs[6 + nr:6 + 2 * nr]
    wb_ref = refs[6 + 2 * nr]
    j = pl.program_id(0)

    step = j * n_tiles + pl.program_id(1)
    for nblk, rin, rout in zip(rounds, rin_refs, rout_refs):
        @pl.when(step < nblk)
        def _(rin=rin, rout=rout):
            rout[...] = rin[...].astype(rout.dtype)

    is_xbc = (j >= n_uvz) & (j < n_uvz + n_xbc)

    @pl.when(pl.program_id(1) == 0)
    def _():
        wb_ref[...] = w_ref[...].astype(BF16)

    def run(is_tail, rows):
        @pl.when(j == 0)
        def _():
            dt = _dot_nt(xn_ref[0:rows, :], wdt_ref[...].astype(BF16))
            lane = _iota2(dt.shape, 1)
            dt_ref[0:rows, :] = jnp.where(lane < n_dt, dt, 0.0)

        @pl.when(jnp.logical_not(is_xbc))
        def _():
            p1_ref[0:rows, :] = _dot_nt(xn_ref[0:rows, :], wb_ref[...]).astype(p1_ref.dtype)

        @pl.when(is_xbc)
        def _():
            xbc_ref[0:rows, :] = _dot_nt(xn_ref[0:rows, :], wb_ref[...])

    _tile_branches(n_full, TOKEN_TILE, tail_rows, run, axis=1)


def _inproj(xn, w_t, c_uvz, c_xbc, c_dt, round_ws=()):
    n, d = xn.shape
    tm, tn = TOKEN_TILE, PROJ_COLS
    n_full, tail_rows = divmod(n, tm)
    n_tiles = n_full + (1 if tail_rows else 0)
    last_i = n_tiles - 1
    c_gate0 = c_uvz + c_xbc + c_dt
    c_gates = w_t.shape[0] - c_gate0
    n_uvz, n_xbc, n_g = c_uvz // tn, c_xbc // tn, c_gates // tn
    n_main = n_uvz + n_xbc
    assert c_gate0 % SUBLANES == 0 and (c_uvz + c_xbc) % SUBLANES == 0

    def w_row(j, i):
        return (pl.multiple_of(jnp.where(j < n_main, j * tn, c_gate0 + (j - n_main) * tn), SUBLANES), 0)

    def p1_idx(j, i):
        writes = (j < n_uvz) | (j >= n_main)
        col = jnp.where(j < n_uvz, j, jnp.maximum(j - n_xbc, n_uvz - 1))
        return (jnp.where(writes, i, last_i), col)

    def xbc_idx(j, i):
        row = jnp.where(j < n_uvz, 0, jnp.where(j < n_main, i, last_i))
        return (row, jnp.clip(j - n_uvz, 0, n_xbc - 1))

    n_steps = (n_main + n_g) * n_tiles
    round_rows = []
    for w in round_ws:
        rr = next(r for r in range(ROUND_ROW_ALIGN, w.shape[0] + 1, ROUND_ROW_ALIGN)
                  if w.shape[0] % r == 0 and w.shape[0] // r <= n_steps)
        round_rows.append(rr)
    rounds = [w.shape[0] // rr for w, rr in zip(round_ws, round_rows)]

    def r_specs():
        return [pl.BlockSpec((rr, w.shape[1]),
                             lambda j, i, nblk=nblk: (jnp.minimum(j * n_tiles + i, nblk - 1), 0))
                for w, rr, nblk in zip(round_ws, round_rows, rounds)]

    return pl.pallas_call(
        functools.partial(_inproj_body, n_full=n_full, tail_rows=tail_rows, n_uvz=n_uvz, n_xbc=n_xbc, n_dt=c_dt,
                          n_tiles=n_tiles, rounds=tuple(rounds)),
        grid=(n_main + n_g, n_tiles),
        in_specs=[
            pl.BlockSpec((tm, d), lambda j, i: (i, 0)),
            pl.BlockSpec((pl.Element(tn), pl.Element(d)), w_row),
            pl.BlockSpec((pl.Element(LANES), pl.Element(d)), lambda j, i: (c_uvz + c_xbc, 0)),
        ] + r_specs(),
        out_specs=[
            pl.BlockSpec((tm, tn), p1_idx),
            pl.BlockSpec((tm, tn), xbc_idx),
            pl.BlockSpec((tm, LANES), lambda j, i: (jnp.where(j == 0, i, last_i), 0)),
        ] + r_specs(),
        out_shape=[
            jax.ShapeDtypeStruct((n, c_uvz + c_gates), BF16),
            jax.ShapeDtypeStruct((n, c_xbc), F32),
            jax.ShapeDtypeStruct((n, LANES), F32),
        ] + [jax.ShapeDtypeStruct(w.shape, BF16) for w in round_ws],
        scratch_shapes=[pltpu.VMEM((tn, d), BF16)],
        compiler_params=_cparams("arbitrary", "arbitrary"),
        name="in_proj",
    )(xn, w_t, w_t, *round_ws)


def _merge_body(x_ref, yap_ref, ybp_ref, yas_ref, ybs_ref, ga_ref, gb_ref, bg_ref, wa_ref, wb_ref, wo_ref, nw_ref,
                o_ref, xn_ref, *, n_prompt_tiles):
    def run(ya_ref, yb_ref):
        pa = _dot(ya_ref[...], wa_ref[...])
        pb = _dot(yb_ref[...], wb_ref[...])
        ga = jax.nn.sigmoid(ga_ref[...].astype(F32) + bg_ref[0:1, :])
        gb = jax.nn.sigmoid(gb_ref[...].astype(F32) + bg_ref[1:2, :])
        m = (ga * pa + gb * pb).astype(BF16)
        o = x_ref[...] + _dot(m, wo_ref[...])
        o_ref[...] = o
        r = lax.rsqrt(jnp.mean(o * o, axis=-1, keepdims=True) + EPS)
        xn_ref[...] = ((o * r) * nw_ref[...]).astype(xn_ref.dtype)

    i = pl.program_id(0)

    @pl.when(i < n_prompt_tiles)
    def _():
        run(yap_ref, ybp_ref)

    @pl.when(i >= n_prompt_tiles)
    def _():
        run(yas_ref, ybs_ref)


def _merge(x, ya_p, yb_p, ya_s, yb_s, p1, b_gate, wa, wb, wo, next_norm_w):
    n, d = x.shape
    tm = MERGE_TILE
    npt = ya_p.shape[0] // tm
    nst = ya_s.shape[0] // tm
    gcol = p1.shape[1] // d - 2
    p_idx = lambda i: (jnp.minimum(i, npt - 1), 0)
    s_idx = lambda i: (jnp.clip(i - npt, 0, nst - 1), 0)
    return pl.pallas_call(
        functools.partial(_merge_body, n_prompt_tiles=npt),
        grid=(n // tm,),
        in_specs=[
            pl.BlockSpec((tm, d), lambda i: (i, 0)),
            pl.BlockSpec((tm, ya_p.shape[1]), p_idx),
            pl.BlockSpec((tm, yb_p.shape[1]), p_idx),
            pl.BlockSpec((tm, ya_s.shape[1]), s_idx),
            pl.BlockSpec((tm, yb_s.shape[1]), s_idx),
            pl.BlockSpec((tm, d), lambda i: (i, gcol)),
            pl.BlockSpec((tm, d), lambda i: (i, gcol + 1)),
            pl.BlockSpec((2, d), lambda i: (0, 0)),
            pl.BlockSpec(wa.shape, lambda i: (0, 0), **_SINGLE),
            pl.BlockSpec(wb.shape, lambda i: (0, 0), **_SINGLE),
            pl.BlockSpec(wo.shape, lambda i: (0, 0), **_SINGLE),
            pl.BlockSpec((1, d), lambda i: (0, 0)),
        ],
        out_specs=[pl.BlockSpec((tm, d), lambda i: (i, 0)), pl.BlockSpec((tm, d), lambda i: (i, 0))],
        out_shape=[jax.ShapeDtypeStruct((n, d), F32), jax.ShapeDtypeStruct((n, d), BF16)],
        compiler_params=_cparams("arbitrary"),
        name="merge",
    )(x, ya_p, yb_p, ya_s, yb_s, p1, p1, b_gate, wa, wb, wo, next_norm_w.reshape(1, d))


def _iota2(shape, dim):
    return lax.broadcasted_iota(jnp.int32, shape, dim)


def _block_masks(seq_len):
    tb = TOKEN_BLOCK
    row = _iota2((tb, tb), 0)
    col = _iota2((tb, tb), 1)
    if seq_len >= tb:
        causal = col <= row
        last = col == tb - 1
    else:
        causal = (col <= row) & ((row // seq_len) == (col // seq_len))
        last = col == (row // seq_len) * seq_len + (seq_len - 1)
    return causal, last


def _init_constants(wm_ref, e_ref, gw_ref, causal):
    for h in range(GM_HEADS):
        wm_ref[h] = jnp.where(causal, gw_ref[h], 0.0).astype(BF16)
    hrow = _iota2(e_ref.shape, 0)
    ccol = _iota2(e_ref.shape, 1)
    e_ref[...] = jnp.where(hrow == ccol // SSD_HEAD_DIM, 1.0, 0.0).astype(BF16)


def _gating(u_ref, v_ref, gvn_ref, bs_ref, wm_ref, ya_ref, vn_ref):
    vg = _gelu(v_ref[...].astype(F32))
    r = lax.rsqrt(jnp.mean(vg * vg, axis=-1, keepdims=True) + EPS)
    vn = (vg * r) * gvn_ref[...]
    if vn_ref is not None:
        vn_ref[...] = vn
    vnb = vn.astype(BF16)
    for h in range(GM_HEADS):
        cs = slice(h * LANES, (h + 1) * LANES)
        s = _dot(wm_ref[h], vnb[:, cs]) + bs_ref[:, h:h + 1]
        ya_ref[:, cs] = (_gelu(u_ref[:, cs].astype(F32)) * s).astype(ya_ref.dtype)


def _conv(xp_ref, cw_ref, cb_ref, xc_ref, cst_ref, seq_len):
    tb = TOKEN_BLOCK
    width = xc_ref.shape[1]
    cw = CHUNK_COLS
    ns = CONV_W - 1
    place = None
    if cst_ref is not None:
        nseq = cst_ref.shape[1]
        tpos = _iota2((tb, 1), 0) % seq_len
        row = _iota2((tb, nseq), 0)
        col = _iota2((tb, nseq), 1)
        place = {(d, r): jnp.where((col == row // seq_len) & (row % seq_len == r + d - ns), 1.0, 0.0).astype(BF16)
                 for d in range(1, CONV_W) for r in range(ns - d, ns)}
    for c0 in range(0, width, cw):
        cs = slice(c0, c0 + cw)
        acc = None
        full = xp_ref[:, cs]
        if place is not None:
            pieces = [_split3(cst_ref[r, :, cs]) for r in range(ns)]
        for k in range(CONV_W):
            d = CONV_W - 1 - k
            xs = (pltpu.roll(full, d, axis=0) if d else full)[SUBLANES:SUBLANES + tb, :]
            if place is not None and d > 0:
                xs = jnp.where(tpos >= d, xs, 0.0)
                for r in range(ns - d, ns):
                    pd = place[(d, r)]
                    xs = xs + ((_dot(pd, pieces[r][0]) + _dot(pd, pieces[r][1])) + _dot(pd, pieces[r][2]))
            term = cw_ref[k:k + 1, cs] * xs
            acc = term if acc is None else acc + term
        xc_ref[:, cs] = _silu(cb_ref[:, cs] + acc)


def _ssd_prepare(dt_ref, dtb_ref, alog_ref, e_ref, causal, last, dtf_ref, tef_ref, eaf_ref):
    dt = _softplus(dt_ref[...] + dtb_ref[...])
    a = -jnp.exp(alog_ref[...])
    da = dt * a
    lmat = jnp.where(causal, 1.0, 0.0).astype(BF16)
    acum = _sel_left(lmat, da)
    a_last = _sel_left(jnp.where(last, 1.0, 0.0).astype(BF16), acum)
    ea = jnp.exp(acum)
    e = e_ref[...]
    dtf_ref[...] = _sel_right(dt, e)
    tef_ref[...] = _sel_right(dt * jnp.exp(a_last - acum), e)
    eaf_ref[...] = _sel_right(ea, e)
    return acum, acum.T


def _ssd_diag_group(g, xc_ref, dtf_ref, acum, acum_t, causal):
    tb = TOKEN_BLOCK
    width = xc_ref.shape[1] - 2 * SSD_GROUPS * D_STATE
    gw = width // SSD_GROUPS
    bg = xc_ref[:, width + g * D_STATE:width + (g + 1) * D_STATE]
    cg = xc_ref[:, width + (SSD_GROUPS + g) * D_STATE:width + (SSD_GROUPS + g + 1) * D_STATE].astype(BF16)
    cb = jnp.where(causal, _dot_nt(cg, bg.astype(BF16)), 0.0)
    lane = _iota2((tb, LANES), 1)
    heads_per_group = gw // SSD_HEAD_DIM
    outs = []
    for jp in range(heads_per_group // 2):
        ms = []
        for hh in range(2):
            h = g * heads_per_group + 2 * jp + hh
            seg = jnp.minimum(acum[:, h:h + 1] - acum_t[h:h + 1, :], 0.0)
            ms.append((cb * jnp.exp(seg)).astype(BF16))
        lhs = jnp.concatenate(ms, axis=1)
        ps = slice(g * gw + jp * LANES, g * gw + (jp + 1) * LANES)
        xpair = (xc_ref[:, ps] * dtf_ref[:, ps]).astype(BF16)
        zero = jnp.zeros_like(xpair)
        rhs = jnp.concatenate([jnp.where(lane < SSD_HEAD_DIM, xpair, zero),
                               jnp.where(lane >= SSD_HEAD_DIM, xpair, zero)], axis=0)
        outs.append(_dot(lhs, rhs))
    return jnp.concatenate(outs, axis=1), cg, bg


def _finish_group(g, y, xc_ref, z_ref, dsk_ref, sn_ref, yb_ref):
    gw = yb_ref.shape[1] // SSD_GROUPS
    gs = slice(g * gw, (g + 1) * gw)
    y = y + dsk_ref[:, gs] * xc_ref[:, gs]
    yz = y * _silu(z_ref[:, gs].astype(F32))
    r = lax.rsqrt(jnp.mean(yz * yz, axis=-1, keepdims=True) + EPS)
    yb_ref[:, gs] = ((yz * r) * sn_ref[:, gs]).astype(yb_ref.dtype)


def _prompt_body(*refs, n_par):
    refs = list(refs)
    ins = [[refs.pop(0) for _ in range(5)] for _ in range(n_par)]
    gvn_ref, gw_ref, bs_ref, cw_ref, cb_ref, dtb_ref, alog_ref, dsk_ref, sn_ref = [refs.pop(0) for _ in range(9)]
    out4 = [refs.pop(0) for _ in range(4)]
    outs = [[o.at[p] for o in out4] for p in range(n_par)]
    wm_ref, e_ref = refs.pop(0), refs.pop(0)
    scr = [[refs.pop(0) for _ in range(6)] for _ in range(n_par)]
    tb = TOKEN_BLOCK
    c = pl.program_id(1)
    causal, last = _block_masks(tb)

    @pl.when((pl.program_id(0) == 0) & (c == 0))
    def _():
        _init_constants(wm_ref, e_ref, gw_ref, causal)

    @pl.when(c == 0)
    def _():
        for xp_ref, _, _, _, _, st_ref in scr:
            st_ref[...] = jnp.zeros_like(st_ref)
            xp_ref[0:SUBLANES, :] = jnp.zeros((SUBLANES, xp_ref.shape[1]), F32)

    for (u_ref, v_ref, z_ref, xbc_ref, dt_ref), (ya_ref, yb_ref, _, _), \
            (xp_ref, xc_ref, dtf_ref, tef_ref, eaf_ref, st_ref) in zip(ins, outs, scr):
        _gating(u_ref, v_ref, gvn_ref, bs_ref, wm_ref, ya_ref, None)

        xp_ref[SUBLANES:SUBLANES + tb, :] = xbc_ref[...]
        _conv(xp_ref, cw_ref, cb_ref, xc_ref, None, tb)
        xp_ref[0:SUBLANES, :] = xbc_ref[tb - SUBLANES:tb, :]

        acum, acum_t = _ssd_prepare(dt_ref, dtb_ref, alog_ref, e_ref, causal, last, dtf_ref, tef_ref, eaf_ref)

        gw = yb_ref.shape[1] // SSD_GROUPS
        for g in range(SSD_GROUPS):
            gs = slice(g * gw, (g + 1) * gw)
            yd, cg, bg = _ssd_diag_group(g, xc_ref, dtf_ref, acum, acum_t, causal)
            st = st_ref[:, gs]
            y = yd + _dot(cg, st.astype(BF16)) * eaf_ref[:, gs]
            _finish_group(g, y, xc_ref, z_ref, dsk_ref, sn_ref, yb_ref)
            xw = (xc_ref[:, gs] * tef_ref[:, gs]).astype(BF16)
            st_ref[:, gs] = eaf_ref[tb - 1:tb, gs] * st + _dot(bg.T.astype(BF16), xw)

    @pl.when(c == pl.num_programs(1) - 1)
    def _():
        for (_, _, _, xbc_ref, _), (_, _, hs_ref, ct_ref), (_, _, _, _, _, st_ref) in zip(ins, outs, scr):
            for k in range(st_ref.shape[1] // LANES):
                hs_ref[k * LANES:(k + 1) * LANES, :] = st_ref[:, k * LANES:(k + 1) * LANES].T
            ct_ref[...] = xbc_ref[tb - SUBLANES:tb, :]


def _sample_body(u_ref, v_ref, z_ref, xbc_ref, dt_ref, cst_ref, h0_ref,
                 gvn_ref, gw_ref, bs_ref, cw_ref, cb_ref, dtb_ref, alog_ref, dsk_ref, sn_ref,
                 ya_ref, yb_ref, vn_ref, hs_ref, cs_ref,
                 wm_ref, e_ref, xp_ref, xc_ref, dtf_ref, tef_ref, eaf_ref,
                 y_ref, yoff_ref, cg_ref, bg_ref, xwt_ref, eat_ref, cdh_ref, *, seq_len):
    tb = TOKEN_BLOCK
    k = pl.program_id(1)
    causal, last = _block_masks(seq_len)
    gw = yb_ref.shape[1] // SSD_GROUPS

    @pl.when((pl.program_id(0) == 0) & (k == 0))
    def _():
        _init_constants(wm_ref, e_ref, gw_ref, causal)
        xp_ref[0:SUBLANES, :] = jnp.zeros((SUBLANES, xp_ref.shape[1]), F32)

    @pl.when(k == 0)
    def _():
        _gating(u_ref, v_ref, gvn_ref, bs_ref, wm_ref, ya_ref, vn_ref)
        xp_ref[SUBLANES:SUBLANES + tb, :] = xbc_ref[...]
        _conv(xp_ref, cw_ref, cb_ref, xc_ref, cst_ref, seq_len)
        nsb = tb // seq_len
        pick = [jnp.where(_iota2((nsb, tb), 1) == _iota2((nsb, tb), 0) * seq_len + (seq_len - (CONV_W - 1) + r),
                          1.0, 0.0).astype(BF16) for r in range(CONV_W - 1)]
        for c0 in range(0, xbc_ref.shape[1], CHUNK_COLS):
            pieces = _split3(xbc_ref[:, c0:c0 + CHUNK_COLS])
            for r in range(CONV_W - 1):
                cs_ref[r, :, c0:c0 + CHUNK_COLS] = ((_dot(pick[r], pieces[0]) + _dot(pick[r], pieces[1]))
                                             + _dot(pick[r], pieces[2]))
        acum, acum_t = _ssd_prepare(dt_ref, dtb_ref, alog_ref, e_ref, causal, last, dtf_ref, tef_ref, eaf_ref)
        eat_ref[...] = jnp.exp(acum_t)
        for g in range(SSD_GROUPS):
            gs = slice(g * gw, (g + 1) * gw)
            yd, cg, bg = _ssd_diag_group(g, xc_ref, dtf_ref, acum, acum_t, causal)
            y_ref[:, gs] = yd
            cg_ref[:, g * D_STATE:(g + 1) * D_STATE] = cg.astype(F32)
            bg_ref[:, g * D_STATE:(g + 1) * D_STATE] = bg
            xw = xc_ref[:, gs] * tef_ref[:, gs]
            for q in range(gw // LANES):
                r0 = g * gw + q * LANES
                xwt_ref[r0:r0 + LANES, :] = xw[:, q * LANES:(q + 1) * LANES].T.astype(BF16)

    nseq = h0_ref.shape[0]
    nt = nseq * seq_len
    rows = pl.ds(pl.multiple_of(k * nt, nt), nt)
    rowseq = _iota2((tb, 1), 0) // seq_len
    subseq = _iota2((nt, 1), 0) // seq_len
    tok = _iota2((tb, LANES), 0)
    onehot = jnp.concatenate(
        [jnp.where(tok == (k * nseq + b) * seq_len + (seq_len - 1), 1.0, 0.0) for b in range(nseq)],
        axis=1).astype(BF16)
    cdh_ref[...] = _sel_right(eat_ref[...], onehot)
    hpg = gw // SSD_HEAD_DIM
    for g in range(SSD_GROUPS):
        gs = slice(g * gw, (g + 1) * gw)
        ds = slice(g * D_STATE, (g + 1) * D_STATE)
        c_sub = cg_ref[rows, ds]
        lhs = jnp.concatenate([jnp.where(subseq == b, c_sub, 0.0) for b in range(nseq)], axis=1).astype(BF16)
        h0cat = jnp.concatenate([h0_ref[b, gs, :].astype(BF16) for b in range(nseq)], axis=1)
        yoff_ref[rows, gs] = _dot_nt(lhs, h0cat)
        b_all = bg_ref[:, ds]
        bm = jnp.concatenate([jnp.where(rowseq == k * nseq + b, b_all, 0.0) for b in range(nseq)],
                             axis=1).astype(BF16)
        s_new = _dot(xwt_ref[gs, :], bm)
        for b in range(nseq):
            ls = slice(b * D_STATE, (b + 1) * D_STATE)
            for j in range(hpg):
                h = g * hpg + j
                r0 = g * gw + j * SSD_HEAD_DIM
                hs_ref[b, r0:r0 + SSD_HEAD_DIM, :] = (cdh_ref[h:h + 1, ls] * h0_ref[b, r0:r0 + SSD_HEAD_DIM, :]
                                                      + s_new[j * SSD_HEAD_DIM:(j + 1) * SSD_HEAD_DIM, ls])

    @pl.when(k == pl.num_programs(1) - 1)
    def _():
        for g in range(SSD_GROUPS):
            gs = slice(g * gw, (g + 1) * gw)
            y = y_ref[:, gs] + yoff_ref[:, gs] * eaf_ref[:, gs]
            _finish_group(g, y, xc_ref, z_ref, dsk_ref, sn_ref, yb_ref)


def _mixer_params(lp):
    (gm_v_norm, gm_ws, gm_bs, conv_w, conv_b, dt_bias, a_log, d_skip, ssd_norm) = lp
    nh = dt_bias.shape[0]
    pad = LANES - nh
    return dict(
        gvn=gm_v_norm.reshape(1, -1),
        cw=conv_w,
        cb=conv_b.reshape(1, -1),
        dtb=jnp.pad(dt_bias, (0, pad)).reshape(1, LANES),
        alog=jnp.pad(a_log, (0, pad)).reshape(1, LANES),
        dsk=jnp.repeat(d_skip, SSD_HEAD_DIM).reshape(1, -1),
        sn=ssd_norm.reshape(1, -1),
    )


def _full_spec(a):
    nd = a.ndim
    return pl.BlockSpec(a.shape, lambda i, j: (0,) * nd)


def _mixer_prompt(p1, xbc, dt, lp, n_seq, seq_len, gm_w, ssd_w, conv_dim):
    tb = TOKEN_BLOCK
    nc = seq_len // tb
    n_par = PROMPT_SEQS_PER_STEP if n_seq % PROMPT_SEQS_PER_STEP == 0 else 1
    mp = _mixer_params(lp)
    gw_full = lp[1]
    bs_t = lp[2].T
    params = [mp["gvn"], gw_full, bs_t, mp["cw"], mp["cb"], mp["dtb"], mp["alog"], mp["dsk"], mp["sn"]]
    in_specs, scratch, operands = [], [], []
    for p in range(n_par):
        row = lambda b, c, p=p: (b * n_par + p) * nc + c
        in_specs += [
            pl.BlockSpec((tb, gm_w), lambda b, c, row=row: (row(b, c), 0)),
            pl.BlockSpec((tb, gm_w), lambda b, c, row=row: (row(b, c), 1)),
            pl.BlockSpec((tb, ssd_w), lambda b, c, row=row: (row(b, c), 1)),
            pl.BlockSpec((tb, conv_dim), lambda b, c, row=row: (row(b, c), 0)),
            pl.BlockSpec((tb, LANES), lambda b, c, row=row: (row(b, c), 0)),
        ]
        operands += [p1, p1, p1, xbc, dt]
        scratch += [
            pltpu.VMEM((SUBLANES + tb, conv_dim), F32),
            pltpu.VMEM((tb, conv_dim), F32),
            pltpu.VMEM((tb, ssd_w), F32),
            pltpu.VMEM((tb, ssd_w), F32),
            pltpu.VMEM((tb, ssd_w), F32),
            pltpu.VMEM((D_STATE, ssd_w), F32),
        ]
    res = pl.pallas_call(
        functools.partial(_prompt_body, n_par=n_par),
        grid=(n_seq // n_par, nc),
        in_specs=in_specs + [_full_spec(a) for a in params],
        out_specs=[
            pl.BlockSpec((n_par, tb, gm_w), lambda b, c: (b, c, 0)),
            pl.BlockSpec((n_par, tb, ssd_w), lambda b, c: (b, c, 0)),
            pl.BlockSpec((n_par, ssd_w, D_STATE), lambda b, c: (b, 0, 0)),
            pl.BlockSpec((n_par, SUBLANES, conv_dim), lambda b, c: (b, 0, 0)),
        ],
        out_shape=[
            jax.ShapeDtypeStruct((n_seq, seq_len, gm_w), BF16),
            jax.ShapeDtypeStruct((n_seq, seq_len, ssd_w), BF16),
            jax.ShapeDtypeStruct((n_seq, ssd_w, D_STATE), F32),
            jax.ShapeDtypeStruct((n_seq, SUBLANES, conv_dim), F32),
        ],
        scratch_shapes=[
            pltpu.VMEM((GM_HEADS, tb, tb), BF16),
            pltpu.VMEM((LANES, ssd_w), BF16),
        ] + scratch,
        compiler_params=_cparams("arbitrary", "arbitrary"),
        name="mixer_prompt",
    )(*operands, *params)
    ya, yb, hs, ct = res
    return ya.reshape(n_seq * seq_len, gm_w), yb.reshape(n_seq * seq_len, ssd_w), hs, ct


def _mixer_sample(p1, xbc, dt, lp, h0, conv_state, row0, n_seq, seq_len, gm_w, ssd_w, conv_dim):
    tb = TOKEN_BLOCK
    seqs_per_block = tb // seq_len
    nblk = n_seq // seqs_per_block
    nsub = seqs_per_block // SAMPLE_SEQS_PER_STEP
    blk0 = row0 // tb
    mp = _mixer_params(lp)
    reps = tb // seq_len
    gw_tiled = jnp.tile(lp[1][:, :seq_len, :seq_len], (1, reps, reps))
    bs_t = jnp.tile(lp[2][:, :seq_len], (1, reps)).T
    params = [mp["gvn"], gw_tiled, bs_t, mp["cw"], mp["cb"], mp["dtb"], mp["alog"], mp["dsk"], mp["sn"]]
    ntok = n_seq * seq_len
    return pl.pallas_call(
        functools.partial(_sample_body, seq_len=seq_len),
        grid=(nblk, nsub),
        in_specs=[
            pl.BlockSpec((tb, gm_w), lambda i, k: (blk0 + i, 0)),
            pl.BlockSpec((tb, gm_w), lambda i, k: (blk0 + i, 1)),
            pl.BlockSpec((tb, ssd_w), lambda i, k: (blk0 + i, 1)),
            pl.BlockSpec((tb, conv_dim), lambda i, k: (blk0 + i, 0)),
            pl.BlockSpec((tb, LANES), lambda i, k: (blk0 + i, 0)),
            pl.BlockSpec((CONV_W - 1, seqs_per_block, conv_dim), lambda i, k: (0, i, 0)),
            pl.BlockSpec((SAMPLE_SEQS_PER_STEP, ssd_w, D_STATE), lambda i, k: (i * nsub + k, 0, 0)),
        ] + [_full_spec(a) for a in params],
        out_specs=[
            pl.BlockSpec((tb, gm_w), lambda i, k: (i, 0)),
            pl.BlockSpec((tb, ssd_w), lambda i, k: (i, 0)),
            pl.BlockSpec((tb, gm_w), lambda i, k: (i, 0)),
            pl.BlockSpec((SAMPLE_SEQS_PER_STEP, ssd_w, D_STATE), lambda i, k: (i * nsub + k, 0, 0)),
            pl.BlockSpec((CONV_W - 1, seqs_per_block, conv_dim), lambda i, k: (0, i, 0)),
        ],
        out_shape=[
            jax.ShapeDtypeStruct((ntok, gm_w), BF16),
            jax.ShapeDtypeStruct((ntok, ssd_w), BF16),
            jax.ShapeDtypeStruct((ntok, gm_w), F32),
            jax.ShapeDtypeStruct((n_seq, ssd_w, D_STATE), F32),
            jax.ShapeDtypeStruct((CONV_W - 1, n_seq, conv_dim), F32),
        ],
        scratch_shapes=[
            pltpu.VMEM((GM_HEADS, tb, tb), BF16),
            pltpu.VMEM((LANES, ssd_w), BF16),
            pltpu.VMEM((SUBLANES + tb, conv_dim), F32),
            pltpu.VMEM((tb, conv_dim), F32),
            pltpu.VMEM((tb, ssd_w), F32),
            pltpu.VMEM((tb, ssd_w), F32),
            pltpu.VMEM((tb, ssd_w), F32),
            pltpu.VMEM((tb, ssd_w), F32),
            pltpu.VMEM((tb, ssd_w), F32),
            pltpu.VMEM((tb, SSD_GROUPS * D_STATE), F32),
            pltpu.VMEM((tb, SSD_GROUPS * D_STATE), F32),
            pltpu.VMEM((ssd_w, tb), BF16),
            pltpu.VMEM((LANES, tb), F32),
            pltpu.VMEM((LANES, SAMPLE_SEQS_PER_STEP * LANES), F32),
        ],
        compiler_params=_cparams("arbitrary", "arbitrary"),
        name="mixer_sample",
    )(p1, p1, p1, xbc, dt, conv_state, h0, *params)


def kernel(x_prompt, x_sample, state_ssm, state_conv, ffn1_norm, ffn1_w1, ffn1_w3, ffn1_w2, mix_norm, w_in,
           b_gate, conv_w, conv_b, dt_bias, a_log, d_skip, ssd_norm, gm_v_norm, gm_ws, gm_bs, w_proj_a,
           w_proj_b, w_out, ffn2_norm, ffn2_w1, ffn2_w3, ffn2_w2, final_norm):
    bp, tp, d = x_prompt.shape
    bs_, ts, _ = x_sample.shape
    depth = w_in.shape[0]
    n_heads = dt_bias.shape[1]
    gm_w = gm_v_norm.shape[1]
    ssd_w = ssd_norm.shape[1]
    conv_dim = conv_w.shape[2]
    np_, ns = bp * tp, bs_ * ts

    xs = [x_prompt.reshape(np_, d), x_sample.reshape(ns, d)]
    ssm_p, conv_p, ssm_s, conv_s, v_s = [], [], [], [], []
    for l in range(depth):
        c_uvz = 2 * gm_w + ssd_w
        x, xn = _ffn(xs, ffn1_norm[l], ffn1_w1[l], ffn1_w3[l], ffn1_w2[l], next_norm_w=mix_norm[l])
        p1, xbc, dt, wa_b, wb_b, wo_b, f1_b, f3_b, f2_b = _inproj(
            xn, w_in[l].T, c_uvz, conv_dim, n_heads,
            round_ws=(w_proj_a[l], w_proj_b[l], w_out[l], ffn2_w1[l], ffn2_w3[l], ffn2_w2[l]))

        lp = (gm_v_norm[l], gm_ws[l], gm_bs[l], conv_w[l], conv_b[l], dt_bias[l], a_log[l], d_skip[l], ssd_norm[l])
        ya_p, yb_p, hs_p, ct_p = _mixer_prompt(p1, xbc, dt, lp, bp, tp, gm_w, ssd_w, conv_dim)

        assert ts >= CONV_W - 1
        st = jnp.transpose(state_conv[l], (1, 0, 2))
        ya_s, yb_s, vn_s, hs_s, cs_s = _mixer_sample(p1, xbc, dt, lp, state_ssm[l].reshape(bs_, ssd_w, D_STATE), st,
                                                     np_, bs_, ts, gm_w, ssd_w, conv_dim)

        x, xn = _merge(x, ya_p, yb_p, ya_s, yb_s, p1, b_gate[l], wa_b, wb_b, wo_b, ffn2_norm[l])
        last = l == depth - 1
        xs = _ffn([x], ffn2_norm[l], f1_b, f3_b, f2_b, xn=xn, final_w=final_norm if last else None,
                  split_rows=np_)

        ssm_p.append(hs_p.reshape(bp, n_heads, SSD_HEAD_DIM, D_STATE))
        conv_p.append(ct_p[:, SUBLANES - (CONV_W - 1):, :])
        ssm_s.append(hs_s.reshape(bs_, n_heads, SSD_HEAD_DIM, D_STATE))
        conv_s.append(jnp.transpose(cs_s, (1, 0, 2)))
        v_s.append(vn_s.reshape(bs_, ts, gm_w))

    return (xs[0].reshape(bp, tp, d), xs[1].reshape(bs_, ts, d), jnp.stack(ssm_p), jnp.stack(conv_p),
            jnp.stack(ssm_s), jnp.stack(conv_s), jnp.stack(v_s))
```

```python
import functools

import jax
import jax.numpy as jnp
from jax import lax
from jax.experimental import pallas as pl
from jax.experimental.pallas import tpu as pltpu

F32 = jnp.float32
BF16 = jnp.bfloat16
EPS = 1e-6

LANES = 128
SUBLANES = 8
VMEM_LIMIT_BYTES = 60 * 1024 * 1024

GM_HEADS = 8
SSD_HEAD_DIM = 64
SSD_GROUPS = 4
D_STATE = 128
CONV_W = 4
TOKEN_BLOCK = 128
SAMPLE_SEQS_PER_STEP = 8
PROMPT_SEQS_PER_STEP = 2

TOKEN_TILE = 1088
FFN_COLS = 512
PROJ_COLS = 1024
MERGE_TILE = 256
NORM_ROWS = 272
ROUND_ROW_ALIGN = 16
CHUNK_COLS = 512

_SINGLE = dict(pipeline_mode=pl.Buffered(1))


def _cparams(*sem):
    return pltpu.CompilerParams(dimension_semantics=sem, vmem_limit_bytes=VMEM_LIMIT_BYTES)


_GELU_C = 0.7978845608028654


def _gelu(x):
    t = jnp.tanh(x * (_GELU_C + (_GELU_C * 0.044715) * (x * x)))
    return x * (0.5 + 0.5 * t)


def _silu(x):
    return x * jax.nn.sigmoid(x)


def _softplus(x):
    return jnp.maximum(x, 0.0) + jnp.log1p(jnp.exp(-jnp.abs(x)))


def _split3(a):
    a1 = a.astype(BF16)
    r1 = a - a1.astype(F32)
    a2 = r1.astype(BF16)
    r2 = r1 - a2.astype(F32)
    return a1, a2, r2.astype(BF16)


def _dot(a, b):
    return jnp.dot(a, b, preferred_element_type=F32)


def _dot_nt(a, b):
    return lax.dot_general(a, b, (((1,), (1,)), ((), ())), preferred_element_type=F32)


def _sel_right(a, sel):
    a1, a2, a3 = _split3(a)
    return (_dot(a1, sel) + _dot(a2, sel)) + _dot(a3, sel)


def _sel_left(sel, a):
    a1, a2, a3 = _split3(a)
    return (_dot(sel, a1) + _dot(sel, a2)) + _dot(sel, a3)


def _row_loop(nrows, fn):
    def body(i, c):
        fn(pl.ds(pl.multiple_of(i * NORM_ROWS, NORM_ROWS), NORM_ROWS))
        return c

    lax.fori_loop(0, nrows // NORM_ROWS, body, 0)


def _rmsnorm_rows(x_ref, w_ref, out_ref, nrows):
    def one(sl):
        x = x_ref[sl, :]
        r = lax.rsqrt(jnp.mean(x * x, axis=-1, keepdims=True) + EPS)
        out_ref[sl, :] = ((x * r) * w_ref[...]).astype(out_ref.dtype)

    _row_loop(nrows, one)


def _tile_branches(n_full, tile, tail_rows, run, axis=0):
    i = pl.program_id(axis)

    @pl.when(i < n_full)
    def _():
        run(False, tile)

    if tail_rows:
        @pl.when(i >= n_full)
        def _():
            run(True, tail_rows)


def _tile_dma(action, t, *, hbm_refs, buf, sem, to_hbm, rows_first):
    tm = buf.shape[0]

    def go(pieces):
        for hbm, h0, b0, nr in pieces:
            h = hbm.at[pl.ds(h0, nr), :]
            v = buf.at[pl.ds(b0, nr), :]
            copy = pltpu.make_async_copy(v, h, sem) if to_hbm else pltpu.make_async_copy(h, v, sem)
            getattr(copy, action)()

    if len(hbm_refs) == 1:
        go([(hbm_refs[0], pl.multiple_of(t * tm, SUBLANES), 0, tm)])
        return
    n_pure, head = divmod(rows_first, tm)

    @pl.when(t < n_pure)
    def _():
        go([(hbm_refs[0], pl.multiple_of(t * tm, SUBLANES), 0, tm)])

    @pl.when(t >= n_pure)
    def _():
        go([(hbm_refs[0], n_pure * tm, 0, head), (hbm_refs[1], 0, head, tm - head)])


def _ffn_body(*refs, n_in, n_out, final_norm, emit_norm, has_xn, n_tiles, rows_first_in, rows_first_out):
    refs = list(refs)
    x_refs = [refs.pop(0) for _ in range(n_in)]
    xni_ref = refs.pop(0) if has_xn else None
    nw_ref, w1_ref, w3_ref, w2_ref, fn_ref = [refs.pop(0) for _ in range(5)]
    o_refs = [refs.pop(0) for _ in range(n_out)]
    xno_ref = refs.pop(0) if emit_norm else None
    acc_ref, xn_ref, in_sem, out_sem = refs[:4]
    xnb_ref, xno_sem = refs[4:] if emit_norm else (None, None)
    i = pl.program_id(0)
    j = pl.program_id(1)
    nj = pl.num_programs(1)
    tm = acc_ref.shape[1]
    slot = i % 2
    other = 1 - slot
    acc = acc_ref.at[slot]
    dma_in = functools.partial(_tile_dma, hbm_refs=x_refs, to_hbm=False, rows_first=rows_first_in)
    dma_out = functools.partial(_tile_dma, hbm_refs=o_refs, to_hbm=True, rows_first=rows_first_out)
    dma_xno = functools.partial(_tile_dma, hbm_refs=[xno_ref], buf=xnb_ref, sem=xno_sem, to_hbm=True, rows_first=None)

    @pl.when(j == 0)
    def _():
        @pl.when(i == 0)
        def _():
            dma_in("start", i, buf=acc, sem=in_sem.at[slot])

        dma_in("wait", i, buf=acc, sem=in_sem.at[slot])
        if not has_xn:
            _rmsnorm_rows(acc, nw_ref, xn_ref, tm)

    xn = (xni_ref if has_xn else xn_ref)[...]
    h1 = _dot(xn, w1_ref[...].astype(BF16))
    h3 = _dot(xn, w3_ref[...].astype(BF16))
    g = ((0.5 * _silu(h1)) * h3).astype(BF16)
    acc[...] += _dot(g, w2_ref[...].astype(BF16))

    @pl.when((j == nj - 2) & (i + 1 < n_tiles))
    def _():
        @pl.when(i >= 1)
        def _():
            dma_out("wait", i - 1, buf=acc_ref.at[other], sem=out_sem.at[other])

        dma_in("start", i + 1, buf=acc_ref.at[other], sem=in_sem.at[other])

    @pl.when(j == nj - 1)
    def _():
        if final_norm:
            _rmsnorm_rows(acc, fn_ref, acc, tm)
        if emit_norm:
            @pl.when(i >= 1)
            def _():
                dma_xno("wait", i - 1)

            _rmsnorm_rows(acc, fn_ref, xnb_ref, tm)
            dma_xno("start", i)
        dma_out("start", i, buf=acc, sem=out_sem.at[slot])

        @pl.when(i == n_tiles - 1)
        def _():
            if n_tiles > 1:
                dma_out("wait", i - 1, buf=acc_ref.at[other], sem=out_sem.at[other])
            dma_out("wait", i, buf=acc, sem=out_sem.at[slot])
            if emit_norm:
                dma_xno("wait", i)


def _ffn(xs, norm_w, w1, w3, w2, *, xn=None, final_w=None, next_norm_w=None, split_rows=None):
    d, f = w1.shape
    tm, tf = TOKEN_TILE, FFN_COLS
    n = sum(x.shape[0] for x in xs)
    assert n % tm == 0 and f % tf == 0 and f // tf >= 2
    n_tiles = n // tm
    for first in ([xs[0].shape[0]] if len(xs) == 2 else []) + ([split_rows] if split_rows else []):
        assert first % tm and (first // tm + 1) * tm == n
    final_norm = final_w is not None
    emit_norm = next_norm_w is not None
    assert not (final_norm and emit_norm)
    fw = final_w if final_norm else (next_norm_w if emit_norm else norm_w)
    has_xn = xn is not None
    any_spec = pl.BlockSpec(memory_space=pl.ANY)
    in_specs = [any_spec for _ in xs] + ([pl.BlockSpec((tm, d), lambda i, j: (i, 0))] if has_xn else [])
    if split_rows:
        o_specs = [any_spec, any_spec]
        o_shapes = [jax.ShapeDtypeStruct((split_rows, d), F32), jax.ShapeDtypeStruct((n - split_rows, d), F32)]
    else:
        o_specs = [any_spec]
        o_shapes = [jax.ShapeDtypeStruct((n, d), F32)]
    n_out = len(o_specs)
    if emit_norm:
        o_specs.append(any_spec)
        o_shapes.append(jax.ShapeDtypeStruct((n, d), BF16))
    return pl.pallas_call(
        functools.partial(_ffn_body, n_in=len(xs), n_out=n_out, final_norm=final_norm, emit_norm=emit_norm,
                          has_xn=has_xn, n_tiles=n_tiles, rows_first_in=xs[0].shape[0], rows_first_out=split_rows),
        grid=(n_tiles, f // tf),
        in_specs=in_specs + [
            pl.BlockSpec((1, d), lambda i, j: (0, 0)),
            pl.BlockSpec((d, tf), lambda i, j: (0, j)),
            pl.BlockSpec((d, tf), lambda i, j: (0, j)),
            pl.BlockSpec((tf, d), lambda i, j: (j, 0)),
            pl.BlockSpec((1, d), lambda i, j: (0, 0)),
        ],
        out_specs=o_specs,
        out_shape=o_shapes,
        scratch_shapes=[
            pltpu.VMEM((2, tm, d), F32),
            pltpu.VMEM((SUBLANES, LANES) if has_xn else (tm, d), BF16),
            pltpu.SemaphoreType.DMA((2,)),
            pltpu.SemaphoreType.DMA((2,)),
        ] + ([pltpu.VMEM((tm, d), BF16), pltpu.SemaphoreType.DMA(())] if emit_norm else []),
        compiler_params=_cparams("arbitrary", "arbitrary"),
        name="ffn",
    )(*xs, *([xn] if has_xn else []), norm_w.reshape(1, d), w1, w3, w2, fw.reshape(1, d))


def _inproj_body(*refs, n_full, tail_rows, n_uvz, n_xbc, n_dt, n_tiles, rounds):
    nr = len(rounds)
    xn_ref, w_ref, wdt_ref = refs[:3]
    rin_refs = refs[3:3 + nr]
    p1_ref, xbc_ref, dt_ref = refs[3 + nr:6 + nr]
    rout_refs = refs[6 + nr:6 + 2 * nr]
    wb_ref = refs[6 + 2 * nr]
    j = pl.program_id(0)

    step = j * n_tiles + pl.program_id(1)
    for nblk, rin, rout in zip(rounds, rin_refs, rout_refs):
        @pl.when(step < nblk)
        def _(rin=rin, rout=rout):
            rout[...] = rin[...].astype(rout.dtype)

    is_xbc = (j >= n_uvz) & (j < n_uvz + n_xbc)

    @pl.when(pl.program_id(1) == 0)
    def _():
        wb_ref[...] = w_ref[...].astype(BF16)

    def run(is_tail, rows):
        @pl.when(j == 0)
        def _():
            dt = _dot_nt(xn_ref[0:rows, :], wdt_ref[...].astype(BF16))
            lane = _iota2(dt.shape, 1)
            dt_ref[0:rows, :] = jnp.where(lane < n_dt, dt, 0.0)

        @pl.when(jnp.logical_not(is_xbc))
        def _():
            p1_ref[0:rows, :] = _dot_nt(xn_ref[0:rows, :], wb_ref[...]).astype(p1_ref.dtype)

        @pl.when(is_xbc)
        def _():
            xbc_ref[0:rows, :] = _dot_nt(xn_ref[0:rows, :], wb_ref[...])

    _tile_branches(n_full, TOKEN_TILE, tail_rows, run, axis=1)


def _inproj(xn, w_t, c_uvz, c_xbc, c_dt, round_ws=()):
    n, d = xn.shape
    tm, tn = TOKEN_TILE, PROJ_COLS
    n_full, tail_rows = divmod(n, tm)
    n_tiles = n_full + (1 if tail_rows else 0)
    last_i = n_tiles - 1
    c_gate0 = c_uvz + c_xbc + c_dt
    c_gates = w_t.shape[0] - c_gate0
    n_uvz, n_xbc, n_g = c_uvz // tn, c_xbc // tn, c_gates // tn
    n_main = n_uvz + n_xbc
    assert c_gate0 % SUBLANES == 0 and (c_uvz + c_xbc) % SUBLANES == 0

    def w_row(j, i):
        return (pl.multiple_of(jnp.where(j < n_main, j * tn, c_gate0 + (j - n_main) * tn), SUBLANES), 0)

    def p1_idx(j, i):
        writes = (j < n_uvz) | (j >= n_main)
        col = jnp.where(j < n_uvz, j, jnp.maximum(j - n_xbc, n_uvz - 1))
        return (jnp.where(writes, i, last_i), col)

    def xbc_idx(j, i):
        row = jnp.where(j < n_uvz, 0, jnp.where(j < n_main, i, last_i))
        return (row, jnp.clip(j - n_uvz, 0, n_xbc - 1))

    n_steps = (n_main + n_g) * n_tiles
    round_rows = []
    for w in round_ws:
        rr = next(r for r in range(ROUND_ROW_ALIGN, w.shape[0] + 1, ROUND_ROW_ALIGN)
                  if w.shape[0] % r == 0 and w.shape[0] // r <= n_steps)
        round_rows.append(rr)
    rounds = [w.shape[0] // rr for w, rr in zip(round_ws, round_rows)]

    def r_specs():
        return [pl.BlockSpec((rr, w.shape[1]),
                             lambda j, i, nblk=nblk: (jnp.minimum(j * n_tiles + i, nblk - 1), 0))
                for w, rr, nblk in zip(round_ws, round_rows, rounds)]

    return pl.pallas_call(
        functools.partial(_inproj_body, n_full=n_full, tail_rows=tail_rows, n_uvz=n_uvz, n_xbc=n_xbc, n_dt=c_dt,
                          n_tiles=n_tiles, rounds=tuple(rounds)),
        grid=(n_main + n_g, n_tiles),
        in_specs=[
            pl.BlockSpec((tm, d), lambda j, i: (i, 0)),
            pl.BlockSpec((pl.Element(tn), pl.Element(d)), w_row),
            pl.BlockSpec((pl.Element(LANES), pl.Element(d)), lambda j, i: (c_uvz + c_xbc, 0)),
        ] + r_specs(),
        out_specs=[
            pl.BlockSpec((tm, tn), p1_idx),
            pl.BlockSpec((tm, tn), xbc_idx),
            pl.BlockSpec((tm, LANES), lambda j, i: (jnp.where(j == 0, i, last_i), 0)),
        ] + r_specs(),
        out_shape=[
            jax.ShapeDtypeStruct((n, c_uvz + c_gates), BF16),
            jax.ShapeDtypeStruct((n, c_xbc), F32),
            jax.ShapeDtypeStruct((n, LANES), F32),
        ] + [jax.ShapeDtypeStruct(w.shape, BF16) for w in round_ws],
        scratch_shapes=[pltpu.VMEM((tn, d), BF16)],
        compiler_params=_cparams("arbitrary", "arbitrary"),
        name="in_proj",
    )(xn, w_t, w_t, *round_ws)


def _merge_body(x_ref, yap_ref, ybp_ref, yas_ref, ybs_ref, ga_ref, gb_ref, bg_ref, wa_ref, wb_ref, wo_ref, nw_ref,
                o_ref, xn_ref, *, n_prompt_tiles):
    def run(ya_ref, yb_ref):
        pa = _dot(ya_ref[...], wa_ref[...])
        pb = _dot(yb_ref[...], wb_ref[...])
        ga = jax.nn.sigmoid(ga_ref[...].astype(F32) + bg_ref[0:1, :])
        gb = jax.nn.sigmoid(gb_ref[...].astype(F32) + bg_ref[1:2, :])
        m = (ga * pa + gb * pb).astype(BF16)
        o = x_ref[...] + _dot(m, wo_ref[...])
        o_ref[...] = o
        r = lax.rsqrt(jnp.mean(o * o, axis=-1, keepdims=True) + EPS)
        xn_ref[...] = ((o * r) * nw_ref[...]).astype(xn_ref.dtype)

    i = pl.program_id(0)

    @pl.when(i < n_prompt_tiles)
    def _():
        run(yap_ref, ybp_ref)

    @pl.when(i >= n_prompt_tiles)
    def _():
        run(yas_ref, ybs_ref)


def _merge(x, ya_p, yb_p, ya_s, yb_s, p1, b_gate, wa, wb, wo, next_norm_w):
    n, d = x.shape
    tm = MERGE_TILE
    npt = ya_p.shape[0] // tm
    nst = ya_s.shape[0] // tm
    gcol = p1.shape[1] // d - 2
    p_idx = lambda i: (jnp.minimum(i, npt - 1), 0)
    s_idx = lambda i: (jnp.clip(i - npt, 0, nst - 1), 0)
    return pl.pallas_call(
        functools.partial(_merge_body, n_prompt_tiles=npt),
        grid=(n // tm,),
        in_specs=[
            pl.BlockSpec((tm, d), lambda i: (i, 0)),
            pl.BlockSpec((tm, ya_p.shape[1]), p_idx),
            pl.BlockSpec((tm, yb_p.shape[1]), p_idx),
            pl.BlockSpec((tm, ya_s.shape[1]), s_idx),
            pl.BlockSpec((tm, yb_s.shape[1]), s_idx),
            pl.BlockSpec((tm, d), lambda i: (i, gcol)),
            pl.BlockSpec((tm, d), lambda i: (i, gcol + 1)),
            pl.BlockSpec((2, d), lambda i: (0, 0)),
            pl.BlockSpec(wa.shape, lambda i: (0, 0), **_SINGLE),
            pl.BlockSpec(wb.shape, lambda i: (0, 0), **_SINGLE),
            pl.BlockSpec(wo.shape, lambda i: (0, 0), **_SINGLE),
            pl.BlockSpec((1, d), lambda i: (0, 0)),
        ],
        out_specs=[pl.BlockSpec((tm, d), lambda i: (i, 0)), pl.BlockSpec((tm, d), lambda i: (i, 0))],
        out_shape=[jax.ShapeDtypeStruct((n, d), F32), jax.ShapeDtypeStruct((n, d), BF16)],
        compiler_params=_cparams("arbitrary"),
        name="merge",
    )(x, ya_p, yb_p, ya_s, yb_s, p1, p1, b_gate, wa, wb, wo, next_norm_w.reshape(1, d))


def _iota2(shape, dim):
    return lax.broadcasted_iota(jnp.int32, shape, dim)


def _block_masks(seq_len):
    tb = TOKEN_BLOCK
    row = _iota2((tb, tb), 0)
    col = _iota2((tb, tb), 1)
    if seq_len >= tb:
        causal = col <= row
        last = col == tb - 1
    else:
        causal = (col <= row) & ((row // seq_len) == (col // seq_len))
        last = col == (row // seq_len) * seq_len + (seq_len - 1)
    return causal, last


def _init_constants(wm_ref, e_ref, gw_ref, causal):
    for h in range(GM_HEADS):
        wm_ref[h] = jnp.where(causal, gw_ref[h], 0.0).astype(BF16)
    hrow = _iota2(e_ref.shape, 0)
    ccol = _iota2(e_ref.shape, 1)
    e_ref[...] = jnp.where(hrow == ccol // SSD_HEAD_DIM, 1.0, 0.0).astype(BF16)


def _gating(u_ref, v_ref, gvn_ref, bs_ref, wm_ref, ya_ref, vn_ref):
    vg = _gelu(v_ref[...].astype(F32))
    r = lax.rsqrt(jnp.mean(vg * vg, axis=-1, keepdims=True) + EPS)
    vn = (vg * r) * gvn_ref[...]
    if vn_ref is not None:
        vn_ref[...] = vn
    vnb = vn.astype(BF16)
    for h in range(GM_HEADS):
        cs = slice(h * LANES, (h + 1) * LANES)
        s = _dot(wm_ref[h], vnb[:, cs]) + bs_ref[:, h:h + 1]
        ya_ref[:, cs] = (_gelu(u_ref[:, cs].astype(F32)) * s).astype(ya_ref.dtype)


def _conv(xp_ref, cw_ref, cb_ref, xc_ref, cst_ref, seq_len):
    tb = TOKEN_BLOCK
    width = xc_ref.shape[1]
    cw = CHUNK_COLS
    ns = CONV_W - 1
    place = None
    if cst_ref is not None:
        nseq = cst_ref.shape[1]
        tpos = _iota2((tb, 1), 0) % seq_len
        row = _iota2((tb, nseq), 0)
        col = _iota2((tb, nseq), 1)
        place = {(d, r): jnp.where((col == row // seq_len) & (row % seq_len == r + d - ns), 1.0, 0.0).astype(BF16)
                 for d in range(1, CONV_W) for r in range(ns - d, ns)}
    for c0 in range(0, width, cw):
        cs = slice(c0, c0 + cw)
        acc = None
        full = xp_ref[:, cs]
        if place is not None:
            pieces = [_split3(cst_ref[r, :, cs]) for r in range(ns)]
        for k in range(CONV_W):
            d = CONV_W - 1 - k
            xs = (pltpu.roll(full, d, axis=0) if d else full)[SUBLANES:SUBLANES + tb, :]
            if place is not None and d > 0:
                xs = jnp.where(tpos >= d, xs, 0.0)
                for r in range(ns - d, ns):
                    pd = place[(d, r)]
                    xs = xs + ((_dot(pd, pieces[r][0]) + _dot(pd, pieces[r][1])) + _dot(pd, pieces[r][2]))
            term = cw_ref[k:k + 1, cs] * xs
            acc = term if acc is None else acc + term
        xc_ref[:, cs] = _silu(cb_ref[:, cs] + acc)


def _ssd_prepare(dt_ref, dtb_ref, alog_ref, e_ref, causal, last, dtf_ref, tef_ref, eaf_ref):
    dt = _softplus(dt_ref[...] + dtb_ref[...])
    a = -jnp.exp(alog_ref[...])
    da = dt * a
    lmat = jnp.where(causal, 1.0, 0.0).astype(BF16)
    acum = _sel_left(lmat, da)
    a_last = _sel_left(jnp.where(last, 1.0, 0.0).astype(BF16), acum)
    ea = jnp.exp(acum)
    e = e_ref[...]
    dtf_ref[...] = _sel_right(dt, e)
    tef_ref[...] = _sel_right(dt * jnp.exp(a_last - acum), e)
    eaf_ref[...] = _sel_right(ea, e)
    return acum, acum.T


def _ssd_diag_group(g, xc_ref, dtf_ref, acum, acum_t, causal):
    tb = TOKEN_BLOCK
    width = xc_ref.shape[1] - 2 * SSD_GROUPS * D_STATE
    gw = width // SSD_GROUPS
    bg = xc_ref[:, width + g * D_STATE:width + (g + 1) * D_STATE]
    cg = xc_ref[:, width + (SSD_GROUPS + g) * D_STATE:width + (SSD_GROUPS + g + 1) * D_STATE].astype(BF16)
    cb = jnp.where(causal, _dot_nt(cg, bg.astype(BF16)), 0.0)
    lane = _iota2((tb, LANES), 1)
    heads_per_group = gw // SSD_HEAD_DIM
    outs = []
    for jp in range(heads_per_group // 2):
        ms = []
        for hh in range(2):
            h = g * heads_per_group + 2 * jp + hh
            seg = jnp.minimum(acum[:, h:h + 1] - acum_t[h:h + 1, :], 0.0)
            ms.append((cb * jnp.exp(seg)).astype(BF16))
        lhs = jnp.concatenate(ms, axis=1)
        ps = slice(g * gw + jp * LANES, g * gw + (jp + 1) * LANES)
        xpair = (xc_ref[:, ps] * dtf_ref[:, ps]).astype(BF16)
        zero = jnp.zeros_like(xpair)
        rhs = jnp.concatenate([jnp.where(lane < SSD_HEAD_DIM, xpair, zero),
                               jnp.where(lane >= SSD_HEAD_DIM, xpair, zero)], axis=0)
        outs.append(_dot(lhs, rhs))
    return jnp.concatenate(outs, axis=1), cg, bg


def _finish_group(g, y, xc_ref, z_ref, dsk_ref, sn_ref, yb_ref):
    gw = yb_ref.shape[1] // SSD_GROUPS
    gs = slice(g * gw, (g + 1) * gw)
    y = y + dsk_ref[:, gs] * xc_ref[:, gs]
    yz = y * _silu(z_ref[:, gs].astype(F32))
    r = lax.rsqrt(jnp.mean(yz * yz, axis=-1, keepdims=True) + EPS)
    yb_ref[:, gs] = ((yz * r) * sn_ref[:, gs]).astype(yb_ref.dtype)


def _prompt_body(*refs, n_par):
    refs = list(refs)
    ins = [[refs.pop(0) for _ in range(5)] for _ in range(n_par)]
    gvn_ref, gw_ref, bs_ref, cw_ref, cb_ref, dtb_ref, alog_ref, dsk_ref, sn_ref = [refs.pop(0) for _ in range(9)]
    out4 = [refs.pop(0) for _ in range(4)]
    outs = [[o.at[p] for o in out4] for p in range(n_par)]
    wm_ref, e_ref = refs.pop(0), refs.pop(0)
    scr = [[refs.pop(0) for _ in range(6)] for _ in range(n_par)]
    tb = TOKEN_BLOCK
    c = pl.program_id(1)
    causal, last = _block_masks(tb)

    @pl.when((pl.program_id(0) == 0) & (c == 0))
    def _():
        _init_constants(wm_ref, e_ref, gw_ref, causal)

    @pl.when(c == 0)
    def _():
        for xp_ref, _, _, _, _, st_ref in scr:
            st_ref[...] = jnp.zeros_like(st_ref)
            xp_ref[0:SUBLANES, :] = jnp.zeros((SUBLANES, xp_ref.shape[1]), F32)

    for (u_ref, v_ref, z_ref, xbc_ref, dt_ref), (ya_ref, yb_ref, _, _), \
            (xp_ref, xc_ref, dtf_ref, tef_ref, eaf_ref, st_ref) in zip(ins, outs, scr):
        _gating(u_ref, v_ref, gvn_ref, bs_ref, wm_ref, ya_ref, None)

        xp_ref[SUBLANES:SUBLANES + tb, :] = xbc_ref[...]
        _conv(xp_ref, cw_ref, cb_ref, xc_ref, None, tb)
        xp_ref[0:SUBLANES, :] = xbc_ref[tb - SUBLANES:tb, :]

        acum, acum_t = _ssd_prepare(dt_ref, dtb_ref, alog_ref, e_ref, causal, last, dtf_ref, tef_ref, eaf_ref)

        gw = yb_ref.shape[1] // SSD_GROUPS
        for g in range(SSD_GROUPS):
            gs = slice(g * gw, (g + 1) * gw)
            yd, cg, bg = _ssd_diag_group(g, xc_ref, dtf_ref, acum, acum_t, causal)
            st = st_ref[:, gs]
            y = yd + _dot(cg, st.astype(BF16)) * eaf_ref[:, gs]
            _finish_group(g, y, xc_ref, z_ref, dsk_ref, sn_ref, yb_ref)
            xw = (xc_ref[:, gs] * tef_ref[:, gs]).astype(BF16)
            st_ref[:, gs] = eaf_ref[tb - 1:tb, gs] * st + _dot(bg.T.astype(BF16), xw)

    @pl.when(c == pl.num_programs(1) - 1)
    def _():
        for (_, _, _, xbc_ref, _), (_, _, hs_ref, ct_ref), (_, _, _, _, _, st_ref) in zip(ins, outs, scr):
            for k in range(st_ref.shape[1] // LANES):
                hs_ref[k * LANES:(k + 1) * LANES, :] = st_ref[:, k * LANES:(k + 1) * LANES].T
            ct_ref[...] = xbc_ref[tb - SUBLANES:tb, :]


def _sample_body(u_ref, v_ref, z_ref, xbc_ref, dt_ref, cst_ref, h0_ref,
                 gvn_ref, gw_ref, bs_ref, cw_ref, cb_ref, dtb_ref, alog_ref, dsk_ref, sn_ref,
                 ya_ref, yb_ref, vn_ref, hs_ref, cs_ref,
                 wm_ref, e_ref, xp_ref, xc_ref, dtf_ref, tef_ref, eaf_ref,
                 y_ref, yoff_ref, cg_ref, bg_ref, xwt_ref, eat_ref, cdh_ref, *, seq_len):
    tb = TOKEN_BLOCK
    k = pl.program_id(1)
    causal, last = _block_masks(seq_len)
    gw = yb_ref.shape[1] // SSD_GROUPS

    @pl.when((pl.program_id(0) == 0) & (k == 0))
    def _():
        _init_constants(wm_ref, e_ref, gw_ref, causal)
        xp_ref[0:SUBLANES, :] = jnp.zeros((SUBLANES, xp_ref.shape[1]), F32)

    @pl.when(k == 0)
    def _():
        _gating(u_ref, v_ref, gvn_ref, bs_ref, wm_ref, ya_ref, vn_ref)
        xp_ref[SUBLANES:SUBLANES + tb, :] = xbc_ref[...]
        _conv(xp_ref, cw_ref, cb_ref, xc_ref, cst_ref, seq_len)
        nsb = tb // seq_len
        pick = [jnp.where(_iota2((nsb, tb), 1) == _iota2((nsb, tb), 0) * seq_len + (seq_len - (CONV_W - 1) + r),
                          1.0, 0.0).astype(BF16) for r in range(CONV_W - 1)]
        for c0 in range(0, xbc_ref.shape[1], CHUNK_COLS):
            pieces = _split3(xbc_ref[:, c0:c0 + CHUNK_COLS])
            for r in range(CONV_W - 1):
                cs_ref[r, :, c0:c0 + CHUNK_COLS] = ((_dot(pick[r], pieces[0]) + _dot(pick[r], pieces[1]))
                                             + _dot(pick[r], pieces[2]))
        acum, acum_t = _ssd_prepare(dt_ref, dtb_ref, alog_ref, e_ref, causal, last, dtf_ref, tef_ref, eaf_ref)
        eat_ref[...] = jnp.exp(acum_t)
        for g in range(SSD_GROUPS):
            gs = slice(g * gw, (g + 1) * gw)
            yd, cg, bg = _ssd_diag_group(g, xc_ref, dtf_ref, acum, acum_t, causal)
            y_ref[:, gs] = yd
            cg_ref[:, g * D_STATE:(g + 1) * D_STATE] = cg.astype(F32)
            bg_ref[:, g * D_STATE:(g + 1) * D_STATE] = bg
            xw = xc_ref[:, gs] * tef_ref[:, gs]
            for q in range(gw // LANES):
                r0 = g * gw + q * LANES
                xwt_ref[r0:r0 + LANES, :] = xw[:, q * LANES:(q + 1) * LANES].T.astype(BF16)

    nseq = h0_ref.shape[0]
    nt = nseq * seq_len
    rows = pl.ds(pl.multiple_of(k * nt, nt), nt)
    rowseq = _iota2((tb, 1), 0) // seq_len
    subseq = _iota2((nt, 1), 0) // seq_len
    tok = _iota2((tb, LANES), 0)
    onehot = jnp.concatenate(
        [jnp.where(tok == (k * nseq + b) * seq_len + (seq_len - 1), 1.0, 0.0) for b in range(nseq)],
        axis=1).astype(BF16)
    cdh_ref[...] = _sel_right(eat_ref[...], onehot)
    hpg = gw // SSD_HEAD_DIM
    for g in range(SSD_GROUPS):
        gs = slice(g * gw, (g + 1) * gw)
        ds = slice(g * D_STATE, (g + 1) * D_STATE)
        c_sub = cg_ref[rows, ds]
        lhs = jnp.concatenate([jnp.where(subseq == b, c_sub, 0.0) for b in range(nseq)], axis=1).astype(BF16)
        h0cat = jnp.concatenate([h0_ref[b, gs, :].astype(BF16) for b in range(nseq)], axis=1)
        yoff_ref[rows, gs] = _dot_nt(lhs, h0cat)
        b_all = bg_ref[:, ds]
        bm = jnp.concatenate([jnp.where(rowseq == k * nseq + b, b_all, 0.0) for b in range(nseq)],
                             axis=1).astype(BF16)
        s_new = _dot(xwt_ref[gs, :], bm)
        for b in range(nseq):
            ls = slice(b * D_STATE, (b + 1) * D_STATE)
            for j in range(hpg):
                h = g * hpg + j
                r0 = g * gw + j * SSD_HEAD_DIM
                hs_ref[b, r0:r0 + SSD_HEAD_DIM, :] = (cdh_ref[h:h + 1, ls] * h0_ref[b, r0:r0 + SSD_HEAD_DIM, :]
                                                      + s_new[j * SSD_HEAD_DIM:(j + 1) * SSD_HEAD_DIM, ls])

    @pl.when(k == pl.num_programs(1) - 1)
    def _():
        for g in range(SSD_GROUPS):
            gs = slice(g * gw, (g + 1) * gw)
            y = y_ref[:, gs] + yoff_ref[:, gs] * eaf_ref[:, gs]
            _finish_group(g, y, xc_ref, z_ref, dsk_ref, sn_ref, yb_ref)


def _mixer_params(lp):
    (gm_v_norm, gm_ws, gm_bs, conv_w, conv_b, dt_bias, a_log, d_skip, ssd_norm) = lp
    nh = dt_bias.shape[0]
    pad = LANES - nh
    return dict(
        gvn=gm_v_norm.reshape(1, -1),
        cw=conv_w,
        cb=conv_b.reshape(1, -1),
        dtb=jnp.pad(dt_bias, (0, pad)).reshape(1, LANES),
        alog=jnp.pad(a_log, (0, pad)).reshape(1, LANES),
        dsk=jnp.repeat(d_skip, SSD_HEAD_DIM).reshape(1, -1),
        sn=ssd_norm.reshape(1, -1),
    )


def _full_spec(a):
    nd = a.ndim
    return pl.BlockSpec(a.shape, lambda i, j: (0,) * nd)


def _mixer_prompt(p1, xbc, dt, lp, n_seq, seq_len, gm_w, ssd_w, conv_dim):
    tb = TOKEN_BLOCK
    nc = seq_len // tb
    n_par = PROMPT_SEQS_PER_STEP if n_seq % PROMPT_SEQS_PER_STEP == 0 else 1
    mp = _mixer_params(lp)
    gw_full = lp[1]
    bs_t = lp[2].T
    params = [mp["gvn"], gw_full, bs_t, mp["cw"], mp["cb"], mp["dtb"], mp["alog"], mp["dsk"], mp["sn"]]
    in_specs, scratch, operands = [], [], []
    for p in range(n_par):
        row = lambda b, c, p=p: (b * n_par + p) * nc + c
        in_specs += [
            pl.BlockSpec((tb, gm_w), lambda b, c, row=row: (row(b, c), 0)),
            pl.BlockSpec((tb, gm_w), lambda b, c, row=row: (row(b, c), 1)),
            pl.BlockSpec((tb, ssd_w), lambda b, c, row=row: (row(b, c), 1)),
            pl.BlockSpec((tb, conv_dim), lambda b, c, row=row: (row(b, c), 0)),
            pl.BlockSpec((tb, LANES), lambda b, c, row=row: (row(b, c), 0)),
        ]
        operands += [p1, p1, p1, xbc, dt]
        scratch += [
            pltpu.VMEM((SUBLANES + tb, conv_dim), F32),
            pltpu.VMEM((tb, conv_dim), F32),
            pltpu.VMEM((tb, ssd_w), F32),
            pltpu.VMEM((tb, ssd_w), F32),
            pltpu.VMEM((tb, ssd_w), F32),
            pltpu.VMEM((D_STATE, ssd_w), F32),
        ]
    res = pl.pallas_call(
        functools.partial(_prompt_body, n_par=n_par),
        grid=(n_seq // n_par, nc),
        in_specs=in_specs + [_full_spec(a) for a in params],
        out_specs=[
            pl.BlockSpec((n_par, tb, gm_w), lambda b, c: (b, c, 0)),
            pl.BlockSpec((n_par, tb, ssd_w), lambda b, c: (b, c, 0)),
            pl.BlockSpec((n_par, ssd_w, D_STATE), lambda b, c: (b, 0, 0)),
            pl.BlockSpec((n_par, SUBLANES, conv_dim), lambda b, c: (b, 0, 0)),
        ],
        out_shape=[
            jax.ShapeDtypeStruct((n_seq, seq_len, gm_w), BF16),
            jax.ShapeDtypeStruct((n_seq, seq_len, ssd_w), BF16),
            jax.ShapeDtypeStruct((n_seq, ssd_w, D_STATE), F32),
            jax.ShapeDtypeStruct((n_seq, SUBLANES, conv_dim), F32),
        ],
        scratch_shapes=[
            pltpu.VMEM((GM_HEADS, tb, tb), BF16),
            pltpu.VMEM((LANES, ssd_w), BF16),
        ] + scratch,
        compiler_params=_cparams("arbitrary", "arbitrary"),
        name="mixer_prompt",
    )(*operands, *params)
    ya, yb, hs, ct = res
    return ya.reshape(n_seq * seq_len, gm_w), yb.reshape(n_seq * seq_len, ssd_w), hs, ct


def _mixer_sample(p1, xbc, dt, lp, h0, conv_state, row0, n_seq, seq_len, gm_w, ssd_w, conv_dim):
    tb = TOKEN_BLOCK
    seqs_per_block = tb // seq_len
    nblk = n_seq // seqs_per_block
    nsub = seqs_per_block // SAMPLE_SEQS_PER_STEP
    blk0 = row0 // tb
    mp = _mixer_params(lp)
    reps = tb // seq_len
    gw_tiled = jnp.tile(lp[1][:, :seq_len, :seq_len], (1, reps, reps))
    bs_t = jnp.tile(lp[2][:, :seq_len], (1, reps)).T
    params = [mp["gvn"], gw_tiled, bs_t, mp["cw"], mp["cb"], mp["dtb"], mp["alog"], mp["dsk"], mp["sn"]]
    ntok = n_seq * seq_len
    return pl.pallas_call(
        functools.partial(_sample_body, seq_len=seq_len),
        grid=(nblk, nsub),
        in_specs=[
            pl.BlockSpec((tb, gm_w), lambda i, k: (blk0 + i, 0)),
            pl.BlockSpec((tb, gm_w), lambda i, k: (blk0 + i, 1)),
            pl.BlockSpec((tb, ssd_w), lambda i, k: (blk0 + i, 1)),
            pl.BlockSpec((tb, conv_dim), lambda i, k: (blk0 + i, 0)),
            pl.BlockSpec((tb, LANES), lambda i, k: (blk0 + i, 0)),
            pl.BlockSpec((CONV_W - 1, seqs_per_block, conv_dim), lambda i, k: (0, i, 0)),
            pl.BlockSpec((SAMPLE_SEQS_PER_STEP, ssd_w, D_STATE), lambda i, k: (i * nsub + k, 0, 0)),
        ] + [_full_spec(a) for a in params],
        out_specs=[
            pl.BlockSpec((tb, gm_w), lambda i, k: (i, 0)),
            pl.BlockSpec((tb, ssd_w), lambda i, k: (i, 0)),
            pl.BlockSpec((tb, gm_w), lambda i, k: (i, 0)),
            pl.BlockSpec((SAMPLE_SEQS_PER_STEP, ssd_w, D_STATE), lambda i, k: (i * nsub + k, 0, 0)),
            pl.BlockSpec((CONV_W - 1, seqs_per_block, conv_dim), lambda i, k: (0, i, 0)),
        ],
        out_shape=[
            jax.ShapeDtypeStruct((ntok, gm_w), BF16),
            jax.ShapeDtypeStruct((ntok, ssd_w), BF16),
            jax.ShapeDtypeStruct((ntok, gm_w), F32),
            jax.ShapeDtypeStruct((n_seq, ssd_w, D_STATE), F32),
            jax.ShapeDtypeStruct((CONV_W - 1, n_seq, conv_dim), F32),
        ],
        scratch_shapes=[
            pltpu.VMEM((GM_HEADS, tb, tb), BF16),
            pltpu.VMEM((LANES, ssd_w), BF16),
            pltpu.VMEM((SUBLANES + tb, conv_dim), F32),
            pltpu.VMEM((tb, conv_dim), F32),
            pltpu.VMEM((tb, ssd_w), F32),
            pltpu.VMEM((tb, ssd_w), F32),
            pltpu.VMEM((tb, ssd_w), F32),
            pltpu.VMEM((tb, ssd_w), F32),
            pltpu.VMEM((tb, ssd_w), F32),
            pltpu.VMEM((tb, SSD_GROUPS * D_STATE), F32),
            pltpu.VMEM((tb, SSD_GROUPS * D_STATE), F32),
            pltpu.VMEM((ssd_w, tb), BF16),
            pltpu.VMEM((LANES, tb), F32),
            pltpu.VMEM((LANES, SAMPLE_SEQS_PER_STEP * LANES), F32),
        ],
        compiler_params=_cparams("arbitrary", "arbitrary"),
        name="mixer_sample",
    )(p1, p1, p1, xbc, dt, conv_state, h0, *params)


def kernel(x_prompt, x_sample, state_ssm, state_conv, ffn1_norm, ffn1_w1, ffn1_w3, ffn1_w2, mix_norm, w_in,
           b_gate, conv_w, conv_b, dt_bias, a_log, d_skip, ssd_norm, gm_v_norm, gm_ws, gm_bs, w_proj_a,
           w_proj_b, w_out, ffn2_norm, ffn2_w1, ffn2_w3, ffn2_w2, final_norm):
    bp, tp, d = x_prompt.shape
    bs_, ts, _ = x_sample.shape
    depth = w_in.shape[0]
    n_heads = dt_bias.shape[1]
    gm_w = gm_v_norm.shape[1]
    ssd_w = ssd_norm.shape[1]
    conv_dim = conv_w.shape[2]
    np_, ns = bp * tp, bs_ * ts

    xs = [x_prompt.reshape(np_, d), x_sample.reshape(ns, d)]
    ssm_p, conv_p, ssm_s, conv_s, v_s = [], [], [], [], []
    for l in range(depth):
        c_uvz = 2 * gm_w + ssd_w
        x, xn = _ffn(xs, ffn1_norm[l], ffn1_w1[l], ffn1_w3[l], ffn1_w2[l], next_norm_w=mix_norm[l])
        p1, xbc, dt, wa_b, wb_b, wo_b = _inproj(xn, w_in[l].T, c_uvz, conv_dim, n_heads,
                                                round_ws=(w_proj_a[l], w_proj_b[l], w_out[l]))

        lp = (gm_v_norm[l], gm_ws[l], gm_bs[l], conv_w[l], conv_b[l], dt_bias[l], a_log[l], d_skip[l], ssd_norm[l])
        ya_p, yb_p, hs_p, ct_p = _mixer_prompt(p1, xbc, dt, lp, bp, tp, gm_w, ssd_w, conv_dim)

        assert ts >= CONV_W - 1
        st = jnp.transpose(state_conv[l], (1, 0, 2))
        ya_s, yb_s, vn_s, hs_s, cs_s = _mixer_sample(p1, xbc, dt, lp, state_ssm[l].reshape(bs_, ssd_w, D_STATE), st,
                                                     np_, bs_, ts, gm_w, ssd_w, conv_dim)

        x, xn = _merge(x, ya_p, yb_p, ya_s, yb_s, p1, b_gate[l], wa_b, wb_b, wo_b, ffn2_norm[l])
        last = l == depth - 1
        xs = _ffn([x], ffn2_norm[l], ffn2_w1[l], ffn2_w3[l], ffn2_w2[l], xn=xn, final_w=final_norm if last else None,
                  split_rows=np_)

        ssm_p.append(hs_p.reshape(bp, n_heads, SSD_HEAD_DIM, D_STATE))
        conv_p.append(ct_p[:, SUBLANES - (CONV_W - 1):, :])
        ssm_s.append(hs_s.reshape(bs_, n_heads, SSD_HEAD_DIM, D_STATE))
        conv_s.append(jnp.transpose(cs_s, (1, 0, 2)))
        v_s.append(vn_s.reshape(bs_, ts, gm_w))

    return (xs[0].reshape(bp, tp, d), xs[1].reshape(bs_, ts, d), jnp.stack(ssm_p), jnp.stack(conv_p),
            jnp.stack(ssm_s), jnp.stack(conv_s), jnp.stack(v_s))
```

```python
import functools

import jax
import jax.numpy as jnp
from jax import lax
from jax.experimental import pallas as pl
from jax.experimental.pallas import tpu as pltpu

F32 = jnp.float32
BF16 = jnp.bfloat16
EPS = 1e-6

LANES = 128
SUBLANES = 8
VMEM_LIMIT_BYTES = 60 * 1024 * 1024

GM_HEADS = 8
SSD_HEAD_DIM = 64
SSD_GROUPS = 4
D_STATE = 128
CONV_W = 4
TOKEN_BLOCK = 128
SAMPLE_SEQS_PER_STEP = 8
PROMPT_SEQS_PER_STEP = 2

TOKEN_TILE = 1088
FFN_COLS = 512
PROJ_COLS = 1024
MERGE_TILE = 256
NORM_ROWS = 272
ROUND_ROW_ALIGN = 16
CHUNK_COLS = 512

_SINGLE = dict(pipeline_mode=pl.Buffered(1))


def _cparams(*sem):
    return pltpu.CompilerParams(dimension_semantics=sem, vmem_limit_bytes=VMEM_LIMIT_BYTES)


_GELU_C = 0.7978845608028654


def _gelu(x):
    t = jnp.tanh(x * (_GELU_C + (_GELU_C * 0.044715) * (x * x)))
    return x * (0.5 + 0.5 * t)


def _silu(x):
    return x * jax.nn.sigmoid(x)


def _softplus(x):
    return jnp.maximum(x, 0.0) + jnp.log1p(jnp.exp(-jnp.abs(x)))


def _split3(a):
    a1 = a.astype(BF16)
    r1 = a - a1.astype(F32)
    a2 = r1.astype(BF16)
    r2 = r1 - a2.astype(F32)
    return a1, a2, r2.astype(BF16)


def _dot(a, b):
    return jnp.dot(a, b, preferred_element_type=F32)


def _dot_nt(a, b):
    return lax.dot_general(a, b, (((1,), (1,)), ((), ())), preferred_element_type=F32)


def _sel_right(a, sel):
    a1, a2, a3 = _split3(a)
    return (_dot(a1, sel) + _dot(a2, sel)) + _dot(a3, sel)


def _sel_left(sel, a):
    a1, a2, a3 = _split3(a)
    return (_dot(sel, a1) + _dot(sel, a2)) + _dot(sel, a3)


def _row_loop(nrows, fn):
    def body(i, c):
        fn(pl.ds(pl.multiple_of(i * NORM_ROWS, NORM_ROWS), NORM_ROWS))
        return c

    lax.fori_loop(0, nrows // NORM_ROWS, body, 0)


def _rmsnorm_rows(x_ref, w_ref, out_ref, nrows):
    def one(sl):
        x = x_ref[sl, :]
        r = lax.rsqrt(jnp.mean(x * x, axis=-1, keepdims=True) + EPS)
        out_ref[sl, :] = ((x * r) * w_ref[...]).astype(out_ref.dtype)

    _row_loop(nrows, one)


def _tile_branches(n_full, tile, tail_rows, run, axis=0):
    i = pl.program_id(axis)

    @pl.when(i < n_full)
    def _():
        run(False, tile)

    if tail_rows:
        @pl.when(i >= n_full)
        def _():
            run(True, tail_rows)


def _tile_dma(action, t, *, hbm_refs, buf, sem, to_hbm, rows_first):
    tm = buf.shape[0]

    def go(pieces):
        for hbm, h0, b0, nr in pieces:
            h = hbm.at[pl.ds(h0, nr), :]
            v = buf.at[pl.ds(b0, nr), :]
            copy = pltpu.make_async_copy(v, h, sem) if to_hbm else pltpu.make_async_copy(h, v, sem)
            getattr(copy, action)()

    if len(hbm_refs) == 1:
        go([(hbm_refs[0], pl.multiple_of(t * tm, SUBLANES), 0, tm)])
        return
    n_pure, head = divmod(rows_first, tm)

    @pl.when(t < n_pure)
    def _():
        go([(hbm_refs[0], pl.multiple_of(t * tm, SUBLANES), 0, tm)])

    @pl.when(t >= n_pure)
    def _():
        go([(hbm_refs[0], n_pure * tm, 0, head), (hbm_refs[1], 0, head, tm - head)])


def _ffn_body(*refs, n_in, n_out, final_norm, emit_norm, has_xn, n_tiles, rows_first_in, rows_first_out):
    refs = list(refs)
    x_refs = [refs.pop(0) for _ in range(n_in)]
    xni_ref = refs.pop(0) if has_xn else None
    nw_ref, w1_ref, w3_ref, w2_ref, fn_ref = [refs.pop(0) for _ in range(5)]
    o_refs = [refs.pop(0) for _ in range(n_out)]
    xno_ref = refs.pop(0) if emit_norm else None
    acc_ref, xn_ref, in_sem, out_sem = refs[:4]
    xnb_ref, xno_sem = refs[4:] if emit_norm else (None, None)
    i = pl.program_id(0)
    j = pl.program_id(1)
    nj = pl.num_programs(1)
    tm = acc_ref.shape[1]
    slot = i % 2
    other = 1 - slot
    acc = acc_ref.at[slot]
    dma_in = functools.partial(_tile_dma, hbm_refs=x_refs, to_hbm=False, rows_first=rows_first_in)
    dma_out = functools.partial(_tile_dma, hbm_refs=o_refs, to_hbm=True, rows_first=rows_first_out)
    dma_xno = functools.partial(_tile_dma, hbm_refs=[xno_ref], buf=xnb_ref, sem=xno_sem, to_hbm=True, rows_first=None)

    @pl.when(j == 0)
    def _():
        @pl.when(i == 0)
        def _():
            dma_in("start", i, buf=acc, sem=in_sem.at[slot])

        dma_in("wait", i, buf=acc, sem=in_sem.at[slot])
        if not has_xn:
            _rmsnorm_rows(acc, nw_ref, xn_ref, tm)

    xn = (xni_ref if has_xn else xn_ref)[...]
    h1 = _dot(xn, w1_ref[...].astype(BF16))
    h3 = _dot(xn, w3_ref[...].astype(BF16))
    g = ((0.5 * _silu(h1)) * h3).astype(BF16)
    acc[...] += _dot(g, w2_ref[...].astype(BF16))

    @pl.when((j == nj - 2) & (i + 1 < n_tiles))
    def _():
        @pl.when(i >= 1)
        def _():
            dma_out("wait", i - 1, buf=acc_ref.at[other], sem=out_sem.at[other])

        dma_in("start", i + 1, buf=acc_ref.at[other], sem=in_sem.at[other])

    @pl.when(j == nj - 1)
    def _():
        if final_norm:
            _rmsnorm_rows(acc, fn_ref, acc, tm)
        if emit_norm:
            @pl.when(i >= 1)
            def _():
                dma_xno("wait", i - 1)

            _rmsnorm_rows(acc, fn_ref, xnb_ref, tm)
            dma_xno("start", i)
        dma_out("start", i, buf=acc, sem=out_sem.at[slot])

        @pl.when(i == n_tiles - 1)
        def _():
            if n_tiles > 1:
                dma_out("wait", i - 1, buf=acc_ref.at[other], sem=out_sem.at[other])
            dma_out("wait", i, buf=acc, sem=out_sem.at[slot])
            if emit_norm:
                dma_xno("wait", i)


def _ffn(xs, norm_w, w1, w3, w2, *, xn=None, final_w=None, next_norm_w=None, split_rows=None):
    d, f = w1.shape
    tm, tf = TOKEN_TILE, FFN_COLS
    n = sum(x.shape[0] for x in xs)
    assert n % tm == 0 and f % tf == 0 and f // tf >= 2
    n_tiles = n // tm
    for first in ([xs[0].shape[0]] if len(xs) == 2 else []) + ([split_rows] if split_rows else []):
        assert first % tm and (first // tm + 1) * tm == n
    final_norm = final_w is not None
    emit_norm = next_norm_w is not None
    assert not (final_norm and emit_norm)
    fw = final_w if final_norm else (next_norm_w if emit_norm else norm_w)
    has_xn = xn is not None
    any_spec = pl.BlockSpec(memory_space=pl.ANY)
    in_specs = [any_spec for _ in xs] + ([pl.BlockSpec((tm, d), lambda i, j: (i, 0))] if has_xn else [])
    if split_rows:
        o_specs = [any_spec, any_spec]
        o_shapes = [jax.ShapeDtypeStruct((split_rows, d), F32), jax.ShapeDtypeStruct((n - split_rows, d), F32)]
    else:
        o_specs = [any_spec]
        o_shapes = [jax.ShapeDtypeStruct((n, d), F32)]
    n_out = len(o_specs)
    if emit_norm:
        o_specs.append(any_spec)
        o_shapes.append(jax.ShapeDtypeStruct((n, d), BF16))
    return pl.pallas_call(
        functools.partial(_ffn_body, n_in=len(xs), n_out=n_out, final_norm=final_norm, emit_norm=emit_norm,
                          has_xn=has_xn, n_tiles=n_tiles, rows_first_in=xs[0].shape[0], rows_first_out=split_rows),
        grid=(n_tiles, f // tf),
        in_specs=in_specs + [
            pl.BlockSpec((1, d), lambda i, j: (0, 0)),
            pl.BlockSpec((d, tf), lambda i, j: (0, j)),
            pl.BlockSpec((d, tf), lambda i, j: (0, j)),
            pl.BlockSpec((tf, d), lambda i, j: (j, 0)),
            pl.BlockSpec((1, d), lambda i, j: (0, 0)),
        ],
        out_specs=o_specs,
        out_shape=o_shapes,
        scratch_shapes=[
            pltpu.VMEM((2, tm, d), F32),
            pltpu.VMEM((SUBLANES, LANES) if has_xn else (tm, d), BF16),
            pltpu.SemaphoreType.DMA((2,)),
            pltpu.SemaphoreType.DMA((2,)),
        ] + ([pltpu.VMEM((tm, d), BF16), pltpu.SemaphoreType.DMA(())] if emit_norm else []),
        compiler_params=_cparams("arbitrary", "arbitrary"),
        name="ffn",
    )(*xs, *([xn] if has_xn else []), norm_w.reshape(1, d), w1, w3, w2, fw.reshape(1, d))


def _inproj_body(*refs, n_full, tail_rows, n_uvz, n_xbc, n_dt, n_tiles, rounds):
    nr = len(rounds)
    xn_ref, w_ref, wdt_ref = refs[:3]
    rin_refs = refs[3:3 + nr]
    p1_ref, xbc_ref, dt_ref = refs[3 + nr:6 + nr]
    rout_refs = refs[6 + nr:6 + 2 * nr]
    wb_ref = refs[6 + 2 * nr]
    j = pl.program_id(0)

    step = j * n_tiles + pl.program_id(1)
    for nblk, rin, rout in zip(rounds, rin_refs, rout_refs):
        @pl.when(step < nblk)
        def _(rin=rin, rout=rout):
            rout[...] = rin[...].astype(rout.dtype)

    is_xbc = (j >= n_uvz) & (j < n_uvz + n_xbc)

    @pl.when(pl.program_id(1) == 0)
    def _():
        wb_ref[...] = w_ref[...].astype(BF16)

    def run(is_tail, rows):
        @pl.when(j == 0)
        def _():
            dt = _dot_nt(xn_ref[0:rows, :], wdt_ref[...].astype(BF16))
            lane = _iota2(dt.shape, 1)
            dt_ref[0:rows, :] = jnp.where(lane < n_dt, dt, 0.0)

        @pl.when(jnp.logical_not(is_xbc))
        def _():
            p1_ref[0:rows, :] = _dot_nt(xn_ref[0:rows, :], wb_ref[...]).astype(p1_ref.dtype)

        @pl.when(is_xbc)
        def _():
            xbc_ref[0:rows, :] = _dot_nt(xn_ref[0:rows, :], wb_ref[...])

    _tile_branches(n_full, TOKEN_TILE, tail_rows, run, axis=1)


def _inproj(xn, w_t, c_uvz, c_xbc, c_dt, round_ws=()):
    n, d = xn.shape
    tm, tn = TOKEN_TILE, PROJ_COLS
    n_full, tail_rows = divmod(n, tm)
    n_tiles = n_full + (1 if tail_rows else 0)
    last_i = n_tiles - 1
    c_gate0 = c_uvz + c_xbc + c_dt
    c_gates = w_t.shape[0] - c_gate0
    n_uvz, n_xbc, n_g = c_uvz // tn, c_xbc // tn, c_gates // tn
    n_main = n_uvz + n_xbc
    assert c_gate0 % SUBLANES == 0 and (c_uvz + c_xbc) % SUBLANES == 0

    def w_row(j, i):
        return (pl.multiple_of(jnp.where(j < n_main, j * tn, c_gate0 + (j - n_main) * tn), SUBLANES), 0)

    def p1_idx(j, i):
        writes = (j < n_uvz) | (j >= n_main)
        col = jnp.where(j < n_uvz, j, jnp.maximum(j - n_xbc, n_uvz - 1))
        return (jnp.where(writes, i, last_i), col)

    def xbc_idx(j, i):
        row = jnp.where(j < n_uvz, 0, jnp.where(j < n_main, i, last_i))
        return (row, jnp.clip(j - n_uvz, 0, n_xbc - 1))

    n_steps = (n_main + n_g) * n_tiles
    round_rows = []
    for w in round_ws:
        rr = next(r for r in range(ROUND_ROW_ALIGN, w.shape[0] + 1, ROUND_ROW_ALIGN)
                  if w.shape[0] % r == 0 and w.shape[0] // r <= n_steps)
        round_rows.append(rr)
    rounds = [w.shape[0] // rr for w, rr in zip(round_ws, round_rows)]

    def r_specs():
        return [pl.BlockSpec((rr, w.shape[1]),
                             lambda j, i, nblk=nblk: (jnp.minimum(j * n_tiles + i, nblk - 1), 0))
                for w, rr, nblk in zip(round_ws, round_rows, rounds)]

    return pl.pallas_call(
        functools.partial(_inproj_body, n_full=n_full, tail_rows=tail_rows, n_uvz=n_uvz, n_xbc=n_xbc, n_dt=c_dt,
                          n_tiles=n_tiles, rounds=tuple(rounds)),
        grid=(n_main + n_g, n_tiles),
        in_specs=[
            pl.BlockSpec((tm, d), lambda j, i: (i, 0)),
            pl.BlockSpec((pl.Element(tn), pl.Element(d)), w_row),
            pl.BlockSpec((pl.Element(LANES), pl.Element(d)), lambda j, i: (c_uvz + c_xbc, 0)),
        ] + r_specs(),
        out_specs=[
            pl.BlockSpec((tm, tn), p1_idx),
            pl.BlockSpec((tm, tn), xbc_idx),
            pl.BlockSpec((tm, LANES), lambda j, i: (jnp.where(j == 0, i, last_i), 0)),
        ] + r_specs(),
        out_shape=[
            jax.ShapeDtypeStruct((n, c_uvz + c_gates), BF16),
            jax.ShapeDtypeStruct((n, c_xbc), F32),
            jax.ShapeDtypeStruct((n, LANES), F32),
        ] + [jax.ShapeDtypeStruct(w.shape, BF16) for w in round_ws],
        scratch_shapes=[pltpu.VMEM((tn, d), BF16)],
        compiler_params=_cparams("arbitrary", "arbitrary"),
        name="in_proj",
    )(xn, w_t, w_t, *round_ws)


def _merge_body(x_ref, yap_ref, ybp_ref, yas_ref, ybs_ref, ga_ref, gb_ref, bg_ref, wa_ref, wb_ref, wo_ref, nw_ref,
                o_ref, xn_ref, *, n_prompt_tiles):
    def run(ya_ref, yb_ref):
        pa = _dot(ya_ref[...], wa_ref[...])
        pb = _dot(yb_ref[...], wb_ref[...])
        ga = jax.nn.sigmoid(ga_ref[...].astype(F32) + bg_ref[0:1, :])
        gb = jax.nn.sigmoid(gb_ref[...].astype(F32) + bg_ref[1:2, :])
        m = (ga * pa + gb * pb).astype(BF16)
        o = x_ref[...] + _dot(m, wo_ref[...])
        o_ref[...] = o
        r = lax.rsqrt(jnp.mean(o * o, axis=-1, keepdims=True) + EPS)
        xn_ref[...] = ((o * r) * nw_ref[...]).astype(xn_ref.dtype)

    i = pl.program_id(0)

    @pl.when(i < n_prompt_tiles)
    def _():
        run(yap_ref, ybp_ref)

    @pl.when(i >= n_prompt_tiles)
    def _():
        run(yas_ref, ybs_ref)


def _merge(x, ya_p, yb_p, ya_s, yb_s, p1, b_gate, wa, wb, wo, next_norm_w):
    n, d = x.shape
    tm = MERGE_TILE
    npt = ya_p.shape[0] // tm
    nst = ya_s.shape[0] // tm
    gcol = p1.shape[1] // d - 2
    p_idx = lambda i: (jnp.minimum(i, npt - 1), 0)
    s_idx = lambda i: (jnp.clip(i - npt, 0, nst - 1), 0)
    return pl.pallas_call(
        functools.partial(_merge_body, n_prompt_tiles=npt),
        grid=(n // tm,),
        in_specs=[
            pl.BlockSpec((tm, d), lambda i: (i, 0)),
            pl.BlockSpec((tm, ya_p.shape[1]), p_idx),
            pl.BlockSpec((tm, yb_p.shape[1]), p_idx),
            pl.BlockSpec((tm, ya_s.shape[1]), s_idx),
            pl.BlockSpec((tm, yb_s.shape[1]), s_idx),
            pl.BlockSpec((tm, d), lambda i: (i, gcol)),
            pl.BlockSpec((tm, d), lambda i: (i, gcol + 1)),
            pl.BlockSpec((2, d), lambda i: (0, 0)),
            pl.BlockSpec(wa.shape, lambda i: (0, 0), **_SINGLE),
            pl.BlockSpec(wb.shape, lambda i: (0, 0), **_SINGLE),
            pl.BlockSpec(wo.shape, lambda i: (0, 0), **_SINGLE),
            pl.BlockSpec((1, d), lambda i: (0, 0)),
        ],
        out_specs=[pl.BlockSpec((tm, d), lambda i: (i, 0)), pl.BlockSpec((tm, d), lambda i: (i, 0))],
        out_shape=[jax.ShapeDtypeStruct((n, d), F32), jax.ShapeDtypeStruct((n, d), BF16)],
        compiler_params=_cparams("arbitrary"),
        name="merge",
    )(x, ya_p, yb_p, ya_s, yb_s, p1, p1, b_gate, wa, wb, wo, next_norm_w.reshape(1, d))


def _iota2(shape, dim):
    return lax.broadcasted_iota(jnp.int32, shape, dim)


def _block_masks(seq_len):
    tb = TOKEN_BLOCK
    row = _iota2((tb, tb), 0)
    col = _iota2((tb, tb), 1)
    if seq_len >= tb:
        causal = col <= row
        last = col == tb - 1
    else:
        causal = (col <= row) & ((row // seq_len) == (col // seq_len))
        last = col == (row // seq_len) * seq_len + (seq_len - 1)
    return causal, last


def _init_constants(wm_ref, e_ref, gw_ref, causal):
    for h in range(GM_HEADS):
        wm_ref[h] = jnp.where(causal, gw_ref[h], 0.0).astype(BF16)
    hrow = _iota2(e_ref.shape, 0)
    ccol = _iota2(e_ref.shape, 1)
    e_ref[...] = jnp.where(hrow == ccol // SSD_HEAD_DIM, 1.0, 0.0).astype(BF16)


def _gating(u_ref, v_ref, gvn_ref, bs_ref, wm_ref, ya_ref, vn_ref):
    vg = _gelu(v_ref[...].astype(F32))
    r = lax.rsqrt(jnp.mean(vg * vg, axis=-1, keepdims=True) + EPS)
    vn = (vg * r) * gvn_ref[...]
    if vn_ref is not None:
        vn_ref[...] = vn
    vnb = vn.astype(BF16)
    for h in range(GM_HEADS):
        cs = slice(h * LANES, (h + 1) * LANES)
        s = _dot(wm_ref[h], vnb[:, cs]) + bs_ref[:, h:h + 1]
        ya_ref[:, cs] = (_gelu(u_ref[:, cs].astype(F32)) * s).astype(ya_ref.dtype)


def _conv(xp_ref, cw_ref, cb_ref, xc_ref, cst_ref, seq_len):
    tb = TOKEN_BLOCK
    width = xc_ref.shape[1]
    cw = CHUNK_COLS
    ns = CONV_W - 1
    place = None
    if cst_ref is not None:
        nseq = cst_ref.shape[1]
        tpos = _iota2((tb, 1), 0) % seq_len
        row = _iota2((tb, nseq), 0)
        col = _iota2((tb, nseq), 1)
        place = {(d, r): jnp.where((col == row // seq_len) & (row % seq_len == r + d - ns), 1.0, 0.0).astype(BF16)
                 for d in range(1, CONV_W) for r in range(ns - d, ns)}
    for c0 in range(0, width, cw):
        cs = slice(c0, c0 + cw)
        acc = None
        full = xp_ref[:, cs]
        if place is not None:
            pieces = [_split3(cst_ref[r, :, cs]) for r in range(ns)]
        for k in range(CONV_W):
            d = CONV_W - 1 - k
            xs = (pltpu.roll(full, d, axis=0) if d else full)[SUBLANES:SUBLANES + tb, :]
            if place is not None and d > 0:
                xs = jnp.where(tpos >= d, xs, 0.0)
                for r in range(ns - d, ns):
                    pd = place[(d, r)]
                    xs = xs + ((_dot(pd, pieces[r][0]) + _dot(pd, pieces[r][1])) + _dot(pd, pieces[r][2]))
            term = cw_ref[k:k + 1, cs] * xs
            acc = term if acc is None else acc + term
        xc_ref[:, cs] = _silu(cb_ref[:, cs] + acc)


def _ssd_prepare(dt_ref, dtb_ref, alog_ref, e_ref, causal, last, dtf_ref, tef_ref, eaf_ref):
    dt = _softplus(dt_ref[...] + dtb_ref[...])
    a = -jnp.exp(alog_ref[...])
    da = dt * a
    lmat = jnp.where(causal, 1.0, 0.0).astype(BF16)
    acum = _sel_left(lmat, da)
    a_last = _sel_left(jnp.where(last, 1.0, 0.0).astype(BF16), acum)
    ea = jnp.exp(acum)
    e = e_ref[...]
    dtf_ref[...] = _sel_right(dt, e)
    tef_ref[...] = _sel_right(dt * jnp.exp(a_last - acum), e)
    eaf_ref[...] = _sel_right(ea, e)
    return acum, acum.T


def _ssd_diag_group(g, xc_ref, dtf_ref, acum, acum_t, causal):
    tb = TOKEN_BLOCK
    width = xc_ref.shape[1] - 2 * SSD_GROUPS * D_STATE
    gw = width // SSD_GROUPS
    bg = xc_ref[:, width + g * D_STATE:width + (g + 1) * D_STATE]
    cg = xc_ref[:, width + (SSD_GROUPS + g) * D_STATE:width + (SSD_GROUPS + g + 1) * D_STATE].astype(BF16)
    cb = jnp.where(causal, _dot_nt(cg, bg.astype(BF16)), 0.0)
    lane = _iota2((tb, LANES), 1)
    heads_per_group = gw // SSD_HEAD_DIM
    outs = []
    for jp in range(heads_per_group // 2):
        ms = []
        for hh in range(2):
            h = g * heads_per_group + 2 * jp + hh
            seg = jnp.minimum(acum[:, h:h + 1] - acum_t[h:h + 1, :], 0.0)
            ms.append((cb * jnp.exp(seg)).astype(BF16))
        lhs = jnp.concatenate(ms, axis=1)
        ps = slice(g * gw + jp * LANES, g * gw + (jp + 1) * LANES)
        xpair = (xc_ref[:, ps] * dtf_ref[:, ps]).astype(BF16)
        zero = jnp.zeros_like(xpair)
        rhs = jnp.concatenate([jnp.where(lane < SSD_HEAD_DIM, xpair, zero),
                               jnp.where(lane >= SSD_HEAD_DIM, xpair, zero)], axis=0)
        outs.append(_dot(lhs, rhs))
    return jnp.concatenate(outs, axis=1), cg, bg


def _finish_group(g, y, xc_ref, z_ref, dsk_ref, sn_ref, yb_ref):
    gw = yb_ref.shape[1] // SSD_GROUPS
    gs = slice(g * gw, (g + 1) * gw)
    y = y + dsk_ref[:, gs] * xc_ref[:, gs]
    yz = y * _silu(z_ref[:, gs].astype(F32))
    r = lax.rsqrt(jnp.mean(yz * yz, axis=-1, keepdims=True) + EPS)
    yb_ref[:, gs] = ((yz * r) * sn_ref[:, gs]).astype(yb_ref.dtype)


def _prompt_body(*refs, n_par):
    refs = list(refs)
    ins = [[refs.pop(0) for _ in range(5)] for _ in range(n_par)]
    gvn_ref, gw_ref, bs_ref, cw_ref, cb_ref, dtb_ref, alog_ref, dsk_ref, sn_ref = [refs.pop(0) for _ in range(9)]
    out4 = [refs.pop(0) for _ in range(4)]
    outs = [[o.at[p] for o in out4] for p in range(n_par)]
    wm_ref, e_ref = refs.pop(0), refs.pop(0)
    scr = [[refs.pop(0) for _ in range(6)] for _ in range(n_par)]
    tb = TOKEN_BLOCK
    c = pl.program_id(1)
    causal, last = _block_masks(tb)

    @pl.when((pl.program_id(0) == 0) & (c == 0))
    def _():
        _init_constants(wm_ref, e_ref, gw_ref, causal)

    @pl.when(c == 0)
    def _():
        for xp_ref, _, _, _, _, st_ref in scr:
            st_ref[...] = jnp.zeros_like(st_ref)
            xp_ref[0:SUBLANES, :] = jnp.zeros((SUBLANES, xp_ref.shape[1]), F32)

    for (u_ref, v_ref, z_ref, xbc_ref, dt_ref), (ya_ref, yb_ref, _, _), \
            (xp_ref, xc_ref, dtf_ref, tef_ref, eaf_ref, st_ref) in zip(ins, outs, scr):
        acum, acum_t = _ssd_prepare(dt_ref, dtb_ref, alog_ref, e_ref, causal, last, dtf_ref, tef_ref, eaf_ref)

        _gating(u_ref, v_ref, gvn_ref, bs_ref, wm_ref, ya_ref, None)

        xp_ref[SUBLANES:SUBLANES + tb, :] = xbc_ref[...]
        _conv(xp_ref, cw_ref, cb_ref, xc_ref, None, tb)
        xp_ref[0:SUBLANES, :] = xbc_ref[tb - SUBLANES:tb, :]

        gw = yb_ref.shape[1] // SSD_GROUPS
        for g in range(SSD_GROUPS):
            gs = slice(g * gw, (g + 1) * gw)
            yd, cg, bg = _ssd_diag_group(g, xc_ref, dtf_ref, acum, acum_t, causal)
            st = st_ref[:, gs]
            y = yd + _dot(cg, st.astype(BF16)) * eaf_ref[:, gs]
            _finish_group(g, y, xc_ref, z_ref, dsk_ref, sn_ref, yb_ref)
            xw = (xc_ref[:, gs] * tef_ref[:, gs]).astype(BF16)
            st_ref[:, gs] = eaf_ref[tb - 1:tb, gs] * st + _dot(bg.T.astype(BF16), xw)

    @pl.when(c == pl.num_programs(1) - 1)
    def _():
        for (_, _, _, xbc_ref, _), (_, _, hs_ref, ct_ref), (_, _, _, _, _, st_ref) in zip(ins, outs, scr):
            for k in range(st_ref.shape[1] // LANES):
                hs_ref[k * LANES:(k + 1) * LANES, :] = st_ref[:, k * LANES:(k + 1) * LANES].T
            ct_ref[...] = xbc_ref[tb - SUBLANES:tb, :]


def _sample_body(u_ref, v_ref, z_ref, xbc_ref, dt_ref, cst_ref, h0_ref,
                 gvn_ref, gw_ref, bs_ref, cw_ref, cb_ref, dtb_ref, alog_ref, dsk_ref, sn_ref,
                 ya_ref, yb_ref, vn_ref, hs_ref, cs_ref,
                 wm_ref, e_ref, xp_ref, xc_ref, dtf_ref, tef_ref, eaf_ref,
                 y_ref, yoff_ref, cg_ref, bg_ref, xwt_ref, eat_ref, cdh_ref, *, seq_len):
    tb = TOKEN_BLOCK
    k = pl.program_id(1)
    causal, last = _block_masks(seq_len)
    gw = yb_ref.shape[1] // SSD_GROUPS

    @pl.when((pl.program_id(0) == 0) & (k == 0))
    def _():
        _init_constants(wm_ref, e_ref, gw_ref, causal)
        xp_ref[0:SUBLANES, :] = jnp.zeros((SUBLANES, xp_ref.shape[1]), F32)

    @pl.when(k == 0)
    def _():
        acum, acum_t = _ssd_prepare(dt_ref, dtb_ref, alog_ref, e_ref, causal, last, dtf_ref, tef_ref, eaf_ref)
        eat_ref[...] = jnp.exp(acum_t)
        _gating(u_ref, v_ref, gvn_ref, bs_ref, wm_ref, ya_ref, vn_ref)
        xp_ref[SUBLANES:SUBLANES + tb, :] = xbc_ref[...]
        _conv(xp_ref, cw_ref, cb_ref, xc_ref, cst_ref, seq_len)
        nsb = tb // seq_len
        pick = [jnp.where(_iota2((nsb, tb), 1) == _iota2((nsb, tb), 0) * seq_len + (seq_len - (CONV_W - 1) + r),
                          1.0, 0.0).astype(BF16) for r in range(CONV_W - 1)]
        for c0 in range(0, xbc_ref.shape[1], CHUNK_COLS):
            pieces = _split3(xbc_ref[:, c0:c0 + CHUNK_COLS])
            for r in range(CONV_W - 1):
                cs_ref[r, :, c0:c0 + CHUNK_COLS] = ((_dot(pick[r], pieces[0]) + _dot(pick[r], pieces[1]))
                                             + _dot(pick[r], pieces[2]))
        for g in range(SSD_GROUPS):
            gs = slice(g * gw, (g + 1) * gw)
            yd, cg, bg = _ssd_diag_group(g, xc_ref, dtf_ref, acum, acum_t, causal)
            y_ref[:, gs] = yd
            cg_ref[:, g * D_STATE:(g + 1) * D_STATE] = cg.astype(F32)
            bg_ref[:, g * D_STATE:(g + 1) * D_STATE] = bg
            xw = xc_ref[:, gs] * tef_ref[:, gs]
            for q in range(gw // LANES):
                r0 = g * gw + q * LANES
                xwt_ref[r0:r0 + LANES, :] = xw[:, q * LANES:(q + 1) * LANES].T.astype(BF16)

    nseq = h0_ref.shape[0]
    nt = nseq * seq_len
    rows = pl.ds(pl.multiple_of(k * nt, nt), nt)
    rowseq = _iota2((tb, 1), 0) // seq_len
    subseq = _iota2((nt, 1), 0) // seq_len
    tok = _iota2((tb, LANES), 0)
    onehot = jnp.concatenate(
        [jnp.where(tok == (k * nseq + b) * seq_len + (seq_len - 1), 1.0, 0.0) for b in range(nseq)],
        axis=1).astype(BF16)
    cdh_ref[...] = _sel_right(eat_ref[...], onehot)
    hpg = gw // SSD_HEAD_DIM
    for g in range(SSD_GROUPS):
        gs = slice(g * gw, (g + 1) * gw)
        ds = slice(g * D_STATE, (g + 1) * D_STATE)
        c_sub = cg_ref[rows, ds]
        lhs = jnp.concatenate([jnp.where(subseq == b, c_sub, 0.0) for b in range(nseq)], axis=1).astype(BF16)
        h0cat = jnp.concatenate([h0_ref[b, gs, :].astype(BF16) for b in range(nseq)], axis=1)
        yoff_ref[rows, gs] = _dot_nt(lhs, h0cat)
        b_all = bg_ref[:, ds]
        bm = jnp.concatenate([jnp.where(rowseq == k * nseq + b, b_all, 0.0) for b in range(nseq)],
                             axis=1).astype(BF16)
        s_new = _dot(xwt_ref[gs, :], bm)
        for b in range(nseq):
            ls = slice(b * D_STATE, (b + 1) * D_STATE)
            for j in range(hpg):
                h = g * hpg + j
                r0 = g * gw + j * SSD_HEAD_DIM
                hs_ref[b, r0:r0 + SSD_HEAD_DIM, :] = (cdh_ref[h:h + 1, ls] * h0_ref[b, r0:r0 + SSD_HEAD_DIM, :]
                                                      + s_new[j * SSD_HEAD_DIM:(j + 1) * SSD_HEAD_DIM, ls])

    @pl.when(k == pl.num_programs(1) - 1)
    def _():
        for g in range(SSD_GROUPS):
            gs = slice(g * gw, (g + 1) * gw)
            y = y_ref[:, gs] + yoff_ref[:, gs] * eaf_ref[:, gs]
            _finish_group(g, y, xc_ref, z_ref, dsk_ref, sn_ref, yb_ref)


def _mixer_params(lp):
    (gm_v_norm, gm_ws, gm_bs, conv_w, conv_b, dt_bias, a_log, d_skip, ssd_norm) = lp
    nh = dt_bias.shape[0]
    pad = LANES - nh
    return dict(
        gvn=gm_v_norm.reshape(1, -1),
        cw=conv_w,
        cb=conv_b.reshape(1, -1),
        dtb=jnp.pad(dt_bias, (0, pad)).reshape(1, LANES),
        alog=jnp.pad(a_log, (0, pad)).reshape(1, LANES),
        dsk=jnp.repeat(d_skip, SSD_HEAD_DIM).reshape(1, -1),
        sn=ssd_norm.reshape(1, -1),
    )


def _full_spec(a):
    nd = a.ndim
    return pl.BlockSpec(a.shape, lambda i, j: (0,) * nd)


def _mixer_prompt(p1, xbc, dt, lp, n_seq, seq_len, gm_w, ssd_w, conv_dim):
    tb = TOKEN_BLOCK
    nc = seq_len // tb
    n_par = PROMPT_SEQS_PER_STEP if n_seq % PROMPT_SEQS_PER_STEP == 0 else 1
    mp = _mixer_params(lp)
    gw_full = lp[1]
    bs_t = lp[2].T
    params = [mp["gvn"], gw_full, bs_t, mp["cw"], mp["cb"], mp["dtb"], mp["alog"], mp["dsk"], mp["sn"]]
    in_specs, scratch, operands = [], [], []
    for p in range(n_par):
        row = lambda b, c, p=p: (b * n_par + p) * nc + c
        in_specs += [
            pl.BlockSpec((tb, gm_w), lambda b, c, row=row: (row(b, c), 0)),
            pl.BlockSpec((tb, gm_w), lambda b, c, row=row: (row(b, c), 1)),
            pl.BlockSpec((tb, ssd_w), lambda b, c, row=row: (row(b, c), 1)),
            pl.BlockSpec((tb, conv_dim), lambda b, c, row=row: (row(b, c), 0)),
            pl.BlockSpec((tb, LANES), lambda b, c, row=row: (row(b, c), 0)),
        ]
        operands += [p1, p1, p1, xbc, dt]
        scratch += [
            pltpu.VMEM((SUBLANES + tb, conv_dim), F32),
            pltpu.VMEM((tb, conv_dim), F32),
            pltpu.VMEM((tb, ssd_w), F32),
            pltpu.VMEM((tb, ssd_w), F32),
            pltpu.VMEM((tb, ssd_w), F32),
            pltpu.VMEM((D_STATE, ssd_w), F32),
        ]
    res = pl.pallas_call(
        functools.partial(_prompt_body, n_par=n_par),
        grid=(n_seq // n_par, nc),
        in_specs=in_specs + [_full_spec(a) for a in params],
        out_specs=[
            pl.BlockSpec((n_par, tb, gm_w), lambda b, c: (b, c, 0)),
            pl.BlockSpec((n_par, tb, ssd_w), lambda b, c: (b, c, 0)),
            pl.BlockSpec((n_par, ssd_w, D_STATE), lambda b, c: (b, 0, 0)),
            pl.BlockSpec((n_par, SUBLANES, conv_dim), lambda b, c: (b, 0, 0)),
        ],
        out_shape=[
            jax.ShapeDtypeStruct((n_seq, seq_len, gm_w), BF16),
            jax.ShapeDtypeStruct((n_seq, seq_len, ssd_w), BF16),
            jax.ShapeDtypeStruct((n_seq, ssd_w, D_STATE), F32),
            jax.ShapeDtypeStruct((n_seq, SUBLANES, conv_dim), F32),
        ],
        scratch_shapes=[
            pltpu.VMEM((GM_HEADS, tb, tb), BF16),
            pltpu.VMEM((LANES, ssd_w), BF16),
        ] + scratch,
        compiler_params=_cparams("arbitrary", "arbitrary"),
        name="mixer_prompt",
    )(*operands, *params)
    ya, yb, hs, ct = res
    return ya.reshape(n_seq * seq_len, gm_w), yb.reshape(n_seq * seq_len, ssd_w), hs, ct


def _mixer_sample(p1, xbc, dt, lp, h0, conv_state, row0, n_seq, seq_len, gm_w, ssd_w, conv_dim):
    tb = TOKEN_BLOCK
    seqs_per_block = tb // seq_len
    nblk = n_seq // seqs_per_block
    nsub = seqs_per_block // SAMPLE_SEQS_PER_STEP
    blk0 = row0 // tb
    mp = _mixer_params(lp)
    reps = tb // seq_len
    gw_tiled = jnp.tile(lp[1][:, :seq_len, :seq_len], (1, reps, reps))
    bs_t = jnp.tile(lp[2][:, :seq_len], (1, reps)).T
    params = [mp["gvn"], gw_tiled, bs_t, mp["cw"], mp["cb"], mp["dtb"], mp["alog"], mp["dsk"], mp["sn"]]
    ntok = n_seq * seq_len
    return pl.pallas_call(
        functools.partial(_sample_body, seq_len=seq_len),
        grid=(nblk, nsub),
        in_specs=[
            pl.BlockSpec((tb, gm_w), lambda i, k: (blk0 + i, 0)),
            pl.BlockSpec((tb, gm_w), lambda i, k: (blk0 + i, 1)),
            pl.BlockSpec((tb, ssd_w), lambda i, k: (blk0 + i, 1)),
            pl.BlockSpec((tb, conv_dim), lambda i, k: (blk0 + i, 0)),
            pl.BlockSpec((tb, LANES), lambda i, k: (blk0 + i, 0)),
            pl.BlockSpec((CONV_W - 1, seqs_per_block, conv_dim), lambda i, k: (0, i, 0)),
            pl.BlockSpec((SAMPLE_SEQS_PER_STEP, ssd_w, D_STATE), lambda i, k: (i * nsub + k, 0, 0)),
        ] + [_full_spec(a) for a in params],
        out_specs=[
            pl.BlockSpec((tb, gm_w), lambda i, k: (i, 0)),
            pl.BlockSpec((tb, ssd_w), lambda i, k: (i, 0)),
            pl.BlockSpec((tb, gm_w), lambda i, k: (i, 0)),
            pl.BlockSpec((SAMPLE_SEQS_PER_STEP, ssd_w, D_STATE), lambda i, k: (i * nsub + k, 0, 0)),
            pl.BlockSpec((CONV_W - 1, seqs_per_block, conv_dim), lambda i, k: (0, i, 0)),
        ],
        out_shape=[
            jax.ShapeDtypeStruct((ntok, gm_w), BF16),
            jax.ShapeDtypeStruct((ntok, ssd_w), BF16),
            jax.ShapeDtypeStruct((ntok, gm_w), F32),
            jax.ShapeDtypeStruct((n_seq, ssd_w, D_STATE), F32),
            jax.ShapeDtypeStruct((CONV_W - 1, n_seq, conv_dim), F32),
        ],
        scratch_shapes=[
            pltpu.VMEM((GM_HEADS, tb, tb), BF16),
            pltpu.VMEM((LANES, ssd_w), BF16),
            pltpu.VMEM((SUBLANES + tb, conv_dim), F32),
            pltpu.VMEM((tb, conv_dim), F32),
            pltpu.VMEM((tb, ssd_w), F32),
            pltpu.VMEM((tb, ssd_w), F32),
            pltpu.VMEM((tb, ssd_w), F32),
            pltpu.VMEM((tb, ssd_w), F32),
            pltpu.VMEM((tb, ssd_w), F32),
            pltpu.VMEM((tb, SSD_GROUPS * D_STATE), F32),
            pltpu.VMEM((tb, SSD_GROUPS * D_STATE), F32),
            pltpu.VMEM((ssd_w, tb), BF16),
            pltpu.VMEM((LANES, tb), F32),
            pltpu.VMEM((LANES, SAMPLE_SEQS_PER_STEP * LANES), F32),
        ],
        compiler_params=_cparams("arbitrary", "arbitrary"),
        name="mixer_sample",
    )(p1, p1, p1, xbc, dt, conv_state, h0, *params)


def kernel(x_prompt, x_sample, state_ssm, state_conv, ffn1_norm, ffn1_w1, ffn1_w3, ffn1_w2, mix_norm, w_in,
           b_gate, conv_w, conv_b, dt_bias, a_log, d_skip, ssd_norm, gm_v_norm, gm_ws, gm_bs, w_proj_a,
           w_proj_b, w_out, ffn2_norm, ffn2_w1, ffn2_w3, ffn2_w2, final_norm):
    bp, tp, d = x_prompt.shape
    bs_, ts, _ = x_sample.shape
    depth = w_in.shape[0]
    n_heads = dt_bias.shape[1]
    gm_w = gm_v_norm.shape[1]
    ssd_w = ssd_norm.shape[1]
    conv_dim = conv_w.shape[2]
    np_, ns = bp * tp, bs_ * ts

    xs = [x_prompt.reshape(np_, d), x_sample.reshape(ns, d)]
    ssm_p, conv_p, ssm_s, conv_s, v_s = [], [], [], [], []
    for l in range(depth):
        c_uvz = 2 * gm_w + ssd_w
        x, xn = _ffn(xs, ffn1_norm[l], ffn1_w1[l], ffn1_w3[l], ffn1_w2[l], next_norm_w=mix_norm[l])
        p1, xbc, dt, wa_b, wb_b, wo_b = _inproj(xn, w_in[l].T, c_uvz, conv_dim, n_heads,
                                                round_ws=(w_proj_a[l], w_proj_b[l], w_out[l]))

        lp = (gm_v_norm[l], gm_ws[l], gm_bs[l], conv_w[l], conv_b[l], dt_bias[l], a_log[l], d_skip[l], ssd_norm[l])
        ya_p, yb_p, hs_p, ct_p = _mixer_prompt(p1, xbc, dt, lp, bp, tp, gm_w, ssd_w, conv_dim)

        assert ts >= CONV_W - 1
        st = jnp.transpose(state_conv[l], (1, 0, 2))
        ya_s, yb_s, vn_s, hs_s, cs_s = _mixer_sample(p1, xbc, dt, lp, state_ssm[l].reshape(bs_, ssd_w, D_STATE), st,
                                                     np_, bs_, ts, gm_w, ssd_w, conv_dim)

        x, xn = _merge(x, ya_p, yb_p, ya_s, yb_s, p1, b_gate[l], wa_b, wb_b, wo_b, ffn2_norm[l])
        last = l == depth - 1
        xs = _ffn([x], ffn2_norm[l], ffn2_w1[l], ffn2_w3[l], ffn2_w2[l], xn=xn, final_w=final_norm if last else None,
                  split_rows=np_)

        ssm_p.append(hs_p.reshape(bp, n_heads, SSD_HEAD_DIM, D_STATE))
        conv_p.append(ct_p[:, SUBLANES - (CONV_W - 1):, :])
        ssm_s.append(hs_s.reshape(bs_, n_heads, SSD_HEAD_DIM, D_STATE))
        conv_s.append(jnp.transpose(cs_s, (1, 0, 2)))
        v_s.append(vn_s.reshape(bs_, ts, gm_w))

    return (xs[0].reshape(bp, tp, d), xs[1].reshape(bs_, ts, d), jnp.stack(ssm_p), jnp.stack(conv_p),
            jnp.stack(ssm_s), jnp.stack(conv_s), jnp.stack(v_s))
```
